```python
import jax, jax.numpy as jnp
from jax import lax
import numpy as np

D_MODEL = 2048
BATCH = 8
SEQ = 8192
DEPTH = 1

MLA_HEADS = 8
QK_NOPE_DIM = 128
QK_ROPE_DIM = 64
V_HEAD_DIM = 128
Q_LORA_RANK = 512
KV_LORA_RANK = 512
MLA_WIDTH = MLA_HEADS * V_HEAD_DIM
QK_HEAD_DIM = QK_NOPE_DIM + QK_ROPE_DIM
CONV_CHANNELS = D_MODEL - MLA_WIDTH
CONV_WIDTH = 31
CONV_PAD = CONV_WIDTH // 2
D_FF = 4 * D_MODEL
ROPE_BASE = 10000.0
Q_BLOCK = 128
LN_EPS = 1e-5
RMS_EPS = 1e-6
DEEPNORM_ALPHA = (2.0 * DEPTH) ** 0.25
DEEPNORM_BETA = (8.0 * DEPTH) ** -0.25
IN_COLS = Q_LORA_RANK + KV_LORA_RANK + QK_ROPE_DIM + 2 * CONV_CHANNELS

kernel_name = "hybrid_mla_conformer_deepnorm_encoder"


def layer_norm(x, g, b):
    xf = x.astype(jnp.float32)
    mu = jnp.mean(xf, axis=-1, keepdims=True)
    xc = xf - mu
    var = jnp.mean(jnp.square(xc), axis=-1, keepdims=True)
    y = xc * lax.rsqrt(var + LN_EPS)
    return (y * g.astype(jnp.float32) + b.astype(jnp.float32)).astype(x.dtype)


def rms_norm(x, g):
    xf = x.astype(jnp.float32)
    y = xf * lax.rsqrt(jnp.mean(jnp.square(xf), axis=-1, keepdims=True) + RMS_EPS)
    return (y * g.astype(jnp.float32)).astype(x.dtype)


def rope_tables(positions, dtype):
    half = QK_ROPE_DIM // 2
    inv_freq = ROPE_BASE ** (-jnp.arange(half, dtype=jnp.float32) * (2.0 / QK_ROPE_DIM))
    ang = positions.astype(jnp.float32)[..., None] * inv_freq
    return jnp.cos(ang).astype(dtype), jnp.sin(ang).astype(dtype)


def apply_rope(x, cos, sin):
    x1, x2 = jnp.split(x, 2, axis=-1)
    return jnp.concatenate([x1 * cos - x2 * sin, x2 * cos + x1 * sin], axis=-1)


def mla_attention(q_nope, q_rope, k_nope, k_rope, v):
    b, s, h, _ = q_nope.shape
    nb = s // Q_BLOCK
    scale = QK_HEAD_DIM ** -0.5
    qn = q_nope.reshape(b, nb, Q_BLOCK, h, QK_NOPE_DIM).transpose(1, 0, 2, 3, 4)
    qr = q_rope.reshape(b, nb, Q_BLOCK, h, QK_ROPE_DIM).transpose(1, 0, 2, 3, 4)

    def block(args):
        qn_b, qr_b = args
        scores = (jnp.einsum('bqhd,bkhd->bhqk', qn_b, k_nope)
                  + jnp.einsum('bqhr,bkr->bhqk', qr_b, k_rope))
        p = jax.nn.softmax(scores.astype(jnp.float32) * scale, axis=-1).astype(v.dtype)
        return jnp.einsum('bhqk,bkhd->bqhd', p, v)

    out = lax.map(block, (qn, qr))
    return out.transpose(1, 0, 2, 3, 4).reshape(b, s, h * V_HEAD_DIM)


def conformer_conv(u_in, conv_w, conv_b, g_ln, b_ln):
    a, gate = jnp.split(u_in, 2, axis=-1)
    u = a * jax.nn.sigmoid(gate)
    kern = conv_w.reshape(CONV_WIDTH, 1, CONV_CHANNELS).astype(u.dtype)
    u = lax.conv_general_dilated(
        u, kern, window_strides=(1,), padding=[(CONV_PAD, CONV_PAD)],
        dimension_numbers=('NWC', 'WIO', 'NWC'),
        feature_group_count=CONV_CHANNELS) + conv_b
    return jax.nn.silu(layer_norm(u, g_ln, b_ln))


def hybrid_layer(x, cos, sin, w_in, g_cq, w_uq, g_ckv, w_uk, w_uv, conv_w, conv_b,
                 g_conv_ln, b_conv_ln, w_out, g_ln1, b_ln1, w_ff1, w_ff2, g_ln2, b_ln2):
    b, s, _ = x.shape
    h = x @ w_in
    c_q, c_kv, k_rope, conv_in = jnp.split(
        h, [Q_LORA_RANK, Q_LORA_RANK + KV_LORA_RANK,
            Q_LORA_RANK + KV_LORA_RANK + QK_ROPE_DIM], axis=-1)
    q = (rms_norm(c_q, g_cq) @ w_uq).reshape(b, s, MLA_HEADS, QK_HEAD_DIM)
    q_nope, q_rope = jnp.split(q, [QK_NOPE_DIM], axis=-1)
    q_rope = apply_rope(q_rope, cos[:, :, None, :], sin[:, :, None, :])
    k_rope = apply_rope(k_rope, cos, sin)
    ckv = rms_norm(c_kv, g_ckv)
    k_nope = (ckv @ w_uk).reshape(b, s, MLA_HEADS, QK_NOPE_DIM)
    v = (ckv @ w_uv).reshape(b, s, MLA_HEADS, V_HEAD_DIM)
    attn_out = mla_attention(q_nope, q_rope, k_nope, k_rope, v)
    conv_out = conformer_conv(conv_in, conv_w, conv_b, g_conv_ln, b_conv_ln)
    mix = jnp.concatenate([attn_out, conv_out], axis=-1) @ w_out
    x = layer_norm(DEEPNORM_ALPHA * x + mix, g_ln1, b_ln1)
    ff = jnp.square(jax.nn.relu(x @ w_ff1)) @ w_ff2
    return layer_norm(DEEPNORM_ALPHA * x + ff, g_ln2, b_ln2)


def _fwd_setup_inputs(seed: int = 0) -> dict:
    key = jax.random.key(seed)
    ks = jax.random.split(key, 24)
    f32 = jnp.float32

    def nrm(k, shape, scale):
        return jax.random.normal(k, shape, f32) * scale

    def gain(k, shape):
        return 1.0 + 0.02 * jax.random.normal(k, shape, f32)

    L = DEPTH
    beta = DEEPNORM_BETA
    return {
        "x": jax.random.normal(ks[0], (BATCH, SEQ, D_MODEL), f32),
        "positions": jnp.broadcast_to(jnp.arange(SEQ, dtype=jnp.int32), (BATCH, SEQ)),
        "ln_in_g": gain(ks[1], (D_MODEL,)),
        "ln_in_b": nrm(ks[2], (D_MODEL,), 0.02),
        "w_in": nrm(ks[3], (L, D_MODEL, IN_COLS), D_MODEL ** -0.5),
        "g_cq": gain(ks[4], (L, Q_LORA_RANK)),
        "w_uq": nrm(ks[5], (L, Q_LORA_RANK, MLA_HEADS * QK_HEAD_DIM), Q_LORA_RANK ** -0.5),
        "g_ckv": gain(ks[6], (L, KV_LORA_RANK)),
        "w_uk": nrm(ks[7], (L, KV_LORA_RANK, MLA_HEADS * QK_NOPE_DIM), KV_LORA_RANK ** -0.5),
        "w_uv": nrm(ks[8], (L, KV_LORA_RANK, MLA_HEADS * V_HEAD_DIM), beta * KV_LORA_RANK ** -0.5),
        "conv_w": nrm(ks[9], (L, CONV_WIDTH, CONV_CHANNELS), CONV_WIDTH ** -0.5),
        "conv_b": nrm(ks[10], (L, CONV_CHANNELS), 0.02),
        "g_conv_ln": gain(ks[11], (L, CONV_CHANNELS)),
        "b_conv_ln": nrm(ks[12], (L, CONV_CHANNELS), 0.02),
        "w_out": nrm(ks[13], (L, D_MODEL, D_MODEL), beta * D_MODEL ** -0.5),
        "g_ln1": gain(ks[14], (L, D_MODEL)),
        "b_ln1": nrm(ks[15], (L, D_MODEL), 0.02),
        "w_ff1": nrm(ks[16], (L, D_MODEL, D_FF), beta * D_MODEL ** -0.5),
        "w_ff2": nrm(ks[17], (L, D_FF, D_MODEL), beta * D_FF ** -0.5),
        "g_ln2": gain(ks[18], (L, D_MODEL)),
        "b_ln2": nrm(ks[19], (L, D_MODEL), 0.02),
    }


def _fwd_reference(x, positions, ln_in_g, ln_in_b, w_in, g_cq, w_uq, g_ckv, w_uk, w_uv,
              conv_w, conv_b, g_conv_ln, b_conv_ln, w_out, g_ln1, b_ln1,
              w_ff1, w_ff2, g_ln2, b_ln2):
    cos, sin = rope_tables(positions, x.dtype)
    x = layer_norm(x, ln_in_g, ln_in_b)
    for l in range(DEPTH):
        x = hybrid_layer(x, cos, sin, w_in[l], g_cq[l], w_uq[l], g_ckv[l], w_uk[l], w_uv[l],
                         conv_w[l], conv_b[l], g_conv_ln[l], b_conv_ln[l], w_out[l],
                         g_ln1[l], b_ln1[l], w_ff1[l], w_ff2[l], g_ln2[l], b_ln2[l])
    return x


import jax as _jax
import jax.numpy as _jnp

TWIN_FORMAT = 'train_step'
FWD_PARAMS = ['x', 'positions', 'ln_in_g', 'ln_in_b', 'w_in', 'g_cq', 'w_uq', 'g_ckv', 'w_uk', 'w_uv', 'conv_w', 'conv_b', 'g_conv_ln', 'b_conv_ln', 'w_out', 'g_ln1', 'b_ln1', 'w_ff1', 'w_ff2', 'g_ln2', 'b_ln2']
TWIN_WEIGHTS = ['ln_in_g', 'ln_in_b', 'w_in', 'g_cq', 'w_uq', 'g_ckv', 'w_uk', 'w_uv', 'conv_w', 'conv_b', 'g_conv_ln', 'b_conv_ln', 'w_out', 'g_ln1', 'b_ln1', 'w_ff1', 'w_ff2', 'g_ln2', 'b_ln2']
TWIN_DIFF_INPUT = 'x'
TWIN_INPUTS = ['x', 'positions', 'ln_in_g', 'ln_in_b', 'w_in', 'g_cq', 'w_uq', 'g_ckv', 'w_uk', 'w_uv', 'conv_w', 'conv_b', 'g_conv_ln', 'b_conv_ln', 'w_out', 'g_ln1', 'b_ln1', 'w_ff1', 'w_ff2', 'g_ln2', 'b_ln2', 'loss_target', 'm_ln_in_g', 'm_ln_in_b', 'm_w_in', 'm_g_cq', 'm_w_uq', 'm_g_ckv', 'm_w_uk', 'm_w_uv', 'm_conv_w', 'm_conv_b', 'm_g_conv_ln', 'm_b_conv_ln', 'm_w_out', 'm_g_ln1', 'm_b_ln1', 'm_w_ff1', 'm_w_ff2', 'm_g_ln2', 'm_b_ln2', 'v_ln_in_g', 'v_ln_in_b', 'v_w_in', 'v_g_cq', 'v_w_uq', 'v_g_ckv', 'v_w_uk', 'v_w_uv', 'v_conv_w', 'v_conv_b', 'v_g_conv_ln', 'v_b_conv_ln', 'v_w_out', 'v_g_ln1', 'v_b_ln1', 'v_w_ff1', 'v_w_ff2', 'v_g_ln2', 'v_b_ln2']
TWIN_OUTPUTS = ['loss', 'grad_x', 'grad_ln_in_g', 'grad_ln_in_b', 'grad_w_in', 'grad_g_cq', 'grad_w_uq', 'grad_g_ckv', 'grad_w_uk', 'grad_w_uv', 'grad_conv_w', 'grad_conv_b', 'grad_g_conv_ln', 'grad_b_conv_ln', 'grad_w_out', 'grad_g_ln1', 'grad_b_ln1', 'grad_w_ff1', 'grad_w_ff2', 'grad_g_ln2', 'grad_b_ln2', 'delta_ln_in_g', 'delta_ln_in_b', 'delta_w_in', 'delta_g_cq', 'delta_w_uq', 'delta_g_ckv', 'delta_w_uk', 'delta_w_uv', 'delta_conv_w', 'delta_conv_b', 'delta_g_conv_ln', 'delta_b_conv_ln', 'delta_w_out', 'delta_g_ln1', 'delta_b_ln1', 'delta_w_ff1', 'delta_w_ff2', 'delta_g_ln2', 'delta_b_ln2', 'new_m_ln_in_g', 'new_m_ln_in_b', 'new_m_w_in', 'new_m_g_cq', 'new_m_w_uq', 'new_m_g_ckv', 'new_m_w_uk', 'new_m_w_uv', 'new_m_conv_w', 'new_m_conv_b', 'new_m_g_conv_ln', 'new_m_b_conv_ln', 'new_m_w_out', 'new_m_g_ln1', 'new_m_b_ln1', 'new_m_w_ff1', 'new_m_w_ff2', 'new_m_g_ln2', 'new_m_b_ln2', 'new_v_ln_in_g', 'new_v_ln_in_b', 'new_v_w_in', 'new_v_g_cq', 'new_v_w_uq', 'new_v_g_ckv', 'new_v_w_uk', 'new_v_w_uv', 'new_v_conv_w', 'new_v_conv_b', 'new_v_g_conv_ln', 'new_v_b_conv_ln', 'new_v_w_out', 'new_v_g_ln1', 'new_v_b_ln1', 'new_v_w_ff1', 'new_v_w_ff2', 'new_v_g_ln2', 'new_v_b_ln2']
TWIN_LEAF_KINDS = {'loss': 'loss', 'grad_x': 'grad_x', 'grad_ln_in_g': 'grad_w', 'grad_ln_in_b': 'grad_w', 'grad_w_in': 'grad_w', 'grad_g_cq': 'grad_w', 'grad_w_uq': 'grad_w', 'grad_g_ckv': 'grad_w', 'grad_w_uk': 'grad_w', 'grad_w_uv': 'grad_w', 'grad_conv_w': 'grad_w', 'grad_conv_b': 'grad_w', 'grad_g_conv_ln': 'grad_w', 'grad_b_conv_ln': 'grad_w', 'grad_w_out': 'grad_w', 'grad_g_ln1': 'grad_w', 'grad_b_ln1': 'grad_w', 'grad_w_ff1': 'grad_w', 'grad_w_ff2': 'grad_w', 'grad_g_ln2': 'grad_w', 'grad_b_ln2': 'grad_w', 'delta_ln_in_g': 'delta_w', 'delta_ln_in_b': 'delta_w', 'delta_w_in': 'delta_w', 'delta_g_cq': 'delta_w', 'delta_w_uq': 'delta_w', 'delta_g_ckv': 'delta_w', 'delta_w_uk': 'delta_w', 'delta_w_uv': 'delta_w', 'delta_conv_w': 'delta_w', 'delta_conv_b': 'delta_w', 'delta_g_conv_ln': 'delta_w', 'delta_b_conv_ln': 'delta_w', 'delta_w_out': 'delta_w', 'delta_g_ln1': 'delta_w', 'delta_b_ln1': 'delta_w', 'delta_w_ff1': 'delta_w', 'delta_w_ff2': 'delta_w', 'delta_g_ln2': 'delta_w', 'delta_b_ln2': 'delta_w', 'new_m_ln_in_g': 'new_m', 'new_m_ln_in_b': 'new_m', 'new_m_w_in': 'new_m', 'new_m_g_cq': 'new_m', 'new_m_w_uq': 'new_m', 'new_m_g_ckv': 'new_m', 'new_m_w_uk': 'new_m', 'new_m_w_uv': 'new_m', 'new_m_conv_w': 'new_m', 'new_m_conv_b': 'new_m', 'new_m_g_conv_ln': 'new_m', 'new_m_b_conv_ln': 'new_m', 'new_m_w_out': 'new_m', 'new_m_g_ln1': 'new_m', 'new_m_b_ln1': 'new_m', 'new_m_w_ff1': 'new_m', 'new_m_w_ff2': 'new_m', 'new_m_g_ln2': 'new_m', 'new_m_b_ln2': 'new_m', 'new_v_ln_in_g': 'new_v', 'new_v_ln_in_b': 'new_v', 'new_v_w_in': 'new_v', 'new_v_g_cq': 'new_v', 'new_v_w_uq': 'new_v', 'new_v_g_ckv': 'new_v', 'new_v_w_uk': 'new_v', 'new_v_w_uv': 'new_v', 'new_v_conv_w': 'new_v', 'new_v_conv_b': 'new_v', 'new_v_g_conv_ln': 'new_v', 'new_v_b_conv_ln': 'new_v', 'new_v_w_out': 'new_v', 'new_v_g_ln1': 'new_v', 'new_v_b_ln1': 'new_v', 'new_v_w_ff1': 'new_v', 'new_v_w_ff2': 'new_v', 'new_v_g_ln2': 'new_v', 'new_v_b_ln2': 'new_v'}


def _forward(args):
    return _fwd_reference(*[args[k] for k in FWD_PARAMS])


def _output_shape():
    def fwd():
        inp = _fwd_setup_inputs(0)
        return _fwd_reference(*[inp[k] for k in FWD_PARAMS])
    out = _jax.eval_shape(fwd)
    return out.shape, out.dtype

N_MICROBATCH = 1
ADAM_LR = 0.001
ADAM_B1 = 0.9
ADAM_B2 = 0.999
ADAM_EPS = 1e-08
ADAM_WD = 0.01
ADAM_STEP = 10
PER_EXAMPLE_BATCH_AXIS = {'x': 0, 'positions': 0, 'loss_target': 0}
SHARED_INPUTS = []
_WEIGHT_DTYPES = {'ln_in_g': _jnp.float32, 'ln_in_b': _jnp.float32, 'w_in': _jnp.float32, 'g_cq': _jnp.float32, 'w_uq': _jnp.float32, 'g_ckv': _jnp.float32, 'w_uk': _jnp.float32, 'w_uv': _jnp.float32, 'conv_w': _jnp.float32, 'conv_b': _jnp.float32, 'g_conv_ln': _jnp.float32, 'b_conv_ln': _jnp.float32, 'w_out': _jnp.float32, 'g_ln1': _jnp.float32, 'b_ln1': _jnp.float32, 'w_ff1': _jnp.float32, 'w_ff2': _jnp.float32, 'g_ln2': _jnp.float32, 'b_ln2': _jnp.float32}
MOMENT_SCALE = {'ln_in_g': 1.002565e+00, 'ln_in_b': 4.987917e-01, 'w_in': 2.376172e-02, 'g_cq': 5.860683e-03, 'w_uq': 3.192426e-03, 'g_ckv': 8.273679e-03, 'w_uk': 3.292805e-03, 'w_uv': 7.077197e-03, 'conv_w': 3.949840e-02, 'conv_b': 2.161620e-01, 'g_conv_ln': 8.268991e-02, 'b_conv_ln': 1.198767e-01, 'w_out': 6.230332e-02, 'g_ln1': 1.052485e+00, 'b_ln1': 4.981128e-01, 'w_ff1': 2.599356e-02, 'w_ff2': 7.087472e-02, 'g_ln2': 3.201522e+01, 'b_ln2': 3.683905e+00}


def _to_microbatches(a, axis):
    t = _jnp.moveaxis(a, axis, 0)
    t = t.reshape((N_MICROBATCH, t.shape[0] // N_MICROBATCH) + t.shape[1:])
    return _jnp.moveaxis(t, 1, axis + 1)


def setup_inputs(seed: int = 0) -> dict:
    inp = _fwd_setup_inputs(seed)
    key = _jax.random.fold_in(_jax.random.key(seed), 7919)
    shape, _ = _output_shape()
    out = dict(inp)
    out["loss_target"] = _jax.random.normal(_jax.random.fold_in(key, 0), shape, _jnp.float32)
    for i, name in enumerate(TWIN_WEIGHTS):
        w = inp[name].astype(_jnp.float32)
        if MOMENT_SCALE is None:
            s = _jnp.sqrt(_jnp.mean(_jnp.square(w)) + 1e-30)
        else:
            s = MOMENT_SCALE[name]
        km, kv = _jax.random.split(_jax.random.fold_in(key, i + 1))
        out[name] = w
        out["m_" + name] = s * _jax.random.normal(km, w.shape, _jnp.float32)
        out["v_" + name] = (s * s) * _jax.random.uniform(kv, w.shape, _jnp.float32, 0.5, 1.5)
    if N_MICROBATCH > 1:
        for name, axis in PER_EXAMPLE_BATCH_AXIS.items():
            out[name] = _to_microbatches(out[name], axis)
    return {'x': out['x'], 'positions': out['positions'], 'ln_in_g': out['ln_in_g'], 'ln_in_b': out['ln_in_b'], 'w_in': out['w_in'], 'g_cq': out['g_cq'], 'w_uq': out['w_uq'], 'g_ckv': out['g_ckv'], 'w_uk': out['w_uk'], 'w_uv': out['w_uv'], 'conv_w': out['conv_w'], 'conv_b': out['conv_b'], 'g_conv_ln': out['g_conv_ln'], 'b_conv_ln': out['b_conv_ln'], 'w_out': out['w_out'], 'g_ln1': out['g_ln1'], 'b_ln1': out['b_ln1'], 'w_ff1': out['w_ff1'], 'w_ff2': out['w_ff2'], 'g_ln2': out['g_ln2'], 'b_ln2': out['b_ln2'], 'loss_target': out['loss_target'], 'm_ln_in_g': out['m_ln_in_g'], 'm_ln_in_b': out['m_ln_in_b'], 'm_w_in': out['m_w_in'], 'm_g_cq': out['m_g_cq'], 'm_w_uq': out['m_w_uq'], 'm_g_ckv': out['m_g_ckv'], 'm_w_uk': out['m_w_uk'], 'm_w_uv': out['m_w_uv'], 'm_conv_w': out['m_conv_w'], 'm_conv_b': out['m_conv_b'], 'm_g_conv_ln': out['m_g_conv_ln'], 'm_b_conv_ln': out['m_b_conv_ln'], 'm_w_out': out['m_w_out'], 'm_g_ln1': out['m_g_ln1'], 'm_b_ln1': out['m_b_ln1'], 'm_w_ff1': out['m_w_ff1'], 'm_w_ff2': out['m_w_ff2'], 'm_g_ln2': out['m_g_ln2'], 'm_b_ln2': out['m_b_ln2'], 'v_ln_in_g': out['v_ln_in_g'], 'v_ln_in_b': out['v_ln_in_b'], 'v_w_in': out['v_w_in'], 'v_g_cq': out['v_g_cq'], 'v_w_uq': out['v_w_uq'], 'v_g_ckv': out['v_g_ckv'], 'v_w_uk': out['v_w_uk'], 'v_w_uv': out['v_w_uv'], 'v_conv_w': out['v_conv_w'], 'v_conv_b': out['v_conv_b'], 'v_g_conv_ln': out['v_g_conv_ln'], 'v_b_conv_ln': out['v_b_conv_ln'], 'v_w_out': out['v_w_out'], 'v_g_ln1': out['v_g_ln1'], 'v_b_ln1': out['v_b_ln1'], 'v_w_ff1': out['v_w_ff1'], 'v_w_ff2': out['v_w_ff2'], 'v_g_ln2': out['v_g_ln2'], 'v_b_ln2': out['v_b_ln2']}


def _loss(weights, diff, rest, loss_target):
    with _jax.named_scope("forward"):
        args = {**rest, TWIN_DIFF_INPUT: diff, **{k: w.astype(_WEIGHT_DTYPES[k]) for k, w in weights.items()}}
        y = _forward(args)
    with _jax.named_scope("loss_head"):
        err = _jnp.square(y.astype(_jnp.float32) - loss_target)
        return 0.5 * _jnp.sum(_jnp.mean(err, axis=-1)) if err.ndim else 0.5 * err


def _adamw(w, g, m, v):
    m = ADAM_B1 * m + (1.0 - ADAM_B1) * g
    v = ADAM_B2 * v + (1.0 - ADAM_B2) * _jnp.square(g)
    m_hat = m / (1.0 - ADAM_B1 ** ADAM_STEP)
    v_hat = v / (1.0 - ADAM_B2 ** ADAM_STEP)
    delta = -ADAM_LR * (m_hat / (_jnp.sqrt(v_hat) + ADAM_EPS) + ADAM_WD * w)
    return delta, m, v


def reference(x, positions, ln_in_g, ln_in_b, w_in, g_cq, w_uq, g_ckv, w_uk, w_uv, conv_w, conv_b, g_conv_ln, b_conv_ln, w_out, g_ln1, b_ln1, w_ff1, w_ff2, g_ln2, b_ln2, loss_target, m_ln_in_g, m_ln_in_b, m_w_in, m_g_cq, m_w_uq, m_g_ckv, m_w_uk, m_w_uv, m_conv_w, m_conv_b, m_g_conv_ln, m_b_conv_ln, m_w_out, m_g_ln1, m_b_ln1, m_w_ff1, m_w_ff2, m_g_ln2, m_b_ln2, v_ln_in_g, v_ln_in_b, v_w_in, v_g_cq, v_w_uq, v_g_ckv, v_w_uk, v_w_uv, v_conv_w, v_conv_b, v_g_conv_ln, v_b_conv_ln, v_w_out, v_g_ln1, v_b_ln1, v_w_ff1, v_w_ff2, v_g_ln2, v_b_ln2):
    given = dict(x=x, positions=positions, ln_in_g=ln_in_g, ln_in_b=ln_in_b, w_in=w_in, g_cq=g_cq, w_uq=w_uq, g_ckv=g_ckv, w_uk=w_uk, w_uv=w_uv, conv_w=conv_w, conv_b=conv_b, g_conv_ln=g_conv_ln, b_conv_ln=b_conv_ln, w_out=w_out, g_ln1=g_ln1, b_ln1=b_ln1, w_ff1=w_ff1, w_ff2=w_ff2, g_ln2=g_ln2, b_ln2=b_ln2, loss_target=loss_target, m_ln_in_g=m_ln_in_g, m_ln_in_b=m_ln_in_b, m_w_in=m_w_in, m_g_cq=m_g_cq, m_w_uq=m_w_uq, m_g_ckv=m_g_ckv, m_w_uk=m_w_uk, m_w_uv=m_w_uv, m_conv_w=m_conv_w, m_conv_b=m_conv_b, m_g_conv_ln=m_g_conv_ln, m_b_conv_ln=m_b_conv_ln, m_w_out=m_w_out, m_g_ln1=m_g_ln1, m_b_ln1=m_b_ln1, m_w_ff1=m_w_ff1, m_w_ff2=m_w_ff2, m_g_ln2=m_g_ln2, m_b_ln2=m_b_ln2, v_ln_in_g=v_ln_in_g, v_ln_in_b=v_ln_in_b, v_w_in=v_w_in, v_g_cq=v_g_cq, v_w_uq=v_w_uq, v_g_ckv=v_g_ckv, v_w_uk=v_w_uk, v_w_uv=v_w_uv, v_conv_w=v_conv_w, v_conv_b=v_conv_b, v_g_conv_ln=v_g_conv_ln, v_b_conv_ln=v_b_conv_ln, v_w_out=v_w_out, v_g_ln1=v_g_ln1, v_b_ln1=v_b_ln1, v_w_ff1=v_w_ff1, v_w_ff2=v_w_ff2, v_g_ln2=v_g_ln2, v_b_ln2=v_b_ln2)
    weights = {n: given[n] for n in TWIN_WEIGHTS}
    shared = {n: given[n] for n in SHARED_INPUTS}
    per_example = {n: given[n] for n in ['x', 'positions']}
    grad_fn = _jax.value_and_grad(_loss, argnums=(0, 1))

    def one_microbatch(ex, loss_target):
        ex = dict(ex)
        diff = ex.pop(TWIN_DIFF_INPUT)
        return grad_fn(weights, diff, {**shared, **ex}, loss_target)

    if N_MICROBATCH == 1:
        loss, (grad_w, grad_x) = one_microbatch(per_example, given["loss_target"])
    else:
        def body(carry, xs):
            loss_sum, grad_sum = carry
            l_k, (gw_k, gx_k) = one_microbatch(xs[0], xs[1])
            with _jax.named_scope("update"):
                return (loss_sum + l_k, _jax.tree.map(_jnp.add, grad_sum, gw_k)), gx_k

        init = (_jnp.zeros((), _jnp.float32), _jax.tree.map(_jnp.zeros_like, weights))
        (loss, grad_w), grad_x = _jax.lax.scan(body, init, (per_example, given["loss_target"]))
    with _jax.named_scope("update"):
        delta_w, new_m, new_v = {}, {}, {}
        for n in TWIN_WEIGHTS:
            delta_w[n], new_m[n], new_v[n] = _adamw(weights[n], grad_w[n], given["m_" + n], given["v_" + n])
    return (loss, grad_x, *[grad_w[n] for n in TWIN_WEIGHTS], *[delta_w[n] for n in TWIN_WEIGHTS],
            *[new_m[n] for n in TWIN_WEIGHTS], *[new_v[n] for n in TWIN_WEIGHTS])
```

```python
import jax
import jax.numpy as jnp
from jax import lax
from jax.experimental import pallas as pl
from jax.experimental.pallas import tpu as pltpu

F32 = jnp.float32
BF16 = jnp.bfloat16

N_HEADS = 8
D_NOPE = 128
D_ROPE = 64
D_V = 128
D_QK = D_NOPE + D_ROPE
D_HEAD_PAD = 256
LORA = 512
CONV_W = 31
CONV_HALF = CONV_W // 2
CONV_W_PAD = 32
HALO = 16
LN_EPS = 1e-5
RMS_EPS = 1e-6
ALPHA = 2.0 ** 0.25
SCALE = float(D_QK) ** -0.5
ROPE_BASE = 10000.0
ADAM_LR, ADAM_B1, ADAM_B2, ADAM_EPS, ADAM_WD, ADAM_STEP = 0.001, 0.9, 0.999, 1e-08, 0.01, 10

N_DEV = 8
LANES = 128
VMEM_LIMIT_V7X = 56 * 1024 * 1024

NN = (((1,), (0,)), ((), ()))
NT = (((1,), (1,)), ((), ()))
TN = (((0,), (0,)), ((), ()))


def _call(body, *, name, grid, in_specs, out_specs, out_shape, scratch=()):
    params = pltpu.CompilerParams(dimension_semantics=("arbitrary",) * len(grid),
                                  vmem_limit_bytes=VMEM_LIMIT_V7X)
    return pl.pallas_call(body, name=name, grid=grid, in_specs=in_specs, out_specs=out_specs,
                          out_shape=out_shape, scratch_shapes=scratch, compiler_params=params)


def _tile(n, pref):
    if n <= pref:
        return n
    t = (pref // LANES) * LANES
    while t > LANES and n % t:
        t -= LANES
    assert n % t == 0, (n, pref)
    return t


def _sds(shape, dtype):
    return jax.ShapeDtypeStruct(shape, dtype)


def _ln_stats(z):
    mu = jnp.mean(z, axis=-1, keepdims=True)
    zc = z - mu
    var = jnp.mean(zc * zc, axis=-1, keepdims=True)
    rstd = lax.rsqrt(var + LN_EPS)
    return zc * rstd, rstd


def _ln_bwd(dy, xhat, rstd, g):
    gd = dy * g
    m1 = jnp.mean(gd, axis=-1, keepdims=True)
    m2 = jnp.mean(gd * xhat, axis=-1, keepdims=True)
    return rstd * (gd - m1 - xhat * m2)


def _rms(x, g):
    return x * lax.rsqrt(jnp.mean(x * x, axis=-1, keepdims=True) + RMS_EPS) * g


def _rms_bwd(dy, x, g):
    r = lax.rsqrt(jnp.mean(x * x, axis=-1, keepdims=True) + RMS_EPS)
    dxn = dy * g
    dx = r * dxn - x * (r * r * r) * jnp.mean(dxn * x, axis=-1, keepdims=True)
    dg = jnp.sum(dy * x * r, axis=0, keepdims=True)
    return dx, dg


def _sigmoid(x):
    return 1.0 / (1.0 + jnp.exp(-x))


def _rope(x, cos_p, sin_a, sin_b):
    return x * cos_p + pltpu.roll(x, 96, 1) * sin_a + pltpu.roll(x, 32, 1) * sin_b


def _unrope(d, cos_p, sin_a, sin_b):
    return d * cos_p - pltpu.roll(d, 96, 1) * sin_a - pltpu.roll(d, 32, 1) * sin_b


def _colsum(v):
    return jnp.sum(v, axis=0, keepdims=True)


def _acc_out(ref, val, first):
    @pl.when(first)
    def _():
        ref[...] = val

    @pl.when(jnp.logical_not(first))
    def _():
        ref[...] += val


def _matmul(name, a, b, *, dims, grid, a_spec, b_spec, acc_shape, outs, epilogue, extras=()):
    nk = grid[2]
    ne, no = len(extras), len(outs)

    def body(*refs):
        a_ref, b_ref = refs[0], refs[1]
        ex = refs[2:2 + ne]
        out = refs[2 + ne:2 + ne + no]
        part = lax.dot_general(a_ref[...], b_ref[...], dims, preferred_element_type=F32)
        if nk == 1:
            epilogue(part, ex, out)
        else:
            acc = refs[2 + ne + no]
            k = pl.program_id(2)

            @pl.when(k == 0)
            def _():
                acc[...] = part

            @pl.when(k > 0)
            def _():
                acc[...] += part

            @pl.when(k == nk - 1)
            def _():
                epilogue(acc[...], ex, out)

    scratch = [] if nk == 1 else [pltpu.VMEM(acc_shape, F32)]
    res = _call(body, name=name, grid=grid,
                in_specs=[a_spec, b_spec] + [s for _, s in extras],
                out_specs=[s for _, s in outs],
                out_shape=[o for o, _ in outs],
                scratch=scratch)(a, b, *[e for e, _ in extras])
    return res


def _store(dtype=F32):
    def ep(acc, ex, out):
        out[0][...] = acc.astype(dtype)
    return ep


def _mesh_pos():
    return lax.axis_index("x"), lax.axis_index("y"), lax.axis_index("c")


def _flip(v, bit):
    return 1 - v if bit else v


def _all_gather(shards):
    n = len(shards)

    def body(*refs):
        ins, outs = refs[:n], refs[n:2 * n]
        send_sems, recv_sems, local_sems = refs[2 * n:]
        x, y, c = _mesh_pos()
        me, sibling = (x, y, c), (x, y, 1 - c)
        chips = [(1 - x, y), (x, 1 - y), (1 - x, 1 - y)]

        def slot(p):
            return 4 * p[0] + 2 * p[1] + p[2]

        def copy(w, k, block, to, src=None):
            dst = outs[w].at[slot(block)]
            return pltpu.make_async_remote_copy(
                src_ref=dst if src is None else src, dst_ref=dst,
                send_sem=send_sems.at[w, k], recv_sem=recv_sems.at[w, k],
                device_id=to, device_id_type=pl.DeviceIdType.MESH)

        mine = [pltpu.make_async_copy(ins[w], outs[w].at[slot(me)], local_sems.at[w]) for w in range(n)]
        for cp in mine:
            cp.start()
        first = []
        for w in range(n):
            first.append(copy(w, 0, me, sibling, src=ins[w]))
            for j, chip in enumerate(chips):
                first.append(copy(w, 1 + j, me, (*chip, c), src=ins[w]))
        for cp in first:
            cp.start()
        passed = []
        for w in range(n):
            for j, chip in enumerate(chips):
                copy(w, 1 + j, (*chip, c), me).wait_recv()
                fwd = copy(w, 4 + j, (*chip, c), sibling)
                fwd.start()
                passed.append(fwd)
        for w in range(n):
            copy(w, 0, sibling, me).wait_recv()
            for j, chip in enumerate(chips):
                copy(w, 4 + j, (*chip, 1 - c), me).wait_recv()
        for cp in first + passed:
            cp.wait_send()
        for cp in mine:
            cp.wait()

    any_spec = pl.BlockSpec(memory_space=pl.ANY)
    return pl.pallas_call(
        body, name="weights_all_gather",
        in_specs=[any_spec] * n, out_specs=[any_spec] * n,
        out_shape=[_sds((N_DEV,) + s.shape, s.dtype) for s in shards],
        scratch_shapes=[pltpu.SemaphoreType.DMA((n, 7)), pltpu.SemaphoreType.DMA((n, 7)),
                        pltpu.SemaphoreType.DMA((n,))],
    )(*shards)


def _reduce_scatter_exchange(stacked, gathered):
    ns, ng = len(stacked), len(gathered)
    n = ns + ng

    def body(*refs):
        ins, outs = refs[:n], refs[n:2 * n]
        send_sems, recv_sems, local_sems = refs[2 * n:]
        x, y, c = _mesh_pos()
        me = 4 * x + 2 * y + c

        def src_of(w, dev_slot):
            return ins[w].at[dev_slot] if w < ns else ins[w]

        local = [pltpu.make_async_copy(src_of(w, me), outs[w].at[me], local_sems.at[w]) for w in range(n)]
        for cp in local:
            cp.start()
        sends, recvs = [], []
        for w in range(n):
            for k in range(1, N_DEV):
                peer = (_flip(x, k & 4), _flip(y, k & 2), _flip(c, k & 1))
                peer_slot = 4 * peer[0] + 2 * peer[1] + peer[2]
                sends.append(pltpu.make_async_remote_copy(
                    src_ref=src_of(w, peer_slot), dst_ref=outs[w].at[me],
                    send_sem=send_sems.at[w, k - 1], recv_sem=recv_sems.at[w, k - 1],
                    device_id=peer, device_id_type=pl.DeviceIdType.MESH))
                recvs.append(pltpu.make_async_remote_copy(
                    src_ref=src_of(w, me), dst_ref=outs[w].at[peer_slot],
                    send_sem=send_sems.at[w, k - 1], recv_sem=recv_sems.at[w, k - 1],
                    device_id=peer, device_id_type=pl.DeviceIdType.MESH))
        for cp in sends:
            cp.start()
        for cp in recvs:
            cp.wait_recv()
        for cp in sends:
            cp.wait_send()
        for cp in local:
            cp.wait()

    any_spec = pl.BlockSpec(memory_space=pl.ANY)
    arrays = list(stacked) + list(gathered)
    out_shape = [_sds(a.shape, a.dtype) for a in stacked] + [_sds((N_DEV,) + a.shape, a.dtype) for a in gathered]
    return pl.pallas_call(
        body, name="grads_reduce_scatter",
        in_specs=[any_spec] * n, out_specs=[any_spec] * n, out_shape=out_shape,
        scratch_shapes=[pltpu.SemaphoreType.DMA((n, 7)), pltpu.SemaphoreType.DMA((n, 7)),
                        pltpu.SemaphoreType.DMA((n,))],
    )(*arrays)


def _adamw(name, parts, w, m, v):
    rows, cols = w.shape
    cap = max(8, (LANES * 1024) // cols)
    tr = rows
    if rows > cap:
        tr = (cap // 8) * 8
        while rows % tr:
            tr -= 8
    c1 = 1.0 / (1.0 - ADAM_B1 ** ADAM_STEP)
    c2 = 1.0 / (1.0 - ADAM_B2 ** ADAM_STEP)

    def body(p_ref, w_ref, m_ref, v_ref, g_o, d_o, m_o, v_o):
        g = p_ref[0]
        for s in range(1, N_DEV):
            g = g + p_ref[s]
        mn = ADAM_B1 * m_ref[...] + (1.0 - ADAM_B1) * g
        vn = ADAM_B2 * v_ref[...] + (1.0 - ADAM_B2) * (g * g)
        g_o[...] = g
        m_o[...] = mn
        v_o[...] = vn
        d_o[...] = -ADAM_LR * ((mn * c1) / (jnp.sqrt(vn * c2) + ADAM_EPS) + ADAM_WD * w_ref[...])

    blk = pl.BlockSpec((tr, cols), lambda i: (i, 0))
    return _call(body, name=name, grid=(rows // tr,),
                 in_specs=[pl.BlockSpec((N_DEV, tr, cols), lambda i: (0, i, 0)), blk, blk, blk],
                 out_specs=[blk] * 4, out_shape=[_sds((rows, cols), F32)] * 4)(parts, w, m, v)


def _rope_tables(pos_col, inv_freq):
    T = pos_col.shape[0]
    tm = _tile(T, 1024)

    def body(p_ref, f_ref, c_o, sa_o, sb_o):
        ang = p_ref[...].astype(F32) * f_ref[...]
        lane = lax.broadcasted_iota(jnp.int32, ang.shape, 1)
        cs, sn = jnp.cos(ang), jnp.sin(ang)
        c_o[...] = jnp.where(lane < D_ROPE, cs, 0.0)
        sa_o[...] = jnp.where(lane < D_ROPE // 2, -sn, 0.0)
        sb_o[...] = jnp.where((lane >= D_ROPE // 2) & (lane < D_ROPE), sn, 0.0)

    blk = pl.BlockSpec((tm, LANES), lambda i: (i, 0))
    return _call(body, name="rope_tables", grid=(T // tm,),
                 in_specs=[pl.BlockSpec((tm, 1), lambda i: (i, 0)), pl.BlockSpec((1, LANES), lambda i: (0, 0))],
                 out_specs=[blk] * 3, out_shape=[_sds((T, LANES), F32)] * 3)(pos_col, inv_freq)


def _ln_in(x, g, b):
    T, D = x.shape
    tm = _tile(T, 512)

    def body(x_ref, g_ref, b_ref, o32, o16):
        xhat, _ = _ln_stats(x_ref[...])
        y = xhat * g_ref[...] + b_ref[...]
        o32[...] = y
        o16[...] = y.astype(BF16)

    blk = pl.BlockSpec((tm, D), lambda i: (i, 0))
    vec = pl.BlockSpec((1, D), lambda i: (0, 0))
    return _call(body, name="ln_in", grid=(T // tm,), in_specs=[blk, vec, vec], out_specs=[blk, blk],
                 out_shape=[_sds((T, D), F32), _sds((T, D), BF16)])(x, g, b)


def _mid(h, g_cq, g_ckv, tabs, C):
    T = h.shape[0]
    tm = _tile(T, 256)
    cq_blk, kr_blk = (2 * C) // LORA, (2 * C + 2 * LORA) // LANES

    def body(a_ref, gt_ref, cq_ref, ckv_ref, kr_ref, gq_ref, gkv_ref, cp, sa, sb, u_o, cqn_o, ckvn_o, kr_o):
        u_o[...] = a_ref[...] * _sigmoid(gt_ref[...])
        cqn_o[...] = _rms(cq_ref[...], gq_ref[...]).astype(BF16)
        ckvn_o[...] = _rms(ckv_ref[...], gkv_ref[...]).astype(BF16)
        kr_o[...] = _rope(kr_ref[...], cp[...], sa[...], sb[...]).astype(BF16)

    def col(w, j):
        return pl.BlockSpec((tm, w), lambda i: (i, j))

    vec = pl.BlockSpec((1, LORA), lambda i: (0, 0))
    return _call(body, name="mid_norm_glu", grid=(T // tm,),
                 in_specs=[col(C, 0), col(C, 1), col(LORA, cq_blk), col(LORA, cq_blk + 1), col(LANES, kr_blk),
                           vec, vec, col(LANES, 0), col(LANES, 0), col(LANES, 0)],
                 out_specs=[col(C, 0), col(LORA, 0), col(LORA, 0), col(LANES, 0)],
                 out_shape=[_sds((T, C), F32), _sds((T, LORA), BF16), _sds((T, LORA), BF16), _sds((T, LANES), BF16)],
                 )(h, h, h, h, h, g_cq, g_ckv, *tabs)


def _q_proj(cqn, w_uq_p, tabs):
    T = cqn.shape[0]
    tm = _tile(T, 512)

    def body(c_ref, w_ref, cp, sa, sb, o_ref):
        q = jnp.dot(c_ref[...], w_ref[...], preferred_element_type=F32)
        o_ref[:, :D_NOPE] = (q[:, :D_NOPE] * SCALE).astype(BF16)
        o_ref[:, D_NOPE:] = (_rope(q[:, D_NOPE:], cp[...], sa[...], sb[...]) * SCALE).astype(BF16)

    tab = pl.BlockSpec((tm, LANES), lambda i, h: (i, 0))
    return _call(body, name="q_proj_rope", grid=(T // tm, N_HEADS),
                 in_specs=[pl.BlockSpec((tm, LORA), lambda i, h: (i, 0)),
                           pl.BlockSpec((None, LORA, D_HEAD_PAD), lambda i, h: (h, 0, 0)), tab, tab, tab],
                 out_specs=pl.BlockSpec((None, tm, D_HEAD_PAD), lambda i, h: (h, i, 0)),
                 out_shape=_sds((N_HEADS, T, D_HEAD_PAD), BF16))(cqn, w_uq_p, *tabs)


def _kv_proj(ckvn, w_ukv, kr):
    T = ckvn.shape[0]
    tm = _tile(T, 512)

    def body(c_ref, w_ref, kr_ref, k_o, v_o):
        kv = jnp.dot(c_ref[...], w_ref[...], preferred_element_type=F32)
        k_o[:, :D_NOPE] = kv[:, :D_NOPE].astype(BF16)
        k_o[:, D_NOPE:] = kr_ref[...]
        v_o[...] = kv[:, D_NOPE:].astype(BF16)

    return _call(body, name="kv_proj", grid=(T // tm, N_HEADS),
                 in_specs=[pl.BlockSpec((tm, LORA), lambda i, h: (i, 0)),
                           pl.BlockSpec((None, LORA, D_NOPE + D_V), lambda i, h: (h, 0, 0)),
                           pl.BlockSpec((tm, LANES), lambda i, h: (i, 0))],
                 out_specs=[pl.BlockSpec((None, tm, D_HEAD_PAD), lambda i, h: (h, i, 0)),
                            pl.BlockSpec((None, tm, D_V), lambda i, h: (h, i, 0))],
                 out_shape=[_sds((N_HEADS, T, D_HEAD_PAD), BF16), _sds((N_HEADS, T, D_V), BF16)])(ckvn, w_ukv, kr)


def _flash_fwd(q, k, v):
    _, T, _ = q.shape
    tq, tk = _tile(T, 512), _tile(T, 512)
    nkv = T // tk

    def body(q_ref, k_ref, v_ref, o_ref, lse_ref, m_sc, l_sc, acc_sc):
        j = pl.program_id(2)

        @pl.when(j == 0)
        def _():
            m_sc[...] = jnp.full_like(m_sc, -jnp.inf)
            l_sc[...] = jnp.zeros_like(l_sc)
            acc_sc[...] = jnp.zeros_like(acc_sc)

        s = lax.dot_general(q_ref[...], k_ref[...], NT, preferred_element_type=F32)
        m_prev = m_sc[...]
        m_new = jnp.maximum(m_prev, jnp.max(s, axis=1, keepdims=True))
        a = jnp.exp(m_prev - m_new)
        p = jnp.exp(s - m_new)
        l_sc[...] = a * l_sc[...] + jnp.sum(p, axis=1, keepdims=True)
        acc_sc[...] = a * acc_sc[...] + jnp.dot(p.astype(BF16), v_ref[...], preferred_element_type=F32)
        m_sc[...] = m_new

        @pl.when(j == nkv - 1)
        def _():
            o_ref[...] = (acc_sc[...] / l_sc[...]).astype(BF16)
            lse_ref[...] = m_sc[...] + jnp.log(l_sc[...])

    return _call(body, name="flash_fwd", grid=(N_HEADS, T // tq, nkv),
                 in_specs=[pl.BlockSpec((None, tq, D_HEAD_PAD), lambda h, i, j: (h, i, 0)),
                           pl.BlockSpec((None, tk, D_HEAD_PAD), lambda h, i, j: (h, j, 0)),
                           pl.BlockSpec((None, tk, D_V), lambda h, i, j: (h, j, 0))],
                 out_specs=[pl.BlockSpec((tq, D_V), lambda h, i, j: (i, h)),
                            pl.BlockSpec((None, tq, 1), lambda h, i, j: (h, i, 0))],
                 out_shape=[_sds((T, N_HEADS * D_V), BF16), _sds((N_HEADS, T, 1), F32)],
                 scratch=[pltpu.VMEM((tq, 1), F32), pltpu.VMEM((tq, 1), F32), pltpu.VMEM((tq, D_V), F32)])(q, k, v)


def _halo_specs(tm, cb, n_t):
    r = tm // HALO
    return [pl.BlockSpec((HALO, cb), lambda jc, i: (jnp.maximum(i * r - 1, 0), jc)),
            pl.BlockSpec((tm, cb), lambda jc, i: (i, jc)),
            pl.BlockSpec((HALO, cb), lambda jc, i: (jnp.minimum((i + 1) * r, n_t * r - 1), jc))]


def _fill_ext(ext, prev_ref, cur_ref, next_ref, i, n_t, tm):
    ext[0:HALO, :] = jnp.where(i > 0, prev_ref[...], 0.0)
    ext[HALO:HALO + tm, :] = cur_ref[...]
    ext[HALO + tm:, :] = jnp.where(i < n_t - 1, next_ref[...], 0.0)


CONV_ROWS = 64


def _conv_fwd(u, w_pad, bias):
    T, C = u.shape
    tm, cb = _tile(T, 256), _tile(C, 256)
    n_t = T // tm
    rb = min(CONV_ROWS, tm)

    def body(up, uc, un, w_ref, b_ref, c_o, ext):
        i = pl.program_id(1)
        _fill_ext(ext, up, uc, un, i, n_t, tm)
        for r0 in range(0, tm, rb):
            acc = jnp.zeros((rb, cb), F32) + b_ref[...]
            for k in range(CONV_W):
                acc = acc + w_ref[k:k + 1, :] * ext[r0 + k + 1:r0 + k + 1 + rb, :]
            c_o[r0:r0 + rb, :] = acc

    return _call(body, name="conv_fwd", grid=(C // cb, n_t),
                 in_specs=_halo_specs(tm, cb, n_t) + [pl.BlockSpec((CONV_W_PAD, cb), lambda jc, i: (0, jc)),
                                                      pl.BlockSpec((1, cb), lambda jc, i: (0, jc))],
                 out_specs=pl.BlockSpec((tm, cb), lambda jc, i: (i, jc)),
                 out_shape=_sds((T, C), F32),
                 scratch=[pltpu.VMEM((tm + 2 * HALO, cb), F32)])(u, u, u, w_pad, bias)


def _conv_post(c, g, b):
    T, C = c.shape
    tm = _tile(T, 512)

    def body(c_ref, g_ref, b_ref, o_ref):
        xhat, _ = _ln_stats(c_ref[...])
        y = xhat * g_ref[...] + b_ref[...]
        o_ref[...] = (y * _sigmoid(y)).astype(BF16)

    blk = pl.BlockSpec((tm, C), lambda i: (i, 0))
    vec = pl.BlockSpec((1, C), lambda i: (0, 0))
    return _call(body, name="conv_ln_silu", grid=(T // tm,), in_specs=[blk, vec, vec], out_specs=blk,
                 out_shape=_sds((T, C), BF16))(c, g, b)


def _conv_post_bwd(dcat, c, g, b):
    T, C = c.shape
    tm = _tile(T, 512)

    def body(d_ref, c_ref, g_ref, b_ref, dc_o, dg_o, db_o):
        first = pl.program_id(0) == 0
        xhat, rstd = _ln_stats(c_ref[...])
        y = xhat * g_ref[...] + b_ref[...]
        sg = _sigmoid(y)
        dy = d_ref[...] * (sg * (1.0 + y * (1.0 - sg)))
        _acc_out(dg_o, _colsum(dy * xhat), first)
        _acc_out(db_o, _colsum(dy), first)
        dc_o[...] = _ln_bwd(dy, xhat, rstd, g_ref[...])

    blk = pl.BlockSpec((tm, C), lambda i: (i, 0))
    vec = pl.BlockSpec((1, C), lambda i: (0, 0))
    return _call(body, name="conv_ln_silu_bwd", grid=(T // tm,),
                 in_specs=[pl.BlockSpec((tm, C), lambda i: (i, 1)), blk, vec, vec],
                 out_specs=[blk, vec, vec],
                 out_shape=[_sds((T, C), F32), _sds((1, C), F32), _sds((1, C), F32)])(dcat, c, g, b)


def _conv_bwd(dc, u, w_pad):
    T, C = u.shape
    tm, cb = _tile(T, 256), _tile(C, 256)
    n_t = T // tm
    rb = min(CONV_ROWS, tm)

    def body(dp, dcur, dn, up, uc, un, w_ref, du_o, dw_o, db_o, dext, uext, dw_sc):
        i = pl.program_id(1)
        _fill_ext(dext, dp, dcur, dn, i, n_t, tm)
        _fill_ext(uext, up, uc, un, i, n_t, tm)

        @pl.when(i == 0)
        def _():
            dw_sc[...] = jnp.zeros_like(dw_sc)

        for r0 in range(0, tm, rb):
            acc = jnp.zeros((rb, cb), F32)
            d_here = dcur[r0:r0 + rb, :]
            for k in range(CONV_W):
                acc = acc + w_ref[k:k + 1, :] * dext[r0 + 2 * HALO - 1 - k:r0 + 2 * HALO - 1 - k + rb, :]
                prod = d_here * uext[r0 + k + 1:r0 + k + 1 + rb, :]
                dw_sc[k] += jnp.sum(prod.reshape(rb // 8, 8, cb), axis=0)
            du_o[r0:r0 + rb, :] = acc
        dw_sc[CONV_W] += jnp.sum(dcur[...].reshape(tm // 8, 8, cb), axis=0)

        @pl.when(i == n_t - 1)
        def _():
            red = jnp.sum(dw_sc[...], axis=1)
            row = lax.broadcasted_iota(jnp.int32, red.shape, 0)
            dw_o[...] = jnp.where(row < CONV_W, red, 0.0)
            db_o[...] = jnp.sum(jnp.where(row == CONV_W, red, 0.0), axis=0, keepdims=True)

    return _call(body, name="conv_bwd", grid=(C // cb, n_t),
                 in_specs=_halo_specs(tm, cb, n_t) + _halo_specs(tm, cb, n_t)
                 + [pl.BlockSpec((CONV_W_PAD, cb), lambda jc, i: (0, jc))],
                 out_specs=[pl.BlockSpec((tm, cb), lambda jc, i: (i, jc)),
                            pl.BlockSpec((CONV_W_PAD, cb), lambda jc, i: (0, jc)),
                            pl.BlockSpec((1, cb), lambda jc, i: (0, jc))],
                 out_shape=[_sds((T, C), F32), _sds((CONV_W_PAD, C), F32), _sds((1, C), F32)],
                 scratch=[pltpu.VMEM((tm + 2 * HALO, cb), F32), pltpu.VMEM((tm + 2 * HALO, cb), F32),
                          pltpu.VMEM((CONV_W_PAD, 8, cb), F32)])(dc, dc, dc, u, u, u, w_pad)


def _glu_bwd(du, h, C):
    T = du.shape[0]
    tm = _tile(T, 512)

    def body(du_ref, a_ref, gt_ref, o_ref):
        sg = _sigmoid(gt_ref[...])
        du_v = du_ref[...]
        o_ref[:, :C] = (du_v * sg).astype(BF16)
        o_ref[:, C:] = (du_v * a_ref[...] * sg * (1.0 - sg)).astype(BF16)

    return _call(body, name="glu_bwd", grid=(T // tm,),
                 in_specs=[pl.BlockSpec((tm, C), lambda i: (i, 0)), pl.BlockSpec((tm, C), lambda i: (i, 0)),
                           pl.BlockSpec((tm, C), lambda i: (i, 1))],
                 out_specs=pl.BlockSpec((tm, 2 * C), lambda i: (i, 0)),
                 out_shape=_sds((T, 2 * C), BF16))(du, h, h)


def _attn_delta(dcat, attn):
    T = attn.shape[0]
    tm = _tile(T, 512)

    def body(d_ref, o_ref, out_ref):
        out_ref[...] = jnp.sum(d_ref[...] * o_ref[...].astype(F32), axis=1, keepdims=True)

    blk = pl.BlockSpec((tm, D_V), lambda i, h: (i, h))
    return _call(body, name="attn_delta", grid=(T // tm, N_HEADS), in_specs=[blk, blk],
                 out_specs=pl.BlockSpec((None, tm, 1), lambda i, h: (h, i, 0)),
                 out_shape=_sds((N_HEADS, T, 1), F32))(dcat, attn)


def _flash_bwd(q, k, v, dcat, lse, delta):
    _, T, _ = q.shape
    tq, tk = _tile(T, 512), _tile(T, 512)
    nq = T // tq

    def body(q_ref, k_ref, v_ref, do_ref, lse_ref, dl_ref, dq_o, dk_o, dv_o, dk_sc, dv_sc):
        j, i = pl.program_id(1), pl.program_id(2)
        qv, kv = q_ref[...], k_ref[...]
        do = do_ref[...].astype(BF16)
        s = lax.dot_general(qv, kv, NT, preferred_element_type=F32)
        p = jnp.exp(s - lse_ref[...])
        dp = lax.dot_general(do, v_ref[...], NT, preferred_element_type=F32)
        ds = (p * (dp - dl_ref[...])).astype(BF16)
        dv_part = lax.dot_general(p.astype(BF16), do, TN, preferred_element_type=F32)
        dk_part = lax.dot_general(ds, qv, TN, preferred_element_type=F32)
        dq_part = jnp.dot(ds, kv, preferred_element_type=F32)
        rows = pl.ds(pl.multiple_of(i * tq, tq), tq)

        @pl.when(j == 0)
        def _():
            dq_o[rows, :] = dq_part

        @pl.when(j > 0)
        def _():
            dq_o[rows, :] += dq_part

        @pl.when(i == 0)
        def _():
            dk_sc[...] = dk_part
            dv_sc[...] = dv_part

        @pl.when(i > 0)
        def _():
            dk_sc[...] += dk_part
            dv_sc[...] += dv_part

        @pl.when(i == nq - 1)
        def _():
            dk_o[...] = dk_sc[...]
            dv_o[...] = dv_sc[...].astype(BF16)

    return _call(body, name="flash_bwd", grid=(N_HEADS, T // tk, nq),
                 in_specs=[pl.BlockSpec((None, tq, D_HEAD_PAD), lambda h, j, i: (h, i, 0)),
                           pl.BlockSpec((None, tk, D_HEAD_PAD), lambda h, j, i: (h, j, 0)),
                           pl.BlockSpec((None, tk, D_V), lambda h, j, i: (h, j, 0)),
                           pl.BlockSpec((tq, D_V), lambda h, j, i: (i, h)),
                           pl.BlockSpec((None, tq, 1), lambda h, j, i: (h, i, 0)),
                           pl.BlockSpec((None, tq, 1), lambda h, j, i: (h, i, 0))],
                 out_specs=[pl.BlockSpec((None, T, D_HEAD_PAD), lambda h, j, i: (h, 0, 0)),
                            pl.BlockSpec((None, tk, D_HEAD_PAD), lambda h, j, i: (h, j, 0)),
                            pl.BlockSpec((None, tk, D_V), lambda h, j, i: (h, j, 0))],
                 out_shape=[_sds((N_HEADS, T, D_HEAD_PAD), F32), _sds((N_HEADS, T, D_HEAD_PAD), F32),
                            _sds((N_HEADS, T, D_V), BF16)],
                 scratch=[pltpu.VMEM((tk, D_HEAD_PAD), F32), pltpu.VMEM((tk, D_V), F32)],
                 )(q, k, v, dcat, lse, delta)


def _dq_post(dq, tabs):
    _, T, _ = dq.shape
    tm = _tile(T, 512)

    def body(d_ref, cp, sa, sb, o_ref):
        d = d_ref[...] * SCALE
        o_ref[:, :D_NOPE] = d[:, :D_NOPE].astype(BF16)
        o_ref[:, D_NOPE:] = _unrope(d[:, D_NOPE:], cp[...], sa[...], sb[...]).astype(BF16)

    blk = pl.BlockSpec((None, tm, D_HEAD_PAD), lambda i, h: (h, i, 0))
    tab = pl.BlockSpec((tm, LANES), lambda i, h: (i, 0))
    return _call(body, name="dq_unrope", grid=(T // tm, N_HEADS), in_specs=[blk, tab, tab, tab], out_specs=blk,
                 out_shape=_sds(dq.shape, BF16))(dq, *tabs)


def _dk_post(dk, dv, tabs):
    _, T, _ = dk.shape
    tm = _tile(T, 512)

    def body(dk_ref, dv_ref, cp, sa, sb, dkv_o, dkr_o, sc):
        h = pl.program_id(1)
        d = dk_ref[...]
        dkv_o[:, :D_NOPE] = d[:, :D_NOPE].astype(BF16)
        dkv_o[:, D_NOPE:] = dv_ref[...]

        @pl.when(h == 0)
        def _():
            sc[...] = d[:, D_NOPE:]

        @pl.when(h > 0)
        def _():
            sc[...] += d[:, D_NOPE:]

        @pl.when(h == N_HEADS - 1)
        def _():
            dkr_o[...] = _unrope(sc[...], cp[...], sa[...], sb[...]).astype(BF16)

    tab = pl.BlockSpec((tm, LANES), lambda i, h: (i, 0))
    return _call(body, name="dk_unrope", grid=(T // tm, N_HEADS),
                 in_specs=[pl.BlockSpec((None, tm, D_HEAD_PAD), lambda i, h: (h, i, 0)),
                           pl.BlockSpec((None, tm, D_V), lambda i, h: (h, i, 0)), tab, tab, tab],
                 out_specs=[pl.BlockSpec((None, tm, D_HEAD_PAD), lambda i, h: (h, i, 0)), tab],
                 out_shape=[_sds((N_HEADS, T, D_HEAD_PAD), BF16), _sds((T, LANES), BF16)],
                 scratch=[pltpu.VMEM((tm, LANES), F32)])(dk, dv, *tabs)


def _latent_bwd(name, dproj, w_heads, h, col_blk, g):
    _, T, _ = dproj.shape
    tm = _tile(T, 512)

    def ep(acc, ex, out):
        dx, dg = _rms_bwd(acc, ex[0][...], ex[1][...])
        out[0][...] = dx.astype(BF16)
        _acc_out(out[1], dg, pl.program_id(0) == 0)

    return _matmul(name, dproj, w_heads, dims=NT, grid=(T // tm, 1, N_HEADS),
                   a_spec=pl.BlockSpec((None, tm, D_HEAD_PAD), lambda i, j, k: (k, i, 0)),
                   b_spec=pl.BlockSpec((None, LORA, D_HEAD_PAD), lambda i, j, k: (k, 0, 0)),
                   acc_shape=(tm, LORA),
                   extras=[(h, pl.BlockSpec((tm, LORA), lambda i, j, k: (i, col_blk))),
                           (g, pl.BlockSpec((1, LORA), lambda i, j, k: (0, 0)))],
                   outs=[(_sds((T, LORA), BF16), pl.BlockSpec((tm, LORA), lambda i, j, k: (i, 0))),
                         (_sds((1, LORA), F32), pl.BlockSpec((1, LORA), lambda i, j, k: (0, 0)))],
                   epilogue=ep)


def _head_weight_grad(name, latent, dproj):
    _, T, _ = dproj.shape
    tk = _tile(T, 512)
    return _matmul(name, latent, dproj, dims=TN, grid=(N_HEADS, 1, T // tk),
                   a_spec=pl.BlockSpec((tk, LORA), lambda i, j, k: (k, 0)),
                   b_spec=pl.BlockSpec((None, tk, D_HEAD_PAD), lambda i, j, k: (i, k, 0)),
                   acc_shape=(LORA, D_HEAD_PAD),
                   outs=[(_sds((N_HEADS, LORA, D_HEAD_PAD), F32),
                          pl.BlockSpec((None, LORA, D_HEAD_PAD), lambda i, j, k: (i, 0, 0)))],
                   epilogue=_store())[0]


def _weight_grad(name, a, b, tm_pref=1024, tn_pref=1024, stacked_cols=None):
    T, M = a.shape
    N = b.shape[1]
    tk = _tile(T, 512)
    tm = _tile(M, tm_pref)
    if stacked_cols is None:
        tn = _tile(N, tn_pref)
        out = (_sds((M, N), F32), pl.BlockSpec((tm, tn), lambda i, j, k: (i, j)))
    else:
        tn = _tile(stacked_cols, tn_pref)
        per = stacked_cols // tn
        out = (_sds((N // stacked_cols, M, stacked_cols), F32),
               pl.BlockSpec((None, tm, tn), lambda i, j, k: (j // per, i, j % per)))
    return _matmul(name, a, b, dims=TN, grid=(M // tm, N // tn, T // tk),
                   a_spec=pl.BlockSpec((tk, tm), lambda i, j, k: (k, i)),
                   b_spec=pl.BlockSpec((tk, tn), lambda i, j, k: (k, j)),
                   acc_shape=(tm, tn), outs=[out], epilogue=_store())[0]


def _small_names():
    return ["ln_in_g", "ln_in_b", "g_cq", "g_ckv", "conv_b", "g_conv_ln", "b_conv_ln", "g_ln1", "b_ln1", "g_ln2", "b_ln2"]


def _pack(vecs):
    flat = jnp.concatenate([v.reshape(-1) for v in vecs])
    assert flat.shape[0] % (8 * LANES) == 0
    return flat.reshape(-1, LANES)


def _unpack(packed, like):
    flat, out, off = packed.reshape(-1), [], 0
    for v in like:
        out.append(flat[off:off + v.size].reshape(v.shape))
        off += v.size
    return out


def kernel(x, positions, ln_in_g, ln_in_b, w_in, g_cq, w_uq, g_ckv, w_uk, w_uv, conv_w, conv_b, g_conv_ln, b_conv_ln, w_out, g_ln1, b_ln1, w_ff1, w_ff2, g_ln2, b_ln2, loss_target, m_ln_in_g, m_ln_in_b, m_w_in, m_g_cq, m_w_uq, m_g_ckv, m_w_uk, m_w_uv, m_conv_w, m_conv_b, m_g_conv_ln, m_b_conv_ln, m_w_out, m_g_ln1, m_b_ln1, m_w_ff1, m_w_ff2, m_g_ln2, m_b_ln2, v_ln_in_g, v_ln_in_b, v_w_in, v_g_cq, v_w_uq, v_g_ckv, v_w_uk, v_w_uv, v_conv_w, v_conv_b, v_g_conv_ln, v_b_conv_ln, v_w_out, v_g_ln1, v_b_ln1, v_w_ff1, v_w_ff2, v_g_ln2, v_b_ln2):
    args = dict(locals())
    T, D = x.shape[1], x.shape[2]
    C = D - N_HEADS * D_V
    Fs = w_ff1.shape[2]
    F = N_DEV * Fs
    n_in = N_DEV * w_in.shape[2]
    n_in_p = 2 * C + 2 * LORA + LANES
    assert n_in == 2 * LORA + D_ROPE + 2 * C and w_uq.shape[2] == D_QK and conv_w.shape[2] * N_DEV == C

    xs, tgt = x[0], loss_target[0]
    row = lambda v_: v_.reshape(1, -1)

    g_w_in, g_w_uq, g_w_uk, g_w_uv, g_conv_w, g_w_out, g_w_ff1, g_w_ff2 = _all_gather(
        [w_in[0].astype(BF16), w_uq[0].astype(BF16), w_uk[0].astype(BF16), w_uv[0].astype(BF16), conv_w[0],
         w_out[0].astype(BF16), w_ff1[0].astype(BF16), w_ff2[0].astype(BF16)])
    w_in_f = jnp.transpose(g_w_in, (1, 0, 2)).reshape(D, n_in)
    s_cq, s_ckv, s_kr, s_a, s_g = 0, LORA, 2 * LORA, 2 * LORA + D_ROPE, 2 * LORA + D_ROPE + C
    w_in_p = jnp.concatenate([w_in_f[:, s_a:s_g], w_in_f[:, s_g:], w_in_f[:, s_cq:s_ckv], w_in_f[:, s_ckv:s_kr],
                              w_in_f[:, s_kr:s_a], jnp.zeros((D, LANES - D_ROPE), BF16)], axis=1)
    w_uq_p = jnp.pad(g_w_uq, ((0, 0), (0, 0), (0, D_HEAD_PAD - D_QK)))
    w_ukv = jnp.concatenate([g_w_uk, g_w_uv], axis=2)
    conv_w_f = jnp.pad(jnp.transpose(g_conv_w, (1, 0, 2)).reshape(CONV_W, C), ((0, CONV_W_PAD - CONV_W), (0, 0)))
    w_out_f = g_w_out.reshape(D, D)
    w_ff2_f = g_w_ff2.reshape(F, D)

    half = D_ROPE // 2
    inv_freq = ROPE_BASE ** (-jnp.arange(half, dtype=F32) * (2.0 / D_ROPE))
    inv_freq = jnp.tile(inv_freq, LANES // half).reshape(1, LANES)
    tabs = _rope_tables(positions.reshape(T, 1), inv_freq)

    x0, x0b = _ln_in(xs, row(ln_in_g), row(ln_in_b))

    tm, tn = _tile(T, 512), _tile(n_in_p, 640)
    h = _matmul("h_proj", x0b, w_in_p, dims=NN, grid=(T // tm, n_in_p // tn, 1),
                a_spec=pl.BlockSpec((tm, D), lambda i, j, k: (i, 0)),
                b_spec=pl.BlockSpec((D, tn), lambda i, j, k: (0, j)), acc_shape=(tm, tn),
                outs=[(_sds((T, n_in_p), F32), pl.BlockSpec((tm, tn), lambda i, j, k: (i, j)))],
                epilogue=_store())[0]

    u, cqn, ckvn, kr = _mid(h, g_cq, g_ckv, tabs, C)
    q = _q_proj(cqn, w_uq_p, tabs)
    kf, vf = _kv_proj(ckvn, w_ukv, kr)
    attn, lse = _flash_fwd(q, kf, vf)
    conv_c = _conv_fwd(u, conv_w_f, conv_b)
    conv_out = _conv_post(conv_c, g_conv_ln, b_conv_ln)
    cat = jnp.concatenate([attn, conv_out], axis=1)

    def ep_ln1(acc, ex, out):
        z1 = ALPHA * ex[0][...] + acc
        xhat, _ = _ln_stats(z1)
        x1 = xhat * ex[1][...] + ex[2][...]
        out[0][...] = z1
        out[1][...] = x1
        out[2][...] = x1.astype(BF16)

    tm = _tile(T, 256)
    rowblk = pl.BlockSpec((tm, D), lambda i, j, k: (i, 0))
    vecD = pl.BlockSpec((1, D), lambda i, j, k: (0, 0))
    z1, x1, x1b = _matmul("mix_ln1", cat, w_out_f, dims=NN, grid=(T // tm, 1, 1), a_spec=rowblk,
                          b_spec=pl.BlockSpec((D, D), lambda i, j, k: (0, 0)), acc_shape=(tm, D),
                          extras=[(x0, rowblk), (g_ln1, vecD), (b_ln1, vecD)],
                          outs=[(_sds((T, D), F32), rowblk), (_sds((T, D), F32), rowblk), (_sds((T, D), BF16), rowblk)],
                          epilogue=ep_ln1)

    def ep_ff1(acc, ex, out):
        r = jnp.maximum(acc, 0.0)
        out[0][...] = (r * r).astype(BF16)
        out[1][...] = r.astype(BF16)

    tm, tn = _tile(T, 512), _tile(Fs, 1024)
    per = Fs // tn
    fblk = pl.BlockSpec((tm, tn), lambda i, j, k: (i, j))
    f_act, r_act = _matmul("ff1_relu2", x1b, g_w_ff1, dims=NN, grid=(T // tm, F // tn, 1),
                           a_spec=pl.BlockSpec((tm, D), lambda i, j, k: (i, 0)),
                           b_spec=pl.BlockSpec((None, D, tn), lambda i, j, k: (j // per, 0, j % per)),
                           acc_shape=(tm, tn), outs=[(_sds((T, F), BF16), fblk), (_sds((T, F), BF16), fblk)],
                           epilogue=ep_ff1)

    def ep_ln2(acc, ex, out):
        first = pl.program_id(0) == 0
        g2 = ex[2][...]
        z2 = ALPHA * ex[0][...] + acc
        xhat, rstd = _ln_stats(z2)
        err = xhat * g2 + ex[3][...] - ex[1][...]
        part = 0.5 * jnp.sum(jnp.mean(err * err, axis=-1, keepdims=True))
        _acc_out(out[2], jnp.zeros((8, LANES), F32) + part, first)
        dy = err * (1.0 / D)
        _acc_out(out[3], _colsum(dy * xhat), first)
        _acc_out(out[4], _colsum(dy), first)
        dz2 = _ln_bwd(dy, xhat, rstd, g2)
        out[0][...] = dz2
        out[1][...] = dz2.astype(BF16)

    tm, tk = _tile(T, 256), _tile(F, 512)
    rowblk = pl.BlockSpec((tm, D), lambda i, j, k: (i, 0))
    dz2, dz2b, loss_blk, dg_ln2, db_ln2 = _matmul(
        "ff2_ln2_loss", f_act, w_ff2_f, dims=NN, grid=(T // tm, 1, F // tk),
        a_spec=pl.BlockSpec((tm, tk), lambda i, j, k: (i, k)), b_spec=pl.BlockSpec((tk, D), lambda i, j, k: (k, 0)),
        acc_shape=(tm, D), extras=[(x1, rowblk), (tgt, rowblk), (g_ln2, vecD), (b_ln2, vecD)],
        outs=[(_sds((T, D), F32), rowblk), (_sds((T, D), BF16), rowblk),
              (_sds((8, LANES), F32), pl.BlockSpec((8, LANES), lambda i, j, k: (0, 0))),
              (_sds((1, D), F32), vecD), (_sds((1, D), F32), vecD)],
        epilogue=ep_ln2)
    loss = lax.psum(loss_blk[0, 0], ("x", "y", "c"))

    def ep_dpre(acc, ex, out):
        out[0][...] = (acc * (2.0 * ex[0][...].astype(F32))).astype(BF16)

    tm, tn = _tile(T, 512), _tile(F, 1024)
    fblk = pl.BlockSpec((tm, tn), lambda i, j, k: (i, j))
    dpre = _matmul("ff2_dgrad", dz2b, w_ff2_f, dims=NT, grid=(T // tm, F // tn, 1),
                   a_spec=pl.BlockSpec((tm, D), lambda i, j, k: (i, 0)), b_spec=pl.BlockSpec((tn, D), lambda i, j, k: (j, 0)),
                   acc_shape=(tm, tn), extras=[(r_act, fblk)], outs=[(_sds((T, F), BF16), fblk)], epilogue=ep_dpre)[0]

    dw_ff2 = _weight_grad("ff2_wgrad", f_act, dz2b).reshape(N_DEV, Fs, D)
    dw_ff1 = _weight_grad("ff1_wgrad", x1b, dpre, stacked_cols=Fs)

    def ep_ln1_bwd(acc, ex, out):
        first = pl.program_id(0) == 0
        dx1 = ALPHA * ex[0][...] + acc
        xhat, rstd = _ln_stats(ex[1][...])
        _acc_out(out[2], _colsum(dx1 * xhat), first)
        _acc_out(out[3], _colsum(dx1), first)
        dz1 = _ln_bwd(dx1, xhat, rstd, ex[2][...])
        out[0][...] = dz1
        out[1][...] = dz1.astype(BF16)

    tm, tk = _tile(T, 256), _tile(Fs, 512)
    per = Fs // tk
    rowblk = pl.BlockSpec((tm, D), lambda i, j, k: (i, 0))
    dz1, dz1b, dg_ln1, db_ln1 = _matmul(
        "ff1_dgrad_ln1_bwd", dpre, g_w_ff1, dims=NT, grid=(T // tm, 1, F // tk),
        a_spec=pl.BlockSpec((tm, tk), lambda i, j, k: (i, k)),
        b_spec=pl.BlockSpec((None, D, tk), lambda i, j, k: (k // per, 0, k % per)),
        acc_shape=(tm, D), extras=[(dz2, rowblk), (z1, rowblk), (g_ln1, vecD)],
        outs=[(_sds((T, D), F32), rowblk), (_sds((T, D), BF16), rowblk), (_sds((1, D), F32), vecD), (_sds((1, D), F32), vecD)],
        epilogue=ep_ln1_bwd)

    dw_out = _weight_grad("out_wgrad", cat, dz1b).reshape(N_DEV, D // N_DEV, D)
    tm, tn = _tile(T, 512), _tile(D, 1024)
    dcat = _matmul("out_dgrad", dz1b, w_out_f, dims=NT, grid=(T // tm, D // tn, 1),
                   a_spec=pl.BlockSpec((tm, D), lambda i, j, k: (i, 0)), b_spec=pl.BlockSpec((tn, D), lambda i, j, k: (j, 0)),
                   acc_shape=(tm, tn), outs=[(_sds((T, D), F32), pl.BlockSpec((tm, tn), lambda i, j, k: (i, j)))],
                   epilogue=_store())[0]

    dc, dg_conv_ln, db_conv_ln = _conv_post_bwd(dcat, conv_c, g_conv_ln, b_conv_ln)
    du, dconv_w_p, dconv_b = _conv_bwd(dc, u, conv_w_f)
    dconv_in = _glu_bwd(du, h, C)

    delta = _attn_delta(dcat, attn)
    dq, dk, dv = _flash_bwd(q, kf, vf, dcat, lse, delta)
    dq_raw = _dq_post(dq, tabs)
    dkv, dkr = _dk_post(dk, dv, tabs)
    cq_blk = (2 * C) // LORA
    dcq, dg_cq = _latent_bwd("q_dgrad_rms_bwd", dq_raw, w_uq_p, h, cq_blk, g_cq)
    dckv, dg_ckv = _latent_bwd("kv_dgrad_rms_bwd", dkv, w_ukv, h, cq_blk + 1, g_ckv)
    dw_uq = _head_weight_grad("uq_wgrad", cqn, dq_raw)[:, :, :D_QK]
    dw_ukv = _head_weight_grad("ukv_wgrad", ckvn, dkv)
    dw_uk, dw_uv = dw_ukv[:, :, :D_NOPE], dw_ukv[:, :, D_NOPE:]

    dh = jnp.concatenate([dconv_in, dcq, dckv, dkr], axis=1)
    dw_in_p = _weight_grad("in_wgrad", x0b, dh, tn_pref=640)
    dw_in_f = jnp.concatenate([dw_in_p[:, 2 * C:2 * C + 2 * LORA + D_ROPE], dw_in_p[:, :2 * C]], axis=1)
    dw_in = jnp.transpose(dw_in_f.reshape(D, N_DEV, n_in // N_DEV), (1, 0, 2))

    def ep_ln_in_bwd(acc, ex, out):
        first = pl.program_id(0) == 0
        dx0 = ALPHA * ex[0][...] + acc
        xhat, rstd = _ln_stats(ex[1][...])
        _acc_out(out[1], _colsum(dx0 * xhat), first)
        _acc_out(out[2], _colsum(dx0), first)
        out[0][...] = _ln_bwd(dx0, xhat, rstd, ex[2][...])

    tm, tk = _tile(T, 256), _tile(n_in_p, 640)
    rowblk = pl.BlockSpec((tm, D), lambda i, j, k: (i, 0))
    grad_x, dg_ln_in, db_ln_in = _matmul(
        "in_dgrad_ln_in_bwd", dh, w_in_p, dims=NT, grid=(T // tm, 1, n_in_p // tk),
        a_spec=pl.BlockSpec((tm, tk), lambda i, j, k: (i, k)), b_spec=pl.BlockSpec((D, tk), lambda i, j, k: (0, k)),
        acc_shape=(tm, D), extras=[(dz1, rowblk), (xs, rowblk), (row(ln_in_g), vecD)],
        outs=[(_sds((T, D), F32), rowblk), (_sds((1, D), F32), vecD), (_sds((1, D), F32), vecD)],
        epilogue=ep_ln_in_bwd)

    dconv_w = jnp.transpose(dconv_w_p[:CONV_W].reshape(CONV_W, N_DEV, C // N_DEV), (1, 0, 2))
    small = dict(ln_in_g=dg_ln_in, ln_in_b=db_ln_in, g_cq=dg_cq, g_ckv=dg_ckv, conv_b=dconv_b, g_conv_ln=dg_conv_ln,
                 b_conv_ln=db_conv_ln, g_ln1=dg_ln1, b_ln1=db_ln1, g_ln2=dg_ln2, b_ln2=db_ln2)
    names = _small_names()
    big = dict(w_in=dw_in, w_uq=dw_uq, w_uk=dw_uk, w_uv=dw_uv, conv_w=dconv_w, w_out=dw_out, w_ff1=dw_ff1, w_ff2=dw_ff2)
    big_names = list(big)
    parts = _reduce_scatter_exchange([big[n] for n in big_names], [_pack([small[n] for n in names])])

    res = {}
    for n, p in zip(big_names, parts[:len(big_names)]):
        res[n] = [o.reshape(args[n].shape) for o in _adamw("adamw_" + n, p, args[n][0], args["m_" + n][0], args["v_" + n][0])]
    packed = _adamw("adamw_small", parts[-1], _pack([args[n] for n in names]), _pack([args["m_" + n] for n in names]),
                    _pack([args["v_" + n] for n in names]))
    like = [args[n] for n in names]
    unpacked = [_unpack(p, like) for p in packed]
    for i, n in enumerate(names):
        res[n] = [unpacked[kind][i] for kind in range(4)]

    order = ["ln_in_g", "ln_in_b", "w_in", "g_cq", "w_uq", "g_ckv", "w_uk", "w_uv", "conv_w", "conv_b", "g_conv_ln",
             "b_conv_ln", "w_out", "g_ln1", "b_ln1", "w_ff1", "w_ff2", "g_ln2", "b_ln2"]
    outs = [loss, grad_x.reshape(x.shape)]
    for kind in range(4):
        outs += [res[n][kind] for n in order]
    return tuple(outs)
```

```python
import jax
import jax.numpy as jnp
from jax import lax
from jax.experimental import pallas as pl
from jax.experimental.pallas import tpu as pltpu

F32 = jnp.float32
BF16 = jnp.bfloat16

N_HEADS = 8
D_NOPE = 128
D_ROPE = 64
D_V = 128
D_QK = D_NOPE + D_ROPE
D_HEAD_PAD = 256
LORA = 512
CONV_W = 31
CONV_HALF = CONV_W // 2
CONV_W_PAD = 32
HALO = 16
LN_EPS = 1e-5
RMS_EPS = 1e-6
ALPHA = 2.0 ** 0.25
SCALE = float(D_QK) ** -0.5
LOG2_E = 1.4426950408889634
LN_2 = 0.6931471805599453
Q_SCALE = SCALE * LOG2_E
ROPE_BASE = 10000.0
ADAM_LR, ADAM_B1, ADAM_B2, ADAM_EPS, ADAM_WD, ADAM_STEP = 0.001, 0.9, 0.999, 1e-08, 0.01, 10

N_DEV = 8
LANES = 128
VMEM_LIMIT_V7X = 56 * 1024 * 1024

NN = (((1,), (0,)), ((), ()))
NT = (((1,), (1,)), ((), ()))
TN = (((0,), (0,)), ((), ()))


def _call(body, *, name, grid, in_specs, out_specs, out_shape, scratch=()):
    params = pltpu.CompilerParams(dimension_semantics=("arbitrary",) * len(grid),
                                  vmem_limit_bytes=VMEM_LIMIT_V7X)
    return pl.pallas_call(body, name=name, grid=grid, in_specs=in_specs, out_specs=out_specs,
                          out_shape=out_shape, scratch_shapes=scratch, compiler_params=params)


def _tile(n, pref):
    if n <= pref:
        return n
    t = (pref // LANES) * LANES
    while t > LANES and n % t:
        t -= LANES
    assert n % t == 0, (n, pref)
    return t


def _sds(shape, dtype):
    return jax.ShapeDtypeStruct(shape, dtype)


def _ln_stats(z):
    mu = jnp.mean(z, axis=-1, keepdims=True)
    zc = z - mu
    var = jnp.mean(zc * zc, axis=-1, keepdims=True)
    rstd = lax.rsqrt(var + LN_EPS)
    return zc * rstd, rstd


def _ln_bwd(dy, xhat, rstd, g):
    gd = dy * g
    m1 = jnp.mean(gd, axis=-1, keepdims=True)
    m2 = jnp.mean(gd * xhat, axis=-1, keepdims=True)
    return rstd * (gd - m1 - xhat * m2)


def _rms(x, g):
    return x * lax.rsqrt(jnp.mean(x * x, axis=-1, keepdims=True) + RMS_EPS) * g


def _rms_bwd(dy, x, g):
    r = lax.rsqrt(jnp.mean(x * x, axis=-1, keepdims=True) + RMS_EPS)
    dxn = dy * g
    dx = r * dxn - x * (r * r * r) * jnp.mean(dxn * x, axis=-1, keepdims=True)
    dg = jnp.sum(dy * x * r, axis=0, keepdims=True)
    return dx, dg


def _sigmoid(x):
    return 1.0 / (1.0 + jnp.exp(-x))


def _rope(x, cos_p, sin_a, sin_b):
    return x * cos_p + pltpu.roll(x, 96, 1) * sin_a + pltpu.roll(x, 32, 1) * sin_b


def _unrope(d, cos_p, sin_a, sin_b):
    return d * cos_p - pltpu.roll(d, 96, 1) * sin_a - pltpu.roll(d, 32, 1) * sin_b


def _colsum(v):
    return jnp.sum(v, axis=0, keepdims=True)


def _acc_out(ref, val, first):
    if first is False:
        ref[...] += val
        return

    @pl.when(first)
    def _():
        ref[...] = val

    @pl.when(jnp.logical_not(first))
    def _():
        ref[...] += val


class _Rows:
    def __init__(self, ref, sl):
        self.ref, self.sl = ref, sl

    def __getitem__(self, idx):
        assert idx is Ellipsis
        return self.ref[self.sl, :]

    def __setitem__(self, idx, val):
        assert idx is Ellipsis
        self.ref[self.sl, :] = val


def _matmul(name, a, b, *, dims, grid, a_spec, b_spec, acc_shape, outs, epilogue, extras=(), ep_rows=None):
    nk = grid[2]
    ne, no = len(extras), len(outs)
    tm = acc_shape[0]

    def finish(acc_rows, ex, out):
        first = pl.program_id(0) == 0
        if ep_rows is None or ep_rows >= tm:
            epilogue(acc_rows(slice(None)), ex, out, first)
            return
        for r0 in range(0, tm, ep_rows):
            sl = slice(r0, r0 + ep_rows)
            view = lambda r: _Rows(r, sl) if r.shape[0] == tm else r
            epilogue(acc_rows(sl), [view(r) for r in ex], [view(r) for r in out], first if r0 == 0 else False)

    def body(*refs):
        a_ref, b_ref = refs[0], refs[1]
        ex = refs[2:2 + ne]
        out = refs[2 + ne:2 + ne + no]
        part = lax.dot_general(a_ref[...], b_ref[...], dims, preferred_element_type=F32)
        if nk == 1:
            finish(lambda sl: part[sl, :], ex, out)
        else:
            acc = refs[2 + ne + no]
            k = pl.program_id(2)

            @pl.when(k == 0)
            def _():
                acc[...] = part

            @pl.when(k > 0)
            def _():
                acc[...] += part

            @pl.when(k == nk - 1)
            def _():
                finish(lambda sl: acc[sl, :], ex, out)

    scratch = [] if nk == 1 else [pltpu.VMEM(acc_shape, F32)]
    res = _call(body, name=name, grid=grid,
                in_specs=[a_spec, b_spec] + [s for _, s in extras],
                out_specs=[s for _, s in outs],
                out_shape=[o for o, _ in outs],
                scratch=scratch)(a, b, *[e for e, _ in extras])
    return res


def _store(dtype=F32):
    def ep(acc, ex, out, first):
        out[0][...] = acc.astype(dtype)
    return ep


def _mesh_pos():
    return lax.axis_index("x"), lax.axis_index("y"), lax.axis_index("c")


def _flip(v, bit):
    return 1 - v if bit else v


def _all_gather(shards):
    n = len(shards)

    def body(*refs):
        ins, outs = refs[:n], refs[n:2 * n]
        send_sems, recv_sems, local_sems = refs[2 * n:]
        x, y, c = _mesh_pos()
        me, sibling = (x, y, c), (x, y, 1 - c)
        chips = [(1 - x, y), (x, 1 - y), (1 - x, 1 - y)]

        def slot(p):
            return 4 * p[0] + 2 * p[1] + p[2]

        def copy(w, k, block, to, src=None):
            dst = outs[w].at[slot(block)]
            return pltpu.make_async_remote_copy(
                src_ref=dst if src is None else src, dst_ref=dst,
                send_sem=send_sems.at[w, k], recv_sem=recv_sems.at[w, k],
                device_id=to, device_id_type=pl.DeviceIdType.MESH)

        mine = [pltpu.make_async_copy(ins[w], outs[w].at[slot(me)], local_sems.at[w]) for w in range(n)]
        for cp in mine:
            cp.start()
        first = []
        for w in range(n):
            first.append(copy(w, 0, me, sibling, src=ins[w]))
            for j, chip in enumerate(chips):
                first.append(copy(w, 1 + j, me, (*chip, c), src=ins[w]))
        for cp in first:
            cp.start()
        passed = []
        for w in range(n):
            for j, chip in enumerate(chips):
                copy(w, 1 + j, (*chip, c), me).wait_recv()
                fwd = copy(w, 4 + j, (*chip, c), sibling)
                fwd.start()
                passed.append(fwd)
        for w in range(n):
            copy(w, 0, sibling, me).wait_recv()
            for j, chip in enumerate(chips):
                copy(w, 4 + j, (*chip, 1 - c), me).wait_recv()
        for cp in first + passed:
            cp.wait_send()
        for cp in mine:
            cp.wait()

    any_spec = pl.BlockSpec(memory_space=pl.ANY)
    return pl.pallas_call(
        body, name="weights_all_gather",
        in_specs=[any_spec] * n, out_specs=[any_spec] * n,
        out_shape=[_sds((N_DEV,) + s.shape, s.dtype) for s in shards],
        scratch_shapes=[pltpu.SemaphoreType.DMA((n, 7)), pltpu.SemaphoreType.DMA((n, 7)),
                        pltpu.SemaphoreType.DMA((n,))],
    )(*shards)


def _reduce_scatter_exchange(stacked, gathered):
    ns, ng = len(stacked), len(gathered)
    n = ns + ng

    def body(*refs):
        ins, outs = refs[:n], refs[n:2 * n]
        send_sems, recv_sems, local_sems = refs[2 * n:]
        x, y, c = _mesh_pos()
        me = 4 * x + 2 * y + c

        def src_of(w, dev_slot):
            return ins[w].at[dev_slot] if w < ns else ins[w]

        local = [pltpu.make_async_copy(src_of(w, me), outs[w].at[me], local_sems.at[w]) for w in range(n)]
        for cp in local:
            cp.start()
        sends, recvs = [], []
        for w in range(n):
            for k in range(1, N_DEV):
                peer = (_flip(x, k & 4), _flip(y, k & 2), _flip(c, k & 1))
                peer_slot = 4 * peer[0] + 2 * peer[1] + peer[2]
                sends.append(pltpu.make_async_remote_copy(
                    src_ref=src_of(w, peer_slot), dst_ref=outs[w].at[me],
                    send_sem=send_sems.at[w, k - 1], recv_sem=recv_sems.at[w, k - 1],
                    device_id=peer, device_id_type=pl.DeviceIdType.MESH))
                recvs.append(pltpu.make_async_remote_copy(
                    src_ref=src_of(w, me), dst_ref=outs[w].at[peer_slot],
                    send_sem=send_sems.at[w, k - 1], recv_sem=recv_sems.at[w, k - 1],
                    device_id=peer, device_id_type=pl.DeviceIdType.MESH))
        for cp in sends:
            cp.start()
        for cp in recvs:
            cp.wait_recv()
        for cp in sends:
            cp.wait_send()
        for cp in local:
            cp.wait()

    any_spec = pl.BlockSpec(memory_space=pl.ANY)
    arrays = list(stacked) + list(gathered)
    out_shape = [_sds(a.shape, a.dtype) for a in stacked] + [_sds((N_DEV,) + a.shape, a.dtype) for a in gathered]
    return pl.pallas_call(
        body, name="grads_reduce_scatter",
        in_specs=[any_spec] * n, out_specs=[any_spec] * n, out_shape=out_shape,
        scratch_shapes=[pltpu.SemaphoreType.DMA((n, 7)), pltpu.SemaphoreType.DMA((n, 7)),
                        pltpu.SemaphoreType.DMA((n,))],
    )(*arrays)


def _adamw(name, parts, w, m, v):
    rows, cols = w.shape
    cap = max(8, (LANES * 1024) // cols)
    tr = rows
    if rows > cap:
        tr = (cap // 8) * 8
        while rows % tr:
            tr -= 8
    c1 = 1.0 / (1.0 - ADAM_B1 ** ADAM_STEP)
    c2 = 1.0 / (1.0 - ADAM_B2 ** ADAM_STEP)

    def body(p_ref, w_ref, m_ref, v_ref, g_o, d_o, m_o, v_o):
        g = p_ref[0]
        for s in range(1, N_DEV):
            g = g + p_ref[s]
        mn = ADAM_B1 * m_ref[...] + (1.0 - ADAM_B1) * g
        vn = ADAM_B2 * v_ref[...] + (1.0 - ADAM_B2) * (g * g)
        g_o[...] = g
        m_o[...] = mn
        v_o[...] = vn
        d_o[...] = -ADAM_LR * ((mn * c1) / (jnp.sqrt(vn * c2) + ADAM_EPS) + ADAM_WD * w_ref[...])

    blk = pl.BlockSpec((tr, cols), lambda i: (i, 0))
    return _call(body, name=name, grid=(rows // tr,),
                 in_specs=[pl.BlockSpec((N_DEV, tr, cols), lambda i: (0, i, 0)), blk, blk, blk],
                 out_specs=[blk] * 4, out_shape=[_sds((rows, cols), F32)] * 4)(parts, w, m, v)


def _rope_tables(pos_col, inv_freq):
    T = pos_col.shape[0]
    tm = _tile(T, 1024)

    def body(p_ref, f_ref, c_o, sa_o, sb_o):
        ang = p_ref[...].astype(F32) * f_ref[...]
        lane = lax.broadcasted_iota(jnp.int32, ang.shape, 1)
        cs, sn = jnp.cos(ang), jnp.sin(ang)
        c_o[...] = jnp.where(lane < D_ROPE, cs, 0.0)
        sa_o[...] = jnp.where(lane < D_ROPE // 2, -sn, 0.0)
        sb_o[...] = jnp.where((lane >= D_ROPE // 2) & (lane < D_ROPE), sn, 0.0)

    blk = pl.BlockSpec((tm, LANES), lambda i: (i, 0))
    return _call(body, name="rope_tables", grid=(T // tm,),
                 in_specs=[pl.BlockSpec((tm, 1), lambda i: (i, 0)), pl.BlockSpec((1, LANES), lambda i: (0, 0))],
                 out_specs=[blk] * 3, out_shape=[_sds((T, LANES), F32)] * 3)(pos_col, inv_freq)


def _ln_in(x, g, b):
    T, D = x.shape
    tm = _tile(T, 512)

    def body(x_ref, g_ref, b_ref, o32, o16):
        xhat, _ = _ln_stats(x_ref[...])
        y = xhat * g_ref[...] + b_ref[...]
        o32[...] = y
        o16[...] = y.astype(BF16)

    blk = pl.BlockSpec((tm, D), lambda i: (i, 0))
    vec = pl.BlockSpec((1, D), lambda i: (0, 0))
    return _call(body, name="ln_in", grid=(T // tm,), in_specs=[blk, vec, vec], out_specs=[blk, blk],
                 out_shape=[_sds((T, D), F32), _sds((T, D), BF16)])(x, g, b)


def _mid(h, g_cq, g_ckv, tabs, C):
    T = h.shape[0]
    tm = _tile(T, 256)
    cq_blk, kr_blk = (2 * C) // LORA, (2 * C + 2 * LORA) // LANES

    def body(a_ref, gt_ref, cq_ref, ckv_ref, kr_ref, gq_ref, gkv_ref, cp, sa, sb, u_o, cqn_o, ckvn_o, kr_o):
        u_o[...] = a_ref[...] * _sigmoid(gt_ref[...])
        cqn_o[...] = _rms(cq_ref[...], gq_ref[...]).astype(BF16)
        ckvn_o[...] = _rms(ckv_ref[...], gkv_ref[...]).astype(BF16)
        kr_o[...] = _rope(kr_ref[...], cp[...], sa[...], sb[...]).astype(BF16)

    def col(w, j):
        return pl.BlockSpec((tm, w), lambda i: (i, j))

    vec = pl.BlockSpec((1, LORA), lambda i: (0, 0))
    return _call(body, name="mid_norm_glu", grid=(T // tm,),
                 in_specs=[col(C, 0), col(C, 1), col(LORA, cq_blk), col(LORA, cq_blk + 1), col(LANES, kr_blk),
                           vec, vec, col(LANES, 0), col(LANES, 0), col(LANES, 0)],
                 out_specs=[col(C, 0), col(LORA, 0), col(LORA, 0), col(LANES, 0)],
                 out_shape=[_sds((T, C), F32), _sds((T, LORA), BF16), _sds((T, LORA), BF16), _sds((T, LANES), BF16)],
                 )(h, h, h, h, h, g_cq, g_ckv, *tabs)


def _q_proj(cqn, w_uq_p, tabs):
    T = cqn.shape[0]
    tm = _tile(T, 512)

    def body(c_ref, w_ref, cp, sa, sb, o_ref):
        q = jnp.dot(c_ref[...], w_ref[...], preferred_element_type=F32)
        o_ref[:, :D_NOPE] = (q[:, :D_NOPE] * Q_SCALE).astype(BF16)
        o_ref[:, D_NOPE:] = (_rope(q[:, D_NOPE:], cp[...], sa[...], sb[...]) * Q_SCALE).astype(BF16)

    tab = pl.BlockSpec((tm, LANES), lambda i, h: (i, 0))
    return _call(body, name="q_proj_rope", grid=(T // tm, N_HEADS),
                 in_specs=[pl.BlockSpec((tm, LORA), lambda i, h: (i, 0)),
                           pl.BlockSpec((None, LORA, D_HEAD_PAD), lambda i, h: (h, 0, 0)), tab, tab, tab],
                 out_specs=pl.BlockSpec((None, tm, D_HEAD_PAD), lambda i, h: (h, i, 0)),
                 out_shape=_sds((N_HEADS, T, D_HEAD_PAD), BF16))(cqn, w_uq_p, *tabs)


def _kv_proj(ckvn, w_ukv, kr):
    T = ckvn.shape[0]
    tm = _tile(T, 512)

    def body(c_ref, w_ref, kr_ref, k_o, v_o):
        kv = jnp.dot(c_ref[...], w_ref[...], preferred_element_type=F32)
        k_o[:, :D_NOPE] = kv[:, :D_NOPE].astype(BF16)
        k_o[:, D_NOPE:] = kr_ref[...]
        v_o[:, :D_V] = kv[:, D_NOPE:].astype(BF16)
        v_o[:, D_V:] = jnp.ones((tm, D_V), BF16)

    return _call(body, name="kv_proj", grid=(T // tm, N_HEADS),
                 in_specs=[pl.BlockSpec((tm, LORA), lambda i, h: (i, 0)),
                           pl.BlockSpec((None, LORA, D_NOPE + D_V), lambda i, h: (h, 0, 0)),
                           pl.BlockSpec((tm, LANES), lambda i, h: (i, 0))],
                 out_specs=[pl.BlockSpec((None, tm, D_HEAD_PAD), lambda i, h: (h, i, 0)),
                            pl.BlockSpec((None, tm, 2 * D_V), lambda i, h: (h, i, 0))],
                 out_shape=[_sds((N_HEADS, T, D_HEAD_PAD), BF16), _sds((N_HEADS, T, 2 * D_V), BF16)])(ckvn, w_ukv, kr)


def _flash_fwd(q, k, v1):
    _, T, _ = q.shape
    tq, tk = _tile(T, FLASH_TQ), _tile(T, FLASH_TK)
    nkv, reps = T // tk, tk // LANES

    def body(q_ref, k_ref, v_ref, o_ref, lse_ref, m_sc, acc_sc):
        m_sc[...] = jnp.full_like(m_sc, -jnp.inf)
        acc_sc[...] = jnp.zeros_like(acc_sc)
        qv = q_ref[...]

        def rows(j):
            return pl.ds(pl.multiple_of(j * tk, tk), tk)

        def scores(j):
            return lax.dot_general(qv, k_ref[rows(j), :], NT, preferred_element_type=F32)

        def update(s, j):
            m_prev = m_sc[...]
            m_new = jnp.maximum(m_prev, jnp.max(s, axis=1, keepdims=True))
            a = jnp.exp2(m_prev - m_new)
            p = jnp.exp2(s - jnp.tile(m_new, (1, reps)))
            pv = jnp.dot(p.astype(BF16), v_ref[rows(j), :], preferred_element_type=F32)
            acc_sc[...] = jnp.tile(a, (1, 2)) * acc_sc[...] + pv
            m_sc[...] = m_new

        if nkv % 2:
            def step(j, carry):
                update(scores(j), j)
                return carry

            lax.fori_loop(0, nkv, step, 0)
        else:
            def pair(jj, s_even):
                s_odd = scores(2 * jj + 1)
                update(s_even, 2 * jj)
                s_next = scores(2 * jj + 2)
                update(s_odd, 2 * jj + 1)
                return s_next

            s_even = lax.fori_loop(0, nkv // 2 - 1, pair, scores(0))
            s_odd = scores(nkv - 1)
            update(s_even, nkv - 2)
            update(s_odd, nkv - 1)
        acc = acc_sc[...]
        l = acc[:, D_V:]
        o_ref[...] = (acc[:, :D_V] / l).astype(BF16)
        lse_ref[...] = m_sc[...] + jnp.log(l) * LOG2_E

    return _call(body, name="flash_fwd", grid=(N_HEADS, T // tq),
                 in_specs=[pl.BlockSpec((None, tq, D_HEAD_PAD), lambda h, i: (h, i, 0)),
                           pl.BlockSpec((None, T, D_HEAD_PAD), lambda h, i: (h, 0, 0)),
                           pl.BlockSpec((None, T, 2 * D_V), lambda h, i: (h, 0, 0))],
                 out_specs=[pl.BlockSpec((tq, D_V), lambda h, i: (i, h)),
                            pl.BlockSpec((None, tq, LANES), lambda h, i: (h, i, 0))],
                 out_shape=[_sds((T, N_HEADS * D_V), BF16), _sds((N_HEADS, T, LANES), F32)],
                 scratch=[pltpu.VMEM((tq, LANES), F32), pltpu.VMEM((tq, 2 * D_V), F32)])(q, k, v1)


def _halo_specs(tm, cb, n_t):
    r = tm // HALO
    return [pl.BlockSpec((HALO, cb), lambda jc, i: (jnp.maximum(i * r - 1, 0), jc)),
            pl.BlockSpec((tm, cb), lambda jc, i: (i, jc)),
            pl.BlockSpec((HALO, cb), lambda jc, i: (jnp.minimum((i + 1) * r, n_t * r - 1), jc))]


def _fill_ext(ext, prev_ref, cur_ref, next_ref, i, n_t, tm):
    ext[0:HALO, :] = jnp.where(i > 0, prev_ref[...], 0.0)
    ext[HALO:HALO + tm, :] = cur_ref[...]
    ext[HALO + tm:, :] = jnp.where(i < n_t - 1, next_ref[...], 0.0)


FLASH_TQ = 512
FLASH_TK = 512
EPILOGUE_ROWS = 128
CONV_ROWS = 64


def _conv_fwd(u, w_pad, bias):
    T, C = u.shape
    tm, cb = _tile(T, 256), _tile(C, 256)
    n_t = T // tm
    rb = min(CONV_ROWS, tm)

    def body(up, uc, un, w_ref, b_ref, c_o, ext):
        i = pl.program_id(1)
        _fill_ext(ext, up, uc, un, i, n_t, tm)
        for r0 in range(0, tm, rb):
            acc = jnp.zeros((rb, cb), F32) + b_ref[...]
            for k in range(CONV_W):
                acc = acc + w_ref[k:k + 1, :] * ext[r0 + k + 1:r0 + k + 1 + rb, :]
            c_o[r0:r0 + rb, :] = acc

    return _call(body, name="conv_fwd", grid=(C // cb, n_t),
                 in_specs=_halo_specs(tm, cb, n_t) + [pl.BlockSpec((CONV_W_PAD, cb), lambda jc, i: (0, jc)),
                                                      pl.BlockSpec((1, cb), lambda jc, i: (0, jc))],
                 out_specs=pl.BlockSpec((tm, cb), lambda jc, i: (i, jc)),
                 out_shape=_sds((T, C), F32),
                 scratch=[pltpu.VMEM((tm + 2 * HALO, cb), F32)])(u, u, u, w_pad, bias)


def _conv_post(c, g, b):
    T, C = c.shape
    tm = _tile(T, 512)

    def body(c_ref, g_ref, b_ref, o_ref):
        xhat, _ = _ln_stats(c_ref[...])
        y = xhat * g_ref[...] + b_ref[...]
        o_ref[...] = (y * _sigmoid(y)).astype(BF16)

    blk = pl.BlockSpec((tm, C), lambda i: (i, 0))
    vec = pl.BlockSpec((1, C), lambda i: (0, 0))
    return _call(body, name="conv_ln_silu", grid=(T // tm,), in_specs=[blk, vec, vec], out_specs=blk,
                 out_shape=_sds((T, C), BF16))(c, g, b)


def _conv_post_bwd(dcat, c, g, b):
    T, C = c.shape
    tm = _tile(T, 512)

    def body(d_ref, c_ref, g_ref, b_ref, dc_o, dg_o, db_o):
        first = pl.program_id(0) == 0
        xhat, rstd = _ln_stats(c_ref[...])
        y = xhat * g_ref[...] + b_ref[...]
        sg = _sigmoid(y)
        dy = d_ref[...] * (sg * (1.0 + y * (1.0 - sg)))
        _acc_out(dg_o, _colsum(dy * xhat), first)
        _acc_out(db_o, _colsum(dy), first)
        dc_o[...] = _ln_bwd(dy, xhat, rstd, g_ref[...])

    blk = pl.BlockSpec((tm, C), lambda i: (i, 0))
    vec = pl.BlockSpec((1, C), lambda i: (0, 0))
    return _call(body, name="conv_ln_silu_bwd", grid=(T // tm,),
                 in_specs=[pl.BlockSpec((tm, C), lambda i: (i, 1)), blk, vec, vec],
                 out_specs=[blk, vec, vec],
                 out_shape=[_sds((T, C), F32), _sds((1, C), F32), _sds((1, C), F32)])(dcat, c, g, b)


def _conv_bwd(dc, u, w_pad):
    T, C = u.shape
    tm, cb = _tile(T, 256), _tile(C, 256)
    n_t = T // tm
    rb = min(CONV_ROWS, tm)

    def body(dp, dcur, dn, up, uc, un, w_ref, du_o, dw_o, db_o, dext, uext, dw_sc):
        i = pl.program_id(1)
        _fill_ext(dext, dp, dcur, dn, i, n_t, tm)
        _fill_ext(uext, up, uc, un, i, n_t, tm)

        @pl.when(i == 0)
        def _():
            dw_sc[...] = jnp.zeros_like(dw_sc)

        for r0 in range(0, tm, rb):
            acc = jnp.zeros((rb, cb), F32)
            d_here = dcur[r0:r0 + rb, :]
            for k in range(CONV_W):
                acc = acc + w_ref[k:k + 1, :] * dext[r0 + 2 * HALO - 1 - k:r0 + 2 * HALO - 1 - k + rb, :]
                prod = d_here * uext[r0 + k + 1:r0 + k + 1 + rb, :]
                dw_sc[k] += jnp.sum(prod.reshape(rb // 8, 8, cb), axis=0)
            du_o[r0:r0 + rb, :] = acc
        dw_sc[CONV_W] += jnp.sum(dcur[...].reshape(tm // 8, 8, cb), axis=0)

        @pl.when(i == n_t - 1)
        def _():
            red = jnp.sum(dw_sc[...], axis=1)
            row = lax.broadcasted_iota(jnp.int32, red.shape, 0)
            dw_o[...] = jnp.where(row < CONV_W, red, 0.0)
            db_o[...] = jnp.sum(jnp.where(row == CONV_W, red, 0.0), axis=0, keepdims=True)

    return _call(body, name="conv_bwd", grid=(C // cb, n_t),
                 in_specs=_halo_specs(tm, cb, n_t) + _halo_specs(tm, cb, n_t)
                 + [pl.BlockSpec((CONV_W_PAD, cb), lambda jc, i: (0, jc))],
                 out_specs=[pl.BlockSpec((tm, cb), lambda jc, i: (i, jc)),
                            pl.BlockSpec((CONV_W_PAD, cb), lambda jc, i: (0, jc)),
                            pl.BlockSpec((1, cb), lambda jc, i: (0, jc))],
                 out_shape=[_sds((T, C), F32), _sds((CONV_W_PAD, C), F32), _sds((1, C), F32)],
                 scratch=[pltpu.VMEM((tm + 2 * HALO, cb), F32), pltpu.VMEM((tm + 2 * HALO, cb), F32),
                          pltpu.VMEM((CONV_W_PAD, 8, cb), F32)])(dc, dc, dc, u, u, u, w_pad)


def _glu_bwd(du, h, C):
    T = du.shape[0]
    tm = _tile(T, 512)

    def body(du_ref, a_ref, gt_ref, o_ref):
        sg = _sigmoid(gt_ref[...])
        du_v = du_ref[...]
        o_ref[:, :C] = (du_v * sg).astype(BF16)
        o_ref[:, C:] = (du_v * a_ref[...] * sg * (1.0 - sg)).astype(BF16)

    return _call(body, name="glu_bwd", grid=(T // tm,),
                 in_specs=[pl.BlockSpec((tm, C), lambda i: (i, 0)), pl.BlockSpec((tm, C), lambda i: (i, 0)),
                           pl.BlockSpec((tm, C), lambda i: (i, 1))],
                 out_specs=pl.BlockSpec((tm, 2 * C), lambda i: (i, 0)),
                 out_shape=_sds((T, 2 * C), BF16))(du, h, h)


def _attn_delta(dcat, attn):
    T = attn.shape[0]
    tm = _tile(T, 512)

    def body(d_ref, o_ref, dl_o, dob_o):
        d = d_ref[...]
        dl = jnp.sum(d * o_ref[...].astype(F32), axis=1, keepdims=True)
        dl_o[...] = jnp.broadcast_to(dl, (tm, LANES))
        dob_o[...] = d.astype(BF16)

    blk = pl.BlockSpec((tm, D_V), lambda i, h: (i, h))
    hblk = pl.BlockSpec((None, tm, D_V), lambda i, h: (h, i, 0))
    return _call(body, name="attn_delta", grid=(T // tm, N_HEADS), in_specs=[blk, blk], out_specs=[hblk, hblk],
                 out_shape=[_sds((N_HEADS, T, LANES), F32), _sds((N_HEADS, T, D_V), BF16)])(dcat, attn)


def _flash_bwd(q, k, v1, do, lse, delta):
    _, T, _ = q.shape
    tq, tk = _tile(T, FLASH_TQ), _tile(T, FLASH_TK)
    nkv, reps = T // tk, tk // LANES

    def body(q_ref, do_ref, lse_ref, dl_ref, k_ref, v_ref, dq_o, dk_o, dv_o, dq_sc):
        @pl.when(pl.program_id(1) == 0)
        def _():
            dk_o[...] = jnp.zeros_like(dk_o)
            dv_o[...] = jnp.zeros_like(dv_o)

        qv, dov = q_ref[...], do_ref[...]
        lse_t = jnp.tile(lse_ref[...], (1, reps))
        dl_t = jnp.tile(dl_ref[...], (1, reps))
        dq_sc[...] = jnp.zeros_like(dq_sc)

        def rows(j):
            return pl.ds(pl.multiple_of(j * tk, tk), tk)

        def scores(j):
            s = lax.dot_general(qv, k_ref[rows(j), :], NT, preferred_element_type=F32)
            dp = lax.dot_general(dov, v_ref[rows(j), :D_V], NT, preferred_element_type=F32)
            return s, dp

        def update(s_dp, j):
            s, dp = s_dp
            p = jnp.exp2(s - lse_t)
            ds = (p * (dp - dl_t)).astype(BF16)
            dv_o[rows(j), :] += lax.dot_general(p.astype(BF16), dov, TN, preferred_element_type=F32)
            dk_o[rows(j), :] += lax.dot_general(ds, qv, TN, preferred_element_type=F32)
            dq_sc[...] += jnp.dot(ds, k_ref[rows(j), :], preferred_element_type=F32)

        def step(j, carry):
            update(scores(j), j)
            return carry

        lax.fori_loop(0, nkv, step, 0)
        dq_o[...] = dq_sc[...]

    def tile(w):
        return pl.BlockSpec((None, tq, w), lambda h, i: (h, i, 0))

    def whole(w):
        return pl.BlockSpec((None, T, w), lambda h, i: (h, 0, 0))

    return _call(body, name="flash_bwd", grid=(N_HEADS, T // tq),
                 in_specs=[tile(D_HEAD_PAD), tile(D_V), tile(LANES), tile(LANES), whole(D_HEAD_PAD), whole(2 * D_V)],
                 out_specs=[tile(D_HEAD_PAD), whole(D_HEAD_PAD), whole(D_V)],
                 out_shape=[_sds((N_HEADS, T, D_HEAD_PAD), F32), _sds((N_HEADS, T, D_HEAD_PAD), F32),
                            _sds((N_HEADS, T, D_V), F32)],
                 scratch=[pltpu.VMEM((tq, D_HEAD_PAD), F32)])(q, do, lse, delta, k, v1)


def _dq_post(dq, tabs):
    _, T, _ = dq.shape
    tm = _tile(T, 512)

    def body(d_ref, cp, sa, sb, o_ref):
        d = d_ref[...] * SCALE
        o_ref[:, :D_NOPE] = d[:, :D_NOPE].astype(BF16)
        o_ref[:, D_NOPE:] = _unrope(d[:, D_NOPE:], cp[...], sa[...], sb[...]).astype(BF16)

    blk = pl.BlockSpec((None, tm, D_HEAD_PAD), lambda i, h: (h, i, 0))
    tab = pl.BlockSpec((tm, LANES), lambda i, h: (i, 0))
    return _call(body, name="dq_unrope", grid=(T // tm, N_HEADS), in_specs=[blk, tab, tab, tab], out_specs=blk,
                 out_shape=_sds(dq.shape, BF16))(dq, *tabs)


def _dk_post(dk, dv, tabs):
    _, T, _ = dk.shape
    tm = _tile(T, 512)

    def body(dk_ref, dv_ref, cp, sa, sb, dkv_o, dkr_o, sc):
        h = pl.program_id(1)
        d = dk_ref[...] * LN_2
        dkv_o[:, :D_NOPE] = d[:, :D_NOPE].astype(BF16)
        dkv_o[:, D_NOPE:] = dv_ref[...].astype(BF16)

        @pl.when(h == 0)
        def _():
            sc[...] = d[:, D_NOPE:]

        @pl.when(h > 0)
        def _():
            sc[...] += d[:, D_NOPE:]

        @pl.when(h == N_HEADS - 1)
        def _():
            dkr_o[...] = _unrope(sc[...], cp[...], sa[...], sb[...]).astype(BF16)

    tab = pl.BlockSpec((tm, LANES), lambda i, h: (i, 0))
    return _call(body, name="dk_unrope", grid=(T // tm, N_HEADS),
                 in_specs=[pl.BlockSpec((None, tm, D_HEAD_PAD), lambda i, h: (h, i, 0)),
                           pl.BlockSpec((None, tm, D_V), lambda i, h: (h, i, 0)), tab, tab, tab],
                 out_specs=[pl.BlockSpec((None, tm, D_HEAD_PAD), lambda i, h: (h, i, 0)), tab],
                 out_shape=[_sds((N_HEADS, T, D_HEAD_PAD), BF16), _sds((T, LANES), BF16)],
                 scratch=[pltpu.VMEM((tm, LANES), F32)])(dk, dv, *tabs)


def _latent_bwd(name, dproj, w_heads, h, col_blk, g):
    _, T, _ = dproj.shape
    tm = _tile(T, 512)

    def ep(acc, ex, out, first):
        dx, dg = _rms_bwd(acc, ex[0][...], ex[1][...])
        out[0][...] = dx.astype(BF16)
        _acc_out(out[1], dg, first)

    return _matmul(name, dproj, w_heads, dims=NT, grid=(T // tm, 1, N_HEADS),
                   a_spec=pl.BlockSpec((None, tm, D_HEAD_PAD), lambda i, j, k: (k, i, 0)),
                   b_spec=pl.BlockSpec((None, LORA, D_HEAD_PAD), lambda i, j, k: (k, 0, 0)),
                   acc_shape=(tm, LORA),
                   extras=[(h, pl.BlockSpec((tm, LORA), lambda i, j, k: (i, col_blk))),
                           (g, pl.BlockSpec((1, LORA), lambda i, j, k: (0, 0)))],
                   outs=[(_sds((T, LORA), BF16), pl.BlockSpec((tm, LORA), lambda i, j, k: (i, 0))),
                         (_sds((1, LORA), F32), pl.BlockSpec((1, LORA), lambda i, j, k: (0, 0)))],
                   epilogue=ep)


def _head_weight_grad(name, latent, dproj):
    _, T, _ = dproj.shape
    tk = _tile(T, 512)
    return _matmul(name, latent, dproj, dims=TN, grid=(N_HEADS, 1, T // tk),
                   a_spec=pl.BlockSpec((tk, LORA), lambda i, j, k: (k, 0)),
                   b_spec=pl.BlockSpec((None, tk, D_HEAD_PAD), lambda i, j, k: (i, k, 0)),
                   acc_shape=(LORA, D_HEAD_PAD),
                   outs=[(_sds((N_HEADS, LORA, D_HEAD_PAD), F32),
                          pl.BlockSpec((None, LORA, D_HEAD_PAD), lambda i, j, k: (i, 0, 0)))],
                   epilogue=_store())[0]


def _weight_grad(name, a, b, tm_pref=1024, tn_pref=1024, stacked_cols=None):
    T, M = a.shape
    N = b.shape[1]
    tk = _tile(T, 512)
    tm = _tile(M, tm_pref)
    if stacked_cols is None:
        tn = _tile(N, tn_pref)
        out = (_sds((M, N), F32), pl.BlockSpec((tm, tn), lambda i, j, k: (i, j)))
    else:
        tn = _tile(stacked_cols, tn_pref)
        per = stacked_cols // tn
        out = (_sds((N // stacked_cols, M, stacked_cols), F32),
               pl.BlockSpec((None, tm, tn), lambda i, j, k: (j // per, i, j % per)))
    return _matmul(name, a, b, dims=TN, grid=(M // tm, N // tn, T // tk),
                   a_spec=pl.BlockSpec((tk, tm), lambda i, j, k: (k, i)),
                   b_spec=pl.BlockSpec((tk, tn), lambda i, j, k: (k, j)),
                   acc_shape=(tm, tn), outs=[out], epilogue=_store())[0]


def _small_names():
    return ["ln_in_g", "ln_in_b", "g_cq", "g_ckv", "conv_b", "g_conv_ln", "b_conv_ln", "g_ln1", "b_ln1", "g_ln2", "b_ln2"]


def _pack(vecs):
    flat = jnp.concatenate([v.reshape(-1) for v in vecs])
    assert flat.shape[0] % (8 * LANES) == 0
    return flat.reshape(-1, LANES)


def _unpack(packed, like):
    flat, out, off = packed.reshape(-1), [], 0
    for v in like:
        out.append(flat[off:off + v.size].reshape(v.shape))
        off += v.size
    return out


def kernel(x, positions, ln_in_g, ln_in_b, w_in, g_cq, w_uq, g_ckv, w_uk, w_uv, conv_w, conv_b, g_conv_ln, b_conv_ln, w_out, g_ln1, b_ln1, w_ff1, w_ff2, g_ln2, b_ln2, loss_target, m_ln_in_g, m_ln_in_b, m_w_in, m_g_cq, m_w_uq, m_g_ckv, m_w_uk, m_w_uv, m_conv_w, m_conv_b, m_g_conv_ln, m_b_conv_ln, m_w_out, m_g_ln1, m_b_ln1, m_w_ff1, m_w_ff2, m_g_ln2, m_b_ln2, v_ln_in_g, v_ln_in_b, v_w_in, v_g_cq, v_w_uq, v_g_ckv, v_w_uk, v_w_uv, v_conv_w, v_conv_b, v_g_conv_ln, v_b_conv_ln, v_w_out, v_g_ln1, v_b_ln1, v_w_ff1, v_w_ff2, v_g_ln2, v_b_ln2):
    args = dict(locals())
    T, D = x.shape[1], x.shape[2]
    C = D - N_HEADS * D_V
    Fs = w_ff1.shape[2]
    F = N_DEV * Fs
    n_in = N_DEV * w_in.shape[2]
    n_in_p = 2 * C + 2 * LORA + LANES
    assert n_in == 2 * LORA + D_ROPE + 2 * C and w_uq.shape[2] == D_QK and conv_w.shape[2] * N_DEV == C

    xs, tgt = x[0], loss_target[0]
    row = lambda v_: v_.reshape(1, -1)

    g_w_in, g_w_uq, g_w_uk, g_w_uv, g_conv_w, g_w_out, g_w_ff1, g_w_ff2 = _all_gather(
        [w_in[0].astype(BF16), w_uq[0].astype(BF16), w_uk[0].astype(BF16), w_uv[0].astype(BF16), conv_w[0],
         w_out[0].astype(BF16), w_ff1[0].astype(BF16), w_ff2[0].astype(BF16)])
    w_in_f = jnp.transpose(g_w_in, (1, 0, 2)).reshape(D, n_in)
    s_cq, s_ckv, s_kr, s_a, s_g = 0, LORA, 2 * LORA, 2 * LORA + D_ROPE, 2 * LORA + D_ROPE + C
    w_in_p = jnp.concatenate([w_in_f[:, s_a:s_g], w_in_f[:, s_g:], w_in_f[:, s_cq:s_ckv], w_in_f[:, s_ckv:s_kr],
                              w_in_f[:, s_kr:s_a], jnp.zeros((D, LANES - D_ROPE), BF16)], axis=1)
    w_uq_p = jnp.pad(g_w_uq, ((0, 0), (0, 0), (0, D_HEAD_PAD - D_QK)))
    w_ukv = jnp.concatenate([g_w_uk, g_w_uv], axis=2)
    conv_w_f = jnp.pad(jnp.transpose(g_conv_w, (1, 0, 2)).reshape(CONV_W, C), ((0, CONV_W_PAD - CONV_W), (0, 0)))
    w_out_f = g_w_out.reshape(D, D)
    w_ff2_f = g_w_ff2.reshape(F, D)

    half = D_ROPE // 2
    inv_freq = ROPE_BASE ** (-jnp.arange(half, dtype=F32) * (2.0 / D_ROPE))
    inv_freq = jnp.tile(inv_freq, LANES // half).reshape(1, LANES)
    tabs = _rope_tables(positions.reshape(T, 1), inv_freq)

    x0, x0b = _ln_in(xs, row(ln_in_g), row(ln_in_b))

    tm, tn = _tile(T, 512), _tile(n_in_p, 640)
    h = _matmul("h_proj", x0b, w_in_p, dims=NN, grid=(T // tm, n_in_p // tn, 1),
                a_spec=pl.BlockSpec((tm, D), lambda i, j, k: (i, 0)),
                b_spec=pl.BlockSpec((D, tn), lambda i, j, k: (0, j)), acc_shape=(tm, tn),
                outs=[(_sds((T, n_in_p), F32), pl.BlockSpec((tm, tn), lambda i, j, k: (i, j)))],
                epilogue=_store())[0]

    u, cqn, ckvn, kr = _mid(h, g_cq, g_ckv, tabs, C)
    q = _q_proj(cqn, w_uq_p, tabs)
    kf, vf = _kv_proj(ckvn, w_ukv, kr)
    attn, lse = _flash_fwd(q, kf, vf)
    conv_c = _conv_fwd(u, conv_w_f, conv_b)
    conv_out = _conv_post(conv_c, g_conv_ln, b_conv_ln)
    cat = jnp.concatenate([attn, conv_out], axis=1)

    def ep_ln1(acc, ex, out, first):
        z1 = ALPHA * ex[0][...] + acc
        xhat, _ = _ln_stats(z1)
        x1 = xhat * ex[1][...] + ex[2][...]
        out[0][...] = z1
        out[1][...] = x1
        out[2][...] = x1.astype(BF16)

    tm = _tile(T, 256)
    rowblk = pl.BlockSpec((tm, D), lambda i, j, k: (i, 0))
    vecD = pl.BlockSpec((1, D), lambda i, j, k: (0, 0))
    z1, x1, x1b = _matmul("mix_ln1", cat, w_out_f, dims=NN, grid=(T // tm, 1, 1), a_spec=rowblk,
                          b_spec=pl.BlockSpec((D, D), lambda i, j, k: (0, 0)), acc_shape=(tm, D),
                          extras=[(x0, rowblk), (g_ln1, vecD), (b_ln1, vecD)],
                          outs=[(_sds((T, D), F32), rowblk), (_sds((T, D), F32), rowblk), (_sds((T, D), BF16), rowblk)],
                          epilogue=ep_ln1, ep_rows=EPILOGUE_ROWS)

    def ep_ff1(acc, ex, out, first):
        r = jnp.maximum(acc, 0.0)
        out[0][...] = (r * r).astype(BF16)
        out[1][...] = r.astype(BF16)

    tm, tn = _tile(T, 512), _tile(Fs, 1024)
    per = Fs // tn
    fblk = pl.BlockSpec((tm, tn), lambda i, j, k: (i, j))
    f_act, r_act = _matmul("ff1_relu2", x1b, g_w_ff1, dims=NN, grid=(T // tm, F // tn, 1),
                           a_spec=pl.BlockSpec((tm, D), lambda i, j, k: (i, 0)),
                           b_spec=pl.BlockSpec((None, D, tn), lambda i, j, k: (j // per, 0, j % per)),
                           acc_shape=(tm, tn), outs=[(_sds((T, F), BF16), fblk), (_sds((T, F), BF16), fblk)],
                           epilogue=ep_ff1)

    def ep_ln2(acc, ex, out, first):
        g2 = ex[2][...]
        z2 = ALPHA * ex[0][...] + acc
        xhat, rstd = _ln_stats(z2)
        err = xhat * g2 + ex[3][...] - ex[1][...]
        part = 0.5 * jnp.sum(jnp.mean(err * err, axis=-1, keepdims=True))
        _acc_out(out[2], jnp.zeros((8, LANES), F32) + part, first)
        dy = err * (1.0 / D)
        _acc_out(out[3], _colsum(dy * xhat), first)
        _acc_out(out[4], _colsum(dy), first)
        dz2 = _ln_bwd(dy, xhat, rstd, g2)
        out[0][...] = dz2
        out[1][...] = dz2.astype(BF16)

    tm, tk = _tile(T, 512), _tile(F, 512)
    rowblk = pl.BlockSpec((tm, D), lambda i, j, k: (i, 0))
    dz2, dz2b, loss_blk, dg_ln2, db_ln2 = _matmul(
        "ff2_ln2_loss", f_act, w_ff2_f, dims=NN, grid=(T // tm, 1, F // tk),
        a_spec=pl.BlockSpec((tm, tk), lambda i, j, k: (i, k)), b_spec=pl.BlockSpec((tk, D), lambda i, j, k: (k, 0)),
        acc_shape=(tm, D), extras=[(x1, rowblk), (tgt, rowblk), (g_ln2, vecD), (b_ln2, vecD)],
        outs=[(_sds((T, D), F32), rowblk), (_sds((T, D), BF16), rowblk),
              (_sds((8, LANES), F32), pl.BlockSpec((8, LANES), lambda i, j, k: (0, 0))),
              (_sds((1, D), F32), vecD), (_sds((1, D), F32), vecD)],
        epilogue=ep_ln2, ep_rows=EPILOGUE_ROWS)
    loss = lax.psum(loss_blk[0, 0], ("x", "y", "c"))

    def ep_dpre(acc, ex, out, first):
        out[0][...] = (acc * (2.0 * ex[0][...].astype(F32))).astype(BF16)

    tm, tn = _tile(T, 512), _tile(F, 1024)
    fblk = pl.BlockSpec((tm, tn), lambda i, j, k: (i, j))
    dpre = _matmul("ff2_dgrad", dz2b, w_ff2_f, dims=NT, grid=(T // tm, F // tn, 1),
                   a_spec=pl.BlockSpec((tm, D), lambda i, j, k: (i, 0)), b_spec=pl.BlockSpec((tn, D), lambda i, j, k: (j, 0)),
                   acc_shape=(tm, tn), extras=[(r_act, fblk)], outs=[(_sds((T, F), BF16), fblk)], epilogue=ep_dpre)[0]

    dw_ff2 = _weight_grad("ff2_wgrad", f_act, dz2b).reshape(N_DEV, Fs, D)
    dw_ff1 = _weight_grad("ff1_wgrad", x1b, dpre, stacked_cols=Fs)

    def ep_ln1_bwd(acc, ex, out, first):
        dx1 = ALPHA * ex[0][...] + acc
        xhat, rstd = _ln_stats(ex[1][...])
        _acc_out(out[2], _colsum(dx1 * xhat), first)
        _acc_out(out[3], _colsum(dx1), first)
        dz1 = _ln_bwd(dx1, xhat, rstd, ex[2][...])
        out[0][...] = dz1
        out[1][...] = dz1.astype(BF16)

    tm, tk = _tile(T, 512), _tile(Fs, 512)
    per = Fs // tk
    rowblk = pl.BlockSpec((tm, D), lambda i, j, k: (i, 0))
    dz1, dz1b, dg_ln1, db_ln1 = _matmul(
        "ff1_dgrad_ln1_bwd", dpre, g_w_ff1, dims=NT, grid=(T // tm, 1, F // tk),
        a_spec=pl.BlockSpec((tm, tk), lambda i, j, k: (i, k)),
        b_spec=pl.BlockSpec((None, D, tk), lambda i, j, k: (k // per, 0, k % per)),
        acc_shape=(tm, D), extras=[(dz2, rowblk), (z1, rowblk), (g_ln1, vecD)],
        outs=[(_sds((T, D), F32), rowblk), (_sds((T, D), BF16), rowblk), (_sds((1, D), F32), vecD), (_sds((1, D), F32), vecD)],
        epilogue=ep_ln1_bwd, ep_rows=EPILOGUE_ROWS)

    dw_out = _weight_grad("out_wgrad", cat, dz1b).reshape(N_DEV, D // N_DEV, D)
    tm, tn = _tile(T, 512), _tile(D, 1024)
    dcat = _matmul("out_dgrad", dz1b, w_out_f, dims=NT, grid=(T // tm, D // tn, 1),
                   a_spec=pl.BlockSpec((tm, D), lambda i, j, k: (i, 0)), b_spec=pl.BlockSpec((tn, D), lambda i, j, k: (j, 0)),
                   acc_shape=(tm, tn), outs=[(_sds((T, D), F32), pl.BlockSpec((tm, tn), lambda i, j, k: (i, j)))],
                   epilogue=_store())[0]

    dc, dg_conv_ln, db_conv_ln = _conv_post_bwd(dcat, conv_c, g_conv_ln, b_conv_ln)
    du, dconv_w_p, dconv_b = _conv_bwd(dc, u, conv_w_f)
    dconv_in = _glu_bwd(du, h, C)

    delta, do_heads = _attn_delta(dcat, attn)
    dq, dk, dv = _flash_bwd(q, kf, vf, do_heads, lse, delta)
    dq_raw = _dq_post(dq, tabs)
    dkv, dkr = _dk_post(dk, dv, tabs)
    cq_blk = (2 * C) // LORA
    dcq, dg_cq = _latent_bwd("q_dgrad_rms_bwd", dq_raw, w_uq_p, h, cq_blk, g_cq)
    dckv, dg_ckv = _latent_bwd("kv_dgrad_rms_bwd", dkv, w_ukv, h, cq_blk + 1, g_ckv)
    dw_uq = _head_weight_grad("uq_wgrad", cqn, dq_raw)[:, :, :D_QK]
    dw_ukv = _head_weight_grad("ukv_wgrad", ckvn, dkv)
    dw_uk, dw_uv = dw_ukv[:, :, :D_NOPE], dw_ukv[:, :, D_NOPE:]

    dh = jnp.concatenate([dconv_in, dcq, dckv, dkr], axis=1)
    dw_in_p = _weight_grad("in_wgrad", x0b, dh, tn_pref=640)
    dw_in_f = jnp.concatenate([dw_in_p[:, 2 * C:2 * C + 2 * LORA + D_ROPE], dw_in_p[:, :2 * C]], axis=1)
    dw_in = jnp.transpose(dw_in_f.reshape(D, N_DEV, n_in // N_DEV), (1, 0, 2))

    def ep_ln_in_bwd(acc, ex, out, first):
        dx0 = ALPHA * ex[0][...] + acc
        xhat, rstd = _ln_stats(ex[1][...])
        _acc_out(out[1], _colsum(dx0 * xhat), first)
        _acc_out(out[2], _colsum(dx0), first)
        out[0][...] = _ln_bwd(dx0, xhat, rstd, ex[2][...])

    tm, tk = _tile(T, 512), _tile(n_in_p, 640)
    rowblk = pl.BlockSpec((tm, D), lambda i, j, k: (i, 0))
    grad_x, dg_ln_in, db_ln_in = _matmul(
        "in_dgrad_ln_in_bwd", dh, w_in_p, dims=NT, grid=(T // tm, 1, n_in_p // tk),
        a_spec=pl.BlockSpec((tm, tk), lambda i, j, k: (i, k)), b_spec=pl.BlockSpec((D, tk), lambda i, j, k: (0, k)),
        acc_shape=(tm, D), extras=[(dz1, rowblk), (xs, rowblk), (row(ln_in_g), vecD)],
        outs=[(_sds((T, D), F32), rowblk), (_sds((1, D), F32), vecD), (_sds((1, D), F32), vecD)],
        epilogue=ep_ln_in_bwd, ep_rows=EPILOGUE_ROWS)

    dconv_w = jnp.transpose(dconv_w_p[:CONV_W].reshape(CONV_W, N_DEV, C // N_DEV), (1, 0, 2))
    small = dict(ln_in_g=dg_ln_in, ln_in_b=db_ln_in, g_cq=dg_cq, g_ckv=dg_ckv, conv_b=dconv_b, g_conv_ln=dg_conv_ln,
                 b_conv_ln=db_conv_ln, g_ln1=dg_ln1, b_ln1=db_ln1, g_ln2=dg_ln2, b_ln2=db_ln2)
    names = _small_names()
    big = dict(w_in=dw_in, w_uq=dw_uq, w_uk=dw_uk, w_uv=dw_uv, conv_w=dconv_w, w_out=dw_out, w_ff1=dw_ff1, w_ff2=dw_ff2)
    big_names = list(big)
    parts = _reduce_scatter_exchange([big[n] for n in big_names], [_pack([small[n] for n in names])])

    res = {}
    for n, p in zip(big_names, parts[:len(big_names)]):
        res[n] = [o.reshape(args[n].shape) for o in _adamw("adamw_" + n, p, args[n][0], args["m_" + n][0], args["v_" + n][0])]
    packed = _adamw("adamw_small", parts[-1], _pack([args[n] for n in names]), _pack([args["m_" + n] for n in names]),
                    _pack([args["v_" + n] for n in names]))
    like = [args[n] for n in names]
    unpacked = [_unpack(p, like) for p in packed]
    for i, n in enumerate(names):
        res[n] = [unpacked[kind][i] for kind in range(4)]

    order = ["ln_in_g", "ln_in_b", "w_in", "g_cq", "w_uq", "g_ckv", "w_uk", "w_uv", "conv_w", "conv_b", "g_conv_ln",
             "b_conv_ln", "w_out", "g_ln1", "b_ln1", "w_ff1", "w_ff2", "g_ln2", "b_ln2"]
    outs = [loss, grad_x.reshape(x.shape)]
    for kind in range(4):
        outs += [res[n][kind] for n in order]
    return tuple(outs)
```

```python
import jax
import jax.numpy as jnp
from jax import lax
from jax.experimental import pallas as pl
from jax.experimental.pallas import tpu as pltpu

F32 = jnp.float32
BF16 = jnp.bfloat16

N_HEADS = 8
D_NOPE = 128
D_ROPE = 64
D_V = 128
D_QK = D_NOPE + D_ROPE
D_HEAD_PAD = 256
LORA = 512
CONV_W = 31
CONV_HALF = CONV_W // 2
CONV_W_PAD = 32
HALO = 16
LN_EPS = 1e-5
RMS_EPS = 1e-6
ALPHA = 2.0 ** 0.25
SCALE = float(D_QK) ** -0.5
LOG2_E = 1.4426950408889634
LN_2 = 0.6931471805599453
Q_SCALE = SCALE * LOG2_E
ROPE_BASE = 10000.0
ADAM_LR, ADAM_B1, ADAM_B2, ADAM_EPS, ADAM_WD, ADAM_STEP = 0.001, 0.9, 0.999, 1e-08, 0.01, 10

N_DEV = 8
LANES = 128
VMEM_LIMIT_V7X = 56 * 1024 * 1024

NN = (((1,), (0,)), ((), ()))
NT = (((1,), (1,)), ((), ()))
TN = (((0,), (0,)), ((), ()))


def _call(body, *, name, grid, in_specs, out_specs, out_shape, scratch=()):
    params = pltpu.CompilerParams(dimension_semantics=("arbitrary",) * len(grid),
                                  vmem_limit_bytes=VMEM_LIMIT_V7X)
    return pl.pallas_call(body, name=name, grid=grid, in_specs=in_specs, out_specs=out_specs,
                          out_shape=out_shape, scratch_shapes=scratch, compiler_params=params)


def _tile(n, pref):
    if n <= pref:
        return n
    t = (pref // LANES) * LANES
    while t > LANES and n % t:
        t -= LANES
    assert n % t == 0, (n, pref)
    return t


def _sds(shape, dtype):
    return jax.ShapeDtypeStruct(shape, dtype)


def _ln_stats(z):
    mu = jnp.mean(z, axis=-1, keepdims=True)
    zc = z - mu
    var = jnp.mean(zc * zc, axis=-1, keepdims=True)
    rstd = lax.rsqrt(var + LN_EPS)
    return zc * rstd, rstd


def _ln_bwd(dy, xhat, rstd, g):
    gd = dy * g
    m1 = jnp.mean(gd, axis=-1, keepdims=True)
    m2 = jnp.mean(gd * xhat, axis=-1, keepdims=True)
    return rstd * (gd - m1 - xhat * m2)


def _rms(x, g):
    return x * lax.rsqrt(jnp.mean(x * x, axis=-1, keepdims=True) + RMS_EPS) * g


def _rms_bwd(dy, x, g):
    r = lax.rsqrt(jnp.mean(x * x, axis=-1, keepdims=True) + RMS_EPS)
    dxn = dy * g
    dx = r * dxn - x * (r * r * r) * jnp.mean(dxn * x, axis=-1, keepdims=True)
    dg = jnp.sum(dy * x * r, axis=0, keepdims=True)
    return dx, dg


def _sigmoid(x):
    return 1.0 / (1.0 + jnp.exp(-x))


def _rope(x, cos_p, sin_a, sin_b):
    return x * cos_p + pltpu.roll(x, 96, 1) * sin_a + pltpu.roll(x, 32, 1) * sin_b


def _unrope(d, cos_p, sin_a, sin_b):
    return d * cos_p - pltpu.roll(d, 96, 1) * sin_a - pltpu.roll(d, 32, 1) * sin_b


def _colsum(v):
    return jnp.sum(v, axis=0, keepdims=True)


def _acc_out(ref, val, first):
    if first is False:
        ref[...] += val
        return

    @pl.when(first)
    def _():
        ref[...] = val

    @pl.when(jnp.logical_not(first))
    def _():
        ref[...] += val


class _Rows:
    def __init__(self, ref, sl):
        self.ref, self.sl = ref, sl

    def __getitem__(self, idx):
        assert idx is Ellipsis
        return self.ref[self.sl, :]

    def __setitem__(self, idx, val):
        assert idx is Ellipsis
        self.ref[self.sl, :] = val


def _matmul(name, a, b, *, dims, grid, a_spec, b_spec, acc_shape, outs, epilogue, extras=(), ep_rows=None):
    nk = grid[2]
    ne, no = len(extras), len(outs)
    tm = acc_shape[0]

    def finish(acc_rows, ex, out):
        first = pl.program_id(0) == 0
        if ep_rows is None or ep_rows >= tm:
            epilogue(acc_rows(slice(None)), ex, out, first)
            return
        for r0 in range(0, tm, ep_rows):
            sl = slice(r0, r0 + ep_rows)
            view = lambda r: _Rows(r, sl) if r.shape[0] == tm else r
            epilogue(acc_rows(sl), [view(r) for r in ex], [view(r) for r in out], first if r0 == 0 else False)

    def body(*refs):
        a_ref, b_ref = refs[0], refs[1]
        ex = refs[2:2 + ne]
        out = refs[2 + ne:2 + ne + no]
        part = lax.dot_general(a_ref[...], b_ref[...], dims, preferred_element_type=F32)
        if nk == 1:
            finish(lambda sl: part[sl, :], ex, out)
        else:
            acc = refs[2 + ne + no]
            k = pl.program_id(2)

            @pl.when(k == 0)
            def _():
                acc[...] = part

            @pl.when(k > 0)
            def _():
                acc[...] += part

            @pl.when(k == nk - 1)
            def _():
                finish(lambda sl: acc[sl, :], ex, out)

    scratch = [] if nk == 1 else [pltpu.VMEM(acc_shape, F32)]
    res = _call(body, name=name, grid=grid,
                in_specs=[a_spec, b_spec] + [s for _, s in extras],
                out_specs=[s for _, s in outs],
                out_shape=[o for o, _ in outs],
                scratch=scratch)(a, b, *[e for e, _ in extras])
    return res


def _store(dtype=F32):
    def ep(acc, ex, out, first):
        out[0][...] = acc.astype(dtype)
    return ep


def _mesh_pos():
    return lax.axis_index("x"), lax.axis_index("y"), lax.axis_index("c")


def _flip(v, bit):
    return 1 - v if bit else v


_HBM = pl.BlockSpec(memory_space=pltpu.HBM)
_SEM = pl.BlockSpec(memory_space=pltpu.SEMAPHORE)
_EFFECT = pltpu.SideEffectType.DATAFLOW_SIDE_EFFECTING


def _my_slot():
    x, y, c = _mesh_pos()
    return 4 * x + 2 * y + c


def _exchange_copies(srcs, lands, send_sems, recv_sems, stacked, receives=True):
    x, y, c = _mesh_pos()
    me = 4 * x + 2 * y + c
    pairs = []
    for w in range(len(srcs)):
        for k in range(1, N_DEV):
            peer = (_flip(x, k & 4), _flip(y, k & 2), _flip(c, k & 1))
            peer_slot = 4 * peer[0] + 2 * peer[1] + peer[2]
            to_peer = srcs[w].at[peer_slot] if stacked[w] else srcs[w]
            mine = srcs[w].at[me] if stacked[w] else srcs[w]
            s = w * (N_DEV - 1) + k - 1
            sems = dict(send_sem=send_sems.at[s], recv_sem=recv_sems.at[s],
                        device_id=peer, device_id_type=pl.DeviceIdType.MESH)
            send = pltpu.make_async_remote_copy(src_ref=to_peer, dst_ref=lands[w].at[me], **sems)
            recv = pltpu.make_async_remote_copy(src_ref=mine, dst_ref=lands[w].at[peer_slot], **sems) if receives else None
            pairs.append((send, recv))
    return pairs


def _exchange_start(name, srcs, stacked):
    n = len(srcs)
    land_shapes = [s.shape if st else (N_DEV,) + s.shape for s, st in zip(srcs, stacked)]

    def body(*refs):
        src, land = refs[:n], refs[n:2 * n]
        send_sems, recv_sems = refs[2 * n], refs[2 * n + 1]
        token = refs[-1]
        for send, _ in _exchange_copies(src, land, send_sems, recv_sems, stacked, receives=False):
            send.start()
        token[...] = jnp.zeros_like(token)

    hbm = lambda a: pltpu.with_memory_space_constraint(a, pltpu.HBM)
    outs = pl.pallas_call(
        body, name=name,
        out_shape=(pltpu.SemaphoreType.DMA((n * (N_DEV - 1),)), pltpu.SemaphoreType.DMA((n * (N_DEV - 1),)),
                   *[pltpu.HBM(s.shape, s.dtype) for s in srcs],
                   *[pltpu.HBM(ls, s.dtype) for ls, s in zip(land_shapes, srcs)],
                   _sds((8, LANES), F32)),
        in_specs=[_HBM] * (2 * n),
        out_specs=(_SEM, _SEM, *[_HBM] * (2 * n), pl.BlockSpec(memory_space=pltpu.VMEM)),
        input_output_aliases={i: 2 + i for i in range(2 * n)},
        compiler_params=pltpu.CompilerParams(has_side_effects=_EFFECT),
    )(*[hbm(s) for s in srcs], *[hbm(lax.empty(ls, s.dtype)) for ls, s in zip(land_shapes, srcs)])
    return outs[0], outs[1], list(outs[2:2 + n]), list(outs[2 + n:2 + 2 * n]), outs[-1][0, 0]


def _exchange_wait(name, started, stacked, after):
    srcs, lands = _wait_call(name, started, stacked, after)
    me = _my_slot()
    full = []
    for src, land, st in zip(srcs, lands, stacked):
        own = lax.dynamic_index_in_dim(src, me, 0, keepdims=True) if st else src[None]
        full.append(lax.dynamic_update_index_in_dim(land, own, me, 0))
    return full


def _wait_call(name, started, stacked, after):
    send_sems, recv_sems, srcs, lands, _ = started
    n = len(srcs)

    def body(*refs):
        src, land = refs[:n], refs[n:2 * n]
        s_sems, r_sems = refs[2 * n], refs[2 * n + 1]
        for send, recv in _exchange_copies(src, land, s_sems, r_sems, stacked):
            send.wait_send()
            recv.wait_recv()

    outs = pl.pallas_call(
        body, name=name,
        out_shape=tuple(pltpu.HBM(a.shape, a.dtype) for a in srcs + lands),
        in_specs=[_HBM] * (2 * n) + [_SEM, _SEM, pl.BlockSpec(memory_space=pl.ANY)],
        out_specs=[_HBM] * (2 * n),
        input_output_aliases={i: i for i in range(2 * n)},
        compiler_params=pltpu.CompilerParams(has_side_effects=_EFFECT),
    )(*srcs, *lands, send_sems, recv_sems, after)
    return outs[:n], outs[n:]


def _adamw(name, parts, w, m, v):
    rows, cols = w.shape
    cap = max(8, (LANES * 1024) // cols)
    tr = rows
    if rows > cap:
        tr = (cap // 8) * 8
        while rows % tr:
            tr -= 8
    c1 = 1.0 / (1.0 - ADAM_B1 ** ADAM_STEP)
    c2 = 1.0 / (1.0 - ADAM_B2 ** ADAM_STEP)

    def body(p_ref, w_ref, m_ref, v_ref, g_o, d_o, m_o, v_o):
        g = p_ref[0]
        for s in range(1, N_DEV):
            g = g + p_ref[s]
        mn = ADAM_B1 * m_ref[...] + (1.0 - ADAM_B1) * g
        vn = ADAM_B2 * v_ref[...] + (1.0 - ADAM_B2) * (g * g)
        g_o[...] = g
        m_o[...] = mn
        v_o[...] = vn
        d_o[...] = -ADAM_LR * ((mn * c1) / (jnp.sqrt(vn * c2) + ADAM_EPS) + ADAM_WD * w_ref[...])

    blk = pl.BlockSpec((tr, cols), lambda i: (i, 0))
    return _call(body, name=name, grid=(rows // tr,),
                 in_specs=[pl.BlockSpec((N_DEV, tr, cols), lambda i: (0, i, 0)), blk, blk, blk],
                 out_specs=[blk] * 4, out_shape=[_sds((rows, cols), F32)] * 4)(parts, w, m, v)


def _rope_tables(pos_col, inv_freq):
    T = pos_col.shape[0]
    tm = _tile(T, 1024)

    def body(p_ref, f_ref, c_o, sa_o, sb_o):
        ang = p_ref[...].astype(F32) * f_ref[...]
        lane = lax.broadcasted_iota(jnp.int32, ang.shape, 1)
        cs, sn = jnp.cos(ang), jnp.sin(ang)
        c_o[...] = jnp.where(lane < D_ROPE, cs, 0.0)
        sa_o[...] = jnp.where(lane < D_ROPE // 2, -sn, 0.0)
        sb_o[...] = jnp.where((lane >= D_ROPE // 2) & (lane < D_ROPE), sn, 0.0)

    blk = pl.BlockSpec((tm, LANES), lambda i: (i, 0))
    return _call(body, name="rope_tables", grid=(T // tm,),
                 in_specs=[pl.BlockSpec((tm, 1), lambda i: (i, 0)), pl.BlockSpec((1, LANES), lambda i: (0, 0))],
                 out_specs=[blk] * 3, out_shape=[_sds((T, LANES), F32)] * 3)(pos_col, inv_freq)


def _ln_in(x, g, b):
    T, D = x.shape
    tm = _tile(T, 512)

    def body(x_ref, g_ref, b_ref, o32, o16):
        xhat, _ = _ln_stats(x_ref[...])
        y = xhat * g_ref[...] + b_ref[...]
        o32[...] = y
        o16[...] = y.astype(BF16)

    blk = pl.BlockSpec((tm, D), lambda i: (i, 0))
    vec = pl.BlockSpec((1, D), lambda i: (0, 0))
    return _call(body, name="ln_in", grid=(T // tm,), in_specs=[blk, vec, vec], out_specs=[blk, blk],
                 out_shape=[_sds((T, D), F32), _sds((T, D), BF16)])(x, g, b)


def _mid(h, g_cq, g_ckv, tabs, C):
    T = h.shape[0]
    tm = _tile(T, 256)
    cq_blk, kr_blk = (2 * C) // LORA, (2 * C + 2 * LORA) // LANES

    def body(a_ref, gt_ref, cq_ref, ckv_ref, kr_ref, gq_ref, gkv_ref, cp, sa, sb, u_o, cqn_o, ckvn_o, kr_o):
        u_o[...] = a_ref[...] * _sigmoid(gt_ref[...])
        cqn_o[...] = _rms(cq_ref[...], gq_ref[...]).astype(BF16)
        ckvn_o[...] = _rms(ckv_ref[...], gkv_ref[...]).astype(BF16)
        kr_o[...] = _rope(kr_ref[...], cp[...], sa[...], sb[...]).astype(BF16)

    def col(w, j):
        return pl.BlockSpec((tm, w), lambda i: (i, j))

    vec = pl.BlockSpec((1, LORA), lambda i: (0, 0))
    return _call(body, name="mid_norm_glu", grid=(T // tm,),
                 in_specs=[col(C, 0), col(C, 1), col(LORA, cq_blk), col(LORA, cq_blk + 1), col(LANES, kr_blk),
                           vec, vec, col(LANES, 0), col(LANES, 0), col(LANES, 0)],
                 out_specs=[col(C, 0), col(LORA, 0), col(LORA, 0), col(LANES, 0)],
                 out_shape=[_sds((T, C), F32), _sds((T, LORA), BF16), _sds((T, LORA), BF16), _sds((T, LANES), BF16)],
                 )(h, h, h, h, h, g_cq, g_ckv, *tabs)


def _q_proj(cqn, w_uq_p, tabs):
    T = cqn.shape[0]
    tm = _tile(T, 512)

    def body(c_ref, w_ref, cp, sa, sb, o_ref):
        q = jnp.dot(c_ref[...], w_ref[...], preferred_element_type=F32)
        o_ref[:, :D_NOPE] = (q[:, :D_NOPE] * Q_SCALE).astype(BF16)
        o_ref[:, D_NOPE:] = (_rope(q[:, D_NOPE:], cp[...], sa[...], sb[...]) * Q_SCALE).astype(BF16)

    tab = pl.BlockSpec((tm, LANES), lambda i, h: (i, 0))
    return _call(body, name="q_proj_rope", grid=(T // tm, N_HEADS),
                 in_specs=[pl.BlockSpec((tm, LORA), lambda i, h: (i, 0)),
                           pl.BlockSpec((None, LORA, D_HEAD_PAD), lambda i, h: (h, 0, 0)), tab, tab, tab],
                 out_specs=pl.BlockSpec((None, tm, D_HEAD_PAD), lambda i, h: (h, i, 0)),
                 out_shape=_sds((N_HEADS, T, D_HEAD_PAD), BF16))(cqn, w_uq_p, *tabs)


def _kv_proj(ckvn, w_ukv, kr):
    T = ckvn.shape[0]
    tm = _tile(T, 512)

    def body(c_ref, w_ref, kr_ref, k_o, v_o):
        kv = jnp.dot(c_ref[...], w_ref[...], preferred_element_type=F32)
        k_o[:, :D_NOPE] = kv[:, :D_NOPE].astype(BF16)
        k_o[:, D_NOPE:] = kr_ref[...]
        v_o[:, :D_V] = kv[:, D_NOPE:].astype(BF16)
        v_o[:, D_V:] = jnp.ones((tm, D_V), BF16)

    return _call(body, name="kv_proj", grid=(T // tm, N_HEADS),
                 in_specs=[pl.BlockSpec((tm, LORA), lambda i, h: (i, 0)),
                           pl.BlockSpec((None, LORA, D_NOPE + D_V), lambda i, h: (h, 0, 0)),
                           pl.BlockSpec((tm, LANES), lambda i, h: (i, 0))],
                 out_specs=[pl.BlockSpec((None, tm, D_HEAD_PAD), lambda i, h: (h, i, 0)),
                            pl.BlockSpec((None, tm, 2 * D_V), lambda i, h: (h, i, 0))],
                 out_shape=[_sds((N_HEADS, T, D_HEAD_PAD), BF16), _sds((N_HEADS, T, 2 * D_V), BF16)])(ckvn, w_ukv, kr)


def _flash_fwd(q, k, v1):
    _, T, _ = q.shape
    tq, tk = _tile(T, FLASH_TQ), _tile(T, FLASH_TK)
    nkv, reps = T // tk, tk // LANES

    def body(q_ref, k_ref, v_ref, o_ref, lse_ref, m_sc, acc_sc):
        m_sc[...] = jnp.full_like(m_sc, -jnp.inf)
        acc_sc[...] = jnp.zeros_like(acc_sc)
        qv = q_ref[...]

        def rows(j):
            return pl.ds(pl.multiple_of(j * tk, tk), tk)

        def scores(j):
            return lax.dot_general(qv, k_ref[rows(j), :], NT, preferred_element_type=F32)

        def update(s, j):
            m_prev = m_sc[...]
            m_new = jnp.maximum(m_prev, jnp.max(s, axis=1, keepdims=True))
            a = jnp.exp2(m_prev - m_new)
            p = jnp.exp2(s - jnp.tile(m_new, (1, reps)))
            pv = jnp.dot(p.astype(BF16), v_ref[rows(j), :], preferred_element_type=F32)
            acc_sc[...] = jnp.tile(a, (1, 2)) * acc_sc[...] + pv
            m_sc[...] = m_new

        if nkv % 2:
            def step(j, carry):
                update(scores(j), j)
                return carry

            lax.fori_loop(0, nkv, step, 0)
        else:
            def pair(jj, s_even):
                s_odd = scores(2 * jj + 1)
                update(s_even, 2 * jj)
                s_next = scores(2 * jj + 2)
                update(s_odd, 2 * jj + 1)
                return s_next

            s_even = lax.fori_loop(0, nkv // 2 - 1, pair, scores(0))
            s_odd = scores(nkv - 1)
            update(s_even, nkv - 2)
            update(s_odd, nkv - 1)
        acc = acc_sc[...]
        l = acc[:, D_V:]
        o_ref[...] = (acc[:, :D_V] / l).astype(BF16)
        lse_ref[...] = m_sc[...] + jnp.log(l) * LOG2_E

    return _call(body, name="flash_fwd", grid=(N_HEADS, T // tq),
                 in_specs=[pl.BlockSpec((None, tq, D_HEAD_PAD), lambda h, i: (h, i, 0)),
                           pl.BlockSpec((None, T, D_HEAD_PAD), lambda h, i: (h, 0, 0)),
                           pl.BlockSpec((None, T, 2 * D_V), lambda h, i: (h, 0, 0))],
                 out_specs=[pl.BlockSpec((tq, D_V), lambda h, i: (i, h)),
                            pl.BlockSpec((None, tq, LANES), lambda h, i: (h, i, 0))],
                 out_shape=[_sds((T, N_HEADS * D_V), BF16), _sds((N_HEADS, T, LANES), F32)],
                 scratch=[pltpu.VMEM((tq, LANES), F32), pltpu.VMEM((tq, 2 * D_V), F32)])(q, k, v1)


def _halo_specs(tm, cb, n_t):
    r = tm // HALO
    return [pl.BlockSpec((HALO, cb), lambda jc, i: (jnp.maximum(i * r - 1, 0), jc)),
            pl.BlockSpec((tm, cb), lambda jc, i: (i, jc)),
            pl.BlockSpec((HALO, cb), lambda jc, i: (jnp.minimum((i + 1) * r, n_t * r - 1), jc))]


def _fill_ext(ext, prev_ref, cur_ref, next_ref, i, n_t, tm):
    ext[0:HALO, :] = jnp.where(i > 0, prev_ref[...], 0.0)
    ext[HALO:HALO + tm, :] = cur_ref[...]
    ext[HALO + tm:, :] = jnp.where(i < n_t - 1, next_ref[...], 0.0)


FLASH_TQ = 512
FLASH_TK = 512
EPILOGUE_ROWS = 128
CONV_ROWS = 64


def _conv_fwd(u, w_pad, bias):
    T, C = u.shape
    tm, cb = _tile(T, 256), _tile(C, 256)
    n_t = T // tm
    rb = min(CONV_ROWS, tm)

    def body(up, uc, un, w_ref, b_ref, c_o, ext):
        i = pl.program_id(1)
        _fill_ext(ext, up, uc, un, i, n_t, tm)
        for r0 in range(0, tm, rb):
            acc = jnp.zeros((rb, cb), F32) + b_ref[...]
            for k in range(CONV_W):
                acc = acc + w_ref[k:k + 1, :] * ext[r0 + k + 1:r0 + k + 1 + rb, :]
            c_o[r0:r0 + rb, :] = acc

    return _call(body, name="conv_fwd", grid=(C // cb, n_t),
                 in_specs=_halo_specs(tm, cb, n_t) + [pl.BlockSpec((CONV_W_PAD, cb), lambda jc, i: (0, jc)),
                                                      pl.BlockSpec((1, cb), lambda jc, i: (0, jc))],
                 out_specs=pl.BlockSpec((tm, cb), lambda jc, i: (i, jc)),
                 out_shape=_sds((T, C), F32),
                 scratch=[pltpu.VMEM((tm + 2 * HALO, cb), F32)])(u, u, u, w_pad, bias)


def _conv_post(c, g, b):
    T, C = c.shape
    tm = _tile(T, 512)

    def body(c_ref, g_ref, b_ref, o_ref):
        xhat, _ = _ln_stats(c_ref[...])
        y = xhat * g_ref[...] + b_ref[...]
        o_ref[...] = (y * _sigmoid(y)).astype(BF16)

    blk = pl.BlockSpec((tm, C), lambda i: (i, 0))
    vec = pl.BlockSpec((1, C), lambda i: (0, 0))
    return _call(body, name="conv_ln_silu", grid=(T // tm,), in_specs=[blk, vec, vec], out_specs=blk,
                 out_shape=_sds((T, C), BF16))(c, g, b)


def _conv_post_bwd(dcat, c, g, b):
    T, C = c.shape
    tm = _tile(T, 512)

    def body(d_ref, c_ref, g_ref, b_ref, dc_o, dg_o, db_o):
        first = pl.program_id(0) == 0
        xhat, rstd = _ln_stats(c_ref[...])
        y = xhat * g_ref[...] + b_ref[...]
        sg = _sigmoid(y)
        dy = d_ref[...] * (sg * (1.0 + y * (1.0 - sg)))
        _acc_out(dg_o, _colsum(dy * xhat), first)
        _acc_out(db_o, _colsum(dy), first)
        dc_o[...] = _ln_bwd(dy, xhat, rstd, g_ref[...])

    blk = pl.BlockSpec((tm, C), lambda i: (i, 0))
    vec = pl.BlockSpec((1, C), lambda i: (0, 0))
    return _call(body, name="conv_ln_silu_bwd", grid=(T // tm,),
                 in_specs=[pl.BlockSpec((tm, C), lambda i: (i, 1)), blk, vec, vec],
                 out_specs=[blk, vec, vec],
                 out_shape=[_sds((T, C), F32), _sds((1, C), F32), _sds((1, C), F32)])(dcat, c, g, b)


def _conv_bwd(dc, u, w_pad):
    T, C = u.shape
    tm, cb = _tile(T, 256), _tile(C, 256)
    n_t = T // tm
    rb = min(CONV_ROWS, tm)

    def body(dp, dcur, dn, up, uc, un, w_ref, du_o, dw_o, db_o, dext, uext, dw_sc):
        i = pl.program_id(1)
        _fill_ext(dext, dp, dcur, dn, i, n_t, tm)
        _fill_ext(uext, up, uc, un, i, n_t, tm)

        @pl.when(i == 0)
        def _():
            dw_sc[...] = jnp.zeros_like(dw_sc)

        for r0 in range(0, tm, rb):
            acc = jnp.zeros((rb, cb), F32)
            d_here = dcur[r0:r0 + rb, :]
            for k in range(CONV_W):
                acc = acc + w_ref[k:k + 1, :] * dext[r0 + 2 * HALO - 1 - k:r0 + 2 * HALO - 1 - k + rb, :]
                prod = d_here * uext[r0 + k + 1:r0 + k + 1 + rb, :]
                dw_sc[k] += jnp.sum(prod.reshape(rb // 8, 8, cb), axis=0)
            du_o[r0:r0 + rb, :] = acc
        dw_sc[CONV_W] += jnp.sum(dcur[...].reshape(tm // 8, 8, cb), axis=0)

        @pl.when(i == n_t - 1)
        def _():
            red = jnp.sum(dw_sc[...], axis=1)
            row = lax.broadcasted_iota(jnp.int32, red.shape, 0)
            dw_o[...] = jnp.where(row < CONV_W, red, 0.0)
            db_o[...] = jnp.sum(jnp.where(row == CONV_W, red, 0.0), axis=0, keepdims=True)

    return _call(body, name="conv_bwd", grid=(C // cb, n_t),
                 in_specs=_halo_specs(tm, cb, n_t) + _halo_specs(tm, cb, n_t)
                 + [pl.BlockSpec((CONV_W_PAD, cb), lambda jc, i: (0, jc))],
                 out_specs=[pl.BlockSpec((tm, cb), lambda jc, i: (i, jc)),
                            pl.BlockSpec((CONV_W_PAD, cb), lambda jc, i: (0, jc)),
                            pl.BlockSpec((1, cb), lambda jc, i: (0, jc))],
                 out_shape=[_sds((T, C), F32), _sds((CONV_W_PAD, C), F32), _sds((1, C), F32)],
                 scratch=[pltpu.VMEM((tm + 2 * HALO, cb), F32), pltpu.VMEM((tm + 2 * HALO, cb), F32),
                          pltpu.VMEM((CONV_W_PAD, 8, cb), F32)])(dc, dc, dc, u, u, u, w_pad)


def _glu_bwd(du, h, C):
    T = du.shape[0]
    tm = _tile(T, 512)

    def body(du_ref, a_ref, gt_ref, o_ref):
        sg = _sigmoid(gt_ref[...])
        du_v = du_ref[...]
        o_ref[:, :C] = (du_v * sg).astype(BF16)
        o_ref[:, C:] = (du_v * a_ref[...] * sg * (1.0 - sg)).astype(BF16)

    return _call(body, name="glu_bwd", grid=(T // tm,),
                 in_specs=[pl.BlockSpec((tm, C), lambda i: (i, 0)), pl.BlockSpec((tm, C), lambda i: (i, 0)),
                           pl.BlockSpec((tm, C), lambda i: (i, 1))],
                 out_specs=pl.BlockSpec((tm, 2 * C), lambda i: (i, 0)),
                 out_shape=_sds((T, 2 * C), BF16))(du, h, h)


def _attn_delta(dcat, attn):
    T = attn.shape[0]
    tm = _tile(T, 512)

    def body(d_ref, o_ref, dl_o, dob_o):
        d = d_ref[...]
        dl = jnp.sum(d * o_ref[...].astype(F32), axis=1, keepdims=True)
        dl_o[...] = jnp.broadcast_to(dl, (tm, LANES))
        dob_o[...] = d.astype(BF16)

    blk = pl.BlockSpec((tm, D_V), lambda i, h: (i, h))
    hblk = pl.BlockSpec((None, tm, D_V), lambda i, h: (h, i, 0))
    return _call(body, name="attn_delta", grid=(T // tm, N_HEADS), in_specs=[blk, blk], out_specs=[hblk, hblk],
                 out_shape=[_sds((N_HEADS, T, LANES), F32), _sds((N_HEADS, T, D_V), BF16)])(dcat, attn)


def _flash_bwd(q, k, v1, do, lse, delta):
    _, T, _ = q.shape
    tq, tk = _tile(T, FLASH_TQ), _tile(T, FLASH_TK)
    nkv, reps = T // tk, tk // LANES

    def body(q_ref, do_ref, lse_ref, dl_ref, k_ref, v_ref, dq_o, dk_o, dv_o, dq_sc):
        @pl.when(pl.program_id(1) == 0)
        def _():
            dk_o[...] = jnp.zeros_like(dk_o)
            dv_o[...] = jnp.zeros_like(dv_o)

        qv, dov = q_ref[...], do_ref[...]
        lse_t = jnp.tile(lse_ref[...], (1, reps))
        dl_t = jnp.tile(dl_ref[...], (1, reps))
        dq_sc[...] = jnp.zeros_like(dq_sc)

        def rows(j):
            return pl.ds(pl.multiple_of(j * tk, tk), tk)

        def scores(j):
            s = lax.dot_general(qv, k_ref[rows(j), :], NT, preferred_element_type=F32)
            dp = lax.dot_general(dov, v_ref[rows(j), :D_V], NT, preferred_element_type=F32)
            return s, dp

        def update(s_dp, j):
            s, dp = s_dp
            p = jnp.exp2(s - lse_t)
            ds = (p * (dp - dl_t)).astype(BF16)
            dv_o[rows(j), :] += lax.dot_general(p.astype(BF16), dov, TN, preferred_element_type=F32)
            dk_o[rows(j), :] += lax.dot_general(ds, qv, TN, preferred_element_type=F32)
            dq_sc[...] += jnp.dot(ds, k_ref[rows(j), :], preferred_element_type=F32)

        def step(j, carry):
            update(scores(j), j)
            return carry

        lax.fori_loop(0, nkv, step, 0)
        dq_o[...] = dq_sc[...]

    def tile(w):
        return pl.BlockSpec((None, tq, w), lambda h, i: (h, i, 0))

    def whole(w):
        return pl.BlockSpec((None, T, w), lambda h, i: (h, 0, 0))

    return _call(body, name="flash_bwd", grid=(N_HEADS, T // tq),
                 in_specs=[tile(D_HEAD_PAD), tile(D_V), tile(LANES), tile(LANES), whole(D_HEAD_PAD), whole(2 * D_V)],
                 out_specs=[tile(D_HEAD_PAD), whole(D_HEAD_PAD), whole(D_V)],
                 out_shape=[_sds((N_HEADS, T, D_HEAD_PAD), F32), _sds((N_HEADS, T, D_HEAD_PAD), F32),
                            _sds((N_HEADS, T, D_V), F32)],
                 scratch=[pltpu.VMEM((tq, D_HEAD_PAD), F32)])(q, do, lse, delta, k, v1)


def _dq_post(dq, tabs):
    _, T, _ = dq.shape
    tm = _tile(T, 512)

    def body(d_ref, cp, sa, sb, o_ref):
        d = d_ref[...] * SCALE
        o_ref[:, :D_NOPE] = d[:, :D_NOPE].astype(BF16)
        o_ref[:, D_NOPE:] = _unrope(d[:, D_NOPE:], cp[...], sa[...], sb[...]).astype(BF16)

    blk = pl.BlockSpec((None, tm, D_HEAD_PAD), lambda i, h: (h, i, 0))
    tab = pl.BlockSpec((tm, LANES), lambda i, h: (i, 0))
    return _call(body, name="dq_unrope", grid=(T // tm, N_HEADS), in_specs=[blk, tab, tab, tab], out_specs=blk,
                 out_shape=_sds(dq.shape, BF16))(dq, *tabs)


def _dk_post(dk, dv, tabs):
    _, T, _ = dk.shape
    tm = _tile(T, 512)

    def body(dk_ref, dv_ref, cp, sa, sb, dkv_o, dkr_o, sc):
        h = pl.program_id(1)
        d = dk_ref[...] * LN_2
        dkv_o[:, :D_NOPE] = d[:, :D_NOPE].astype(BF16)
        dkv_o[:, D_NOPE:] = dv_ref[...].astype(BF16)

        @pl.when(h == 0)
        def _():
            sc[...] = d[:, D_NOPE:]

        @pl.when(h > 0)
        def _():
            sc[...] += d[:, D_NOPE:]

        @pl.when(h == N_HEADS - 1)
        def _():
            dkr_o[...] = _unrope(sc[...], cp[...], sa[...], sb[...]).astype(BF16)

    tab = pl.BlockSpec((tm, LANES), lambda i, h: (i, 0))
    return _call(body, name="dk_unrope", grid=(T // tm, N_HEADS),
                 in_specs=[pl.BlockSpec((None, tm, D_HEAD_PAD), lambda i, h: (h, i, 0)),
                           pl.BlockSpec((None, tm, D_V), lambda i, h: (h, i, 0)), tab, tab, tab],
                 out_specs=[pl.BlockSpec((None, tm, D_HEAD_PAD), lambda i, h: (h, i, 0)), tab],
                 out_shape=[_sds((N_HEADS, T, D_HEAD_PAD), BF16), _sds((T, LANES), BF16)],
                 scratch=[pltpu.VMEM((tm, LANES), F32)])(dk, dv, *tabs)


def _latent_bwd(name, dproj, w_heads, h, col_blk, g):
    _, T, _ = dproj.shape
    tm = _tile(T, 512)

    def ep(acc, ex, out, first):
        dx, dg = _rms_bwd(acc, ex[0][...], ex[1][...])
        out[0][...] = dx.astype(BF16)
        _acc_out(out[1], dg, first)

    return _matmul(name, dproj, w_heads, dims=NT, grid=(T // tm, 1, N_HEADS),
                   a_spec=pl.BlockSpec((None, tm, D_HEAD_PAD), lambda i, j, k: (k, i, 0)),
                   b_spec=pl.BlockSpec((None, LORA, D_HEAD_PAD), lambda i, j, k: (k, 0, 0)),
                   acc_shape=(tm, LORA),
                   extras=[(h, pl.BlockSpec((tm, LORA), lambda i, j, k: (i, col_blk))),
                           (g, pl.BlockSpec((1, LORA), lambda i, j, k: (0, 0)))],
                   outs=[(_sds((T, LORA), BF16), pl.BlockSpec((tm, LORA), lambda i, j, k: (i, 0))),
                         (_sds((1, LORA), F32), pl.BlockSpec((1, LORA), lambda i, j, k: (0, 0)))],
                   epilogue=ep)


def _head_weight_grad(name, latent, dproj):
    _, T, _ = dproj.shape
    tk = _tile(T, 512)
    return _matmul(name, latent, dproj, dims=TN, grid=(N_HEADS, 1, T // tk),
                   a_spec=pl.BlockSpec((tk, LORA), lambda i, j, k: (k, 0)),
                   b_spec=pl.BlockSpec((None, tk, D_HEAD_PAD), lambda i, j, k: (i, k, 0)),
                   acc_shape=(LORA, D_HEAD_PAD),
                   outs=[(_sds((N_HEADS, LORA, D_HEAD_PAD), F32),
                          pl.BlockSpec((None, LORA, D_HEAD_PAD), lambda i, j, k: (i, 0, 0)))],
                   epilogue=_store())[0]


def _weight_grad(name, a, b, tm_pref=1024, tn_pref=1024, stacked_cols=None):
    T, M = a.shape
    N = b.shape[1]
    tk = _tile(T, 512)
    tm = _tile(M, tm_pref)
    if stacked_cols is None:
        tn = _tile(N, tn_pref)
        out = (_sds((M, N), F32), pl.BlockSpec((tm, tn), lambda i, j, k: (i, j)))
    else:
        tn = _tile(stacked_cols, tn_pref)
        per = stacked_cols // tn
        out = (_sds((N // stacked_cols, M, stacked_cols), F32),
               pl.BlockSpec((None, tm, tn), lambda i, j, k: (j // per, i, j % per)))
    return _matmul(name, a, b, dims=TN, grid=(M // tm, N // tn, T // tk),
                   a_spec=pl.BlockSpec((tk, tm), lambda i, j, k: (k, i)),
                   b_spec=pl.BlockSpec((tk, tn), lambda i, j, k: (k, j)),
                   acc_shape=(tm, tn), outs=[out], epilogue=_store())[0]


def _small_names():
    return ["ln_in_g", "ln_in_b", "g_cq", "g_ckv", "conv_b", "g_conv_ln", "b_conv_ln", "g_ln1", "b_ln1", "g_ln2", "b_ln2"]


def _pack(vecs):
    flat = jnp.concatenate([v.reshape(-1) for v in vecs])
    assert flat.shape[0] % (8 * LANES) == 0
    return flat.reshape(-1, LANES)


def _unpack(packed, like):
    flat, out, off = packed.reshape(-1), [], 0
    for v in like:
        out.append(flat[off:off + v.size].reshape(v.shape))
        off += v.size
    return out


def kernel(x, positions, ln_in_g, ln_in_b, w_in, g_cq, w_uq, g_ckv, w_uk, w_uv, conv_w, conv_b, g_conv_ln, b_conv_ln, w_out, g_ln1, b_ln1, w_ff1, w_ff2, g_ln2, b_ln2, loss_target, m_ln_in_g, m_ln_in_b, m_w_in, m_g_cq, m_w_uq, m_g_ckv, m_w_uk, m_w_uv, m_conv_w, m_conv_b, m_g_conv_ln, m_b_conv_ln, m_w_out, m_g_ln1, m_b_ln1, m_w_ff1, m_w_ff2, m_g_ln2, m_b_ln2, v_ln_in_g, v_ln_in_b, v_w_in, v_g_cq, v_w_uq, v_g_ckv, v_w_uk, v_w_uv, v_conv_w, v_conv_b, v_g_conv_ln, v_b_conv_ln, v_w_out, v_g_ln1, v_b_ln1, v_w_ff1, v_w_ff2, v_g_ln2, v_b_ln2):
    args = dict(locals())
    T, D = x.shape[1], x.shape[2]
    C = D - N_HEADS * D_V
    Fs = w_ff1.shape[2]
    F = N_DEV * Fs
    n_in = N_DEV * w_in.shape[2]
    n_in_p = 2 * C + 2 * LORA + LANES
    assert n_in == 2 * LORA + D_ROPE + 2 * C and w_uq.shape[2] == D_QK and conv_w.shape[2] * N_DEV == C

    xs, tgt = x[0], loss_target[0]
    row = lambda v_: v_.reshape(1, -1)

    gather = lambda k_: [False] * k_
    ag_in = _exchange_start("ag_in_start", [w_in[0].astype(BF16)], gather(1))
    ag_heads = _exchange_start("ag_heads_start", [w_uq[0].astype(BF16), w_uk[0].astype(BF16), w_uv[0].astype(BF16),
                                                  conv_w[0]], gather(4))
    ag_ff = _exchange_start("ag_ff_start", [w_out[0].astype(BF16), w_ff1[0].astype(BF16), w_ff2[0].astype(BF16)],
                            gather(3))
    started = ag_in[4] + ag_heads[4] + ag_ff[4]

    half = D_ROPE // 2
    inv_freq = ROPE_BASE ** (-jnp.arange(half, dtype=F32) * (2.0 / D_ROPE))
    inv_freq = jnp.tile(inv_freq, LANES // half).reshape(1, LANES)
    tabs = _rope_tables(positions.reshape(T, 1), inv_freq)

    x0, x0b = _ln_in(xs, row(ln_in_g) + started, row(ln_in_b))

    (g_w_in,) = _exchange_wait("ag_in_wait", ag_in, gather(1), after=x0b)
    w_in_f = jnp.transpose(g_w_in, (1, 0, 2)).reshape(D, n_in)
    s_cq, s_ckv, s_kr, s_a, s_g = 0, LORA, 2 * LORA, 2 * LORA + D_ROPE, 2 * LORA + D_ROPE + C
    w_in_p = jnp.concatenate([w_in_f[:, s_a:s_g], w_in_f[:, s_g:], w_in_f[:, s_cq:s_ckv], w_in_f[:, s_ckv:s_kr],
                              w_in_f[:, s_kr:s_a], jnp.zeros((D, LANES - D_ROPE), BF16)], axis=1)

    tm, tn = _tile(T, 512), _tile(n_in_p, 640)
    h = _matmul("h_proj", x0b, w_in_p, dims=NN, grid=(T // tm, n_in_p // tn, 1),
                a_spec=pl.BlockSpec((tm, D), lambda i, j, k: (i, 0)),
                b_spec=pl.BlockSpec((D, tn), lambda i, j, k: (0, j)), acc_shape=(tm, tn),
                outs=[(_sds((T, n_in_p), F32), pl.BlockSpec((tm, tn), lambda i, j, k: (i, j)))],
                epilogue=_store())[0]

    u, cqn, ckvn, kr = _mid(h, g_cq, g_ckv, tabs, C)
    g_w_uq, g_w_uk, g_w_uv, g_conv_w = _exchange_wait("ag_heads_wait", ag_heads, gather(4), after=cqn)
    w_uq_p = jnp.pad(g_w_uq, ((0, 0), (0, 0), (0, D_HEAD_PAD - D_QK)))
    w_ukv = jnp.concatenate([g_w_uk, g_w_uv], axis=2)
    conv_w_f = jnp.pad(jnp.transpose(g_conv_w, (1, 0, 2)).reshape(CONV_W, C), ((0, CONV_W_PAD - CONV_W), (0, 0)))
    q = _q_proj(cqn, w_uq_p, tabs)
    kf, vf = _kv_proj(ckvn, w_ukv, kr)
    attn, lse = _flash_fwd(q, kf, vf)
    conv_c = _conv_fwd(u, conv_w_f, conv_b)
    conv_out = _conv_post(conv_c, g_conv_ln, b_conv_ln)
    g_w_out, g_w_ff1, g_w_ff2 = _exchange_wait("ag_ff_wait", ag_ff, gather(3), after=conv_out)
    w_out_f = g_w_out.reshape(D, D)
    w_ff2_f = g_w_ff2.reshape(F, D)
    cat = jnp.concatenate([attn, conv_out], axis=1)

    def ep_ln1(acc, ex, out, first):
        z1 = ALPHA * ex[0][...] + acc
        xhat, _ = _ln_stats(z1)
        x1 = xhat * ex[1][...] + ex[2][...]
        out[0][...] = z1
        out[1][...] = x1
        out[2][...] = x1.astype(BF16)

    tm = _tile(T, 256)
    rowblk = pl.BlockSpec((tm, D), lambda i, j, k: (i, 0))
    vecD = pl.BlockSpec((1, D), lambda i, j, k: (0, 0))
    z1, x1, x1b = _matmul("mix_ln1", cat, w_out_f, dims=NN, grid=(T // tm, 1, 1), a_spec=rowblk,
                          b_spec=pl.BlockSpec((D, D), lambda i, j, k: (0, 0)), acc_shape=(tm, D),
                          extras=[(x0, rowblk), (g_ln1, vecD), (b_ln1, vecD)],
                          outs=[(_sds((T, D), F32), rowblk), (_sds((T, D), F32), rowblk), (_sds((T, D), BF16), rowblk)],
                          epilogue=ep_ln1, ep_rows=EPILOGUE_ROWS)

    def ep_ff1(acc, ex, out, first):
        r = jnp.maximum(acc, 0.0)
        out[0][...] = (r * r).astype(BF16)
        out[1][...] = r.astype(BF16)

    tm, tn = _tile(T, 512), _tile(Fs, 1024)
    per = Fs // tn
    fblk = pl.BlockSpec((tm, tn), lambda i, j, k: (i, j))
    f_act, r_act = _matmul("ff1_relu2", x1b, g_w_ff1, dims=NN, grid=(T // tm, F // tn, 1),
                           a_spec=pl.BlockSpec((tm, D), lambda i, j, k: (i, 0)),
                           b_spec=pl.BlockSpec((None, D, tn), lambda i, j, k: (j // per, 0, j % per)),
                           acc_shape=(tm, tn), outs=[(_sds((T, F), BF16), fblk), (_sds((T, F), BF16), fblk)],
                           epilogue=ep_ff1)

    def ep_ln2(acc, ex, out, first):
        g2 = ex[2][...]
        z2 = ALPHA * ex[0][...] + acc
        xhat, rstd = _ln_stats(z2)
        err = xhat * g2 + ex[3][...] - ex[1][...]
        part = 0.5 * jnp.sum(jnp.mean(err * err, axis=-1, keepdims=True))
        _acc_out(out[2], jnp.zeros((8, LANES), F32) + part, first)
        dy = err * (1.0 / D)
        _acc_out(out[3], _colsum(dy * xhat), first)
        _acc_out(out[4], _colsum(dy), first)
        dz2 = _ln_bwd(dy, xhat, rstd, g2)
        out[0][...] = dz2
        out[1][...] = dz2.astype(BF16)

    tm, tk = _tile(T, 512), _tile(F, 512)
    rowblk = pl.BlockSpec((tm, D), lambda i, j, k: (i, 0))
    dz2, dz2b, loss_blk, dg_ln2, db_ln2 = _matmul(
        "ff2_ln2_loss", f_act, w_ff2_f, dims=NN, grid=(T // tm, 1, F // tk),
        a_spec=pl.BlockSpec((tm, tk), lambda i, j, k: (i, k)), b_spec=pl.BlockSpec((tk, D), lambda i, j, k: (k, 0)),
        acc_shape=(tm, D), extras=[(x1, rowblk), (tgt, rowblk), (g_ln2, vecD), (b_ln2, vecD)],
        outs=[(_sds((T, D), F32), rowblk), (_sds((T, D), BF16), rowblk),
              (_sds((8, LANES), F32), pl.BlockSpec((8, LANES), lambda i, j, k: (0, 0))),
              (_sds((1, D), F32), vecD), (_sds((1, D), F32), vecD)],
        epilogue=ep_ln2, ep_rows=EPILOGUE_ROWS)
    loss = lax.psum(loss_blk[0, 0], ("x", "y", "c"))

    def ep_dpre(acc, ex, out, first):
        out[0][...] = (acc * (2.0 * ex[0][...].astype(F32))).astype(BF16)

    tm, tn = _tile(T, 512), _tile(F, 1024)
    fblk = pl.BlockSpec((tm, tn), lambda i, j, k: (i, j))
    dpre = _matmul("ff2_dgrad", dz2b, w_ff2_f, dims=NT, grid=(T // tm, F // tn, 1),
                   a_spec=pl.BlockSpec((tm, D), lambda i, j, k: (i, 0)), b_spec=pl.BlockSpec((tn, D), lambda i, j, k: (j, 0)),
                   acc_shape=(tm, tn), extras=[(r_act, fblk)], outs=[(_sds((T, F), BF16), fblk)], epilogue=ep_dpre)[0]

    dw_ff2 = _weight_grad("ff2_wgrad", f_act, dz2b).reshape(N_DEV, Fs, D)
    dw_ff1 = _weight_grad("ff1_wgrad", x1b, dpre, stacked_cols=Fs)
    scatter = lambda k_: [True] * k_
    rs_ff = _exchange_start("rs_ff_start", [dw_ff2, dw_ff1], scatter(2))

    def ep_ln1_bwd(acc, ex, out, first):
        dx1 = ALPHA * ex[0][...] + acc
        xhat, rstd = _ln_stats(ex[1][...])
        _acc_out(out[2], _colsum(dx1 * xhat), first)
        _acc_out(out[3], _colsum(dx1), first)
        dz1 = _ln_bwd(dx1, xhat, rstd, ex[2][...])
        out[0][...] = dz1
        out[1][...] = dz1.astype(BF16)

    tm, tk = _tile(T, 512), _tile(Fs, 512)
    per = Fs // tk
    rowblk = pl.BlockSpec((tm, D), lambda i, j, k: (i, 0))
    dz1, dz1b, dg_ln1, db_ln1 = _matmul(
        "ff1_dgrad_ln1_bwd", dpre, g_w_ff1, dims=NT, grid=(T // tm, 1, F // tk),
        a_spec=pl.BlockSpec((tm, tk), lambda i, j, k: (i, k)),
        b_spec=pl.BlockSpec((None, D, tk), lambda i, j, k: (k // per, 0, k % per)),
        acc_shape=(tm, D), extras=[(dz2, rowblk), (z1, rowblk), (g_ln1 + rs_ff[4], vecD)],
        outs=[(_sds((T, D), F32), rowblk), (_sds((T, D), BF16), rowblk), (_sds((1, D), F32), vecD), (_sds((1, D), F32), vecD)],
        epilogue=ep_ln1_bwd, ep_rows=EPILOGUE_ROWS)

    dw_out = _weight_grad("out_wgrad", cat, dz1b).reshape(N_DEV, D // N_DEV, D)
    rs_out = _exchange_start("rs_out_start", [dw_out], scatter(1))
    tm, tn = _tile(T, 512), _tile(D, 1024)
    dcat = _matmul("out_dgrad", dz1b, w_out_f, dims=NT, grid=(T // tm, D // tn, 1),
                   a_spec=pl.BlockSpec((tm, D), lambda i, j, k: (i, 0)), b_spec=pl.BlockSpec((tn, D), lambda i, j, k: (j, 0)),
                   acc_shape=(tm, tn), outs=[(_sds((T, D), F32), pl.BlockSpec((tm, tn), lambda i, j, k: (i, j)))],
                   epilogue=_store())[0]

    dc, dg_conv_ln, db_conv_ln = _conv_post_bwd(dcat, conv_c, g_conv_ln + rs_out[4], b_conv_ln)
    du, dconv_w_p, dconv_b = _conv_bwd(dc, u, conv_w_f)
    dconv_in = _glu_bwd(du, h, C)

    delta, do_heads = _attn_delta(dcat, attn)
    dq, dk, dv = _flash_bwd(q, kf, vf, do_heads, lse, delta)
    dq_raw = _dq_post(dq, tabs)
    dkv, dkr = _dk_post(dk, dv, tabs)
    cq_blk = (2 * C) // LORA
    dcq, dg_cq = _latent_bwd("q_dgrad_rms_bwd", dq_raw, w_uq_p, h, cq_blk, g_cq)
    dckv, dg_ckv = _latent_bwd("kv_dgrad_rms_bwd", dkv, w_ukv, h, cq_blk + 1, g_ckv)
    dw_uq = _head_weight_grad("uq_wgrad", cqn, dq_raw)[:, :, :D_QK]
    dw_ukv = _head_weight_grad("ukv_wgrad", ckvn, dkv)
    dw_uk, dw_uv = dw_ukv[:, :, :D_NOPE], dw_ukv[:, :, D_NOPE:]

    dh = jnp.concatenate([dconv_in, dcq, dckv, dkr], axis=1)
    dw_in_p = _weight_grad("in_wgrad", x0b, dh, tn_pref=640)
    dw_in_f = jnp.concatenate([dw_in_p[:, 2 * C:2 * C + 2 * LORA + D_ROPE], dw_in_p[:, :2 * C]], axis=1)
    dw_in = jnp.transpose(dw_in_f.reshape(D, N_DEV, n_in // N_DEV), (1, 0, 2))
    dconv_w = jnp.transpose(dconv_w_p[:CONV_W].reshape(CONV_W, N_DEV, C // N_DEV), (1, 0, 2))
    rs_in = _exchange_start("rs_in_start", [dw_in, dw_uq, dw_uk, dw_uv, dconv_w], scatter(5))

    def ep_ln_in_bwd(acc, ex, out, first):
        dx0 = ALPHA * ex[0][...] + acc
        xhat, rstd = _ln_stats(ex[1][...])
        _acc_out(out[1], _colsum(dx0 * xhat), first)
        _acc_out(out[2], _colsum(dx0), first)
        out[0][...] = _ln_bwd(dx0, xhat, rstd, ex[2][...])

    tm, tk = _tile(T, 512), _tile(n_in_p, 640)
    rowblk = pl.BlockSpec((tm, D), lambda i, j, k: (i, 0))
    grad_x, dg_ln_in, db_ln_in = _matmul(
        "in_dgrad_ln_in_bwd", dh, w_in_p, dims=NT, grid=(T // tm, 1, n_in_p // tk),
        a_spec=pl.BlockSpec((tm, tk), lambda i, j, k: (i, k)), b_spec=pl.BlockSpec((D, tk), lambda i, j, k: (0, k)),
        acc_shape=(tm, D), extras=[(dz1, rowblk), (xs, rowblk), (row(ln_in_g) + rs_in[4], vecD)],
        outs=[(_sds((T, D), F32), rowblk), (_sds((1, D), F32), vecD), (_sds((1, D), F32), vecD)],
        epilogue=ep_ln_in_bwd, ep_rows=EPILOGUE_ROWS)

    small = dict(ln_in_g=dg_ln_in, ln_in_b=db_ln_in, g_cq=dg_cq, g_ckv=dg_ckv, conv_b=dconv_b, g_conv_ln=dg_conv_ln,
                 b_conv_ln=db_conv_ln, g_ln1=dg_ln1, b_ln1=db_ln1, g_ln2=dg_ln2, b_ln2=db_ln2)
    names = _small_names()
    rs_small = _exchange_start("rs_small_start", [_pack([small[n] for n in names])], gather(1))
    parts = {}
    parts["w_ff2"], parts["w_ff1"] = _exchange_wait("rs_ff_wait", rs_ff, scatter(2), after=grad_x)
    (parts["w_out"],) = _exchange_wait("rs_out_wait", rs_out, scatter(1), after=grad_x)
    parts["w_in"], parts["w_uq"], parts["w_uk"], parts["w_uv"], parts["conv_w"] = _exchange_wait(
        "rs_in_wait", rs_in, scatter(5), after=grad_x)
    (small_parts,) = _exchange_wait("rs_small_wait", rs_small, gather(1), after=grad_x)

    res = {}
    for n, p in parts.items():
        res[n] = [o.reshape(args[n].shape) for o in _adamw("adamw_" + n, p, args[n][0], args["m_" + n][0], args["v_" + n][0])]
    packed = _adamw("adamw_small", small_parts, _pack([args[n] for n in names]), _pack([args["m_" + n] for n in names]),
                    _pack([args["v_" + n] for n in names]))
    like = [args[n] for n in names]
    unpacked = [_unpack(p, like) for p in packed]
    for i, n in enumerate(names):
        res[n] = [unpacked[kind][i] for kind in range(4)]

    order = ["ln_in_g", "ln_in_b", "w_in", "g_cq", "w_uq", "g_ckv", "w_uk", "w_uv", "conv_w", "conv_b", "g_conv_ln",
             "b_conv_ln", "w_out", "g_ln1", "b_ln1", "w_ff1", "w_ff2", "g_ln2", "b_ln2"]
    outs = [loss, grad_x.reshape(x.shape)]
    for kind in range(4):
        outs += [res[n][kind] for n in order]
    return tuple(outs)
```

```python
import jax
import jax.numpy as jnp
from jax import lax
from jax.experimental import pallas as pl
from jax.experimental.pallas import tpu as pltpu

F32 = jnp.float32
BF16 = jnp.bfloat16

N_HEADS = 8
D_NOPE = 128
D_ROPE = 64
D_V = 128
D_QK = D_NOPE + D_ROPE
D_HEAD_PAD = 256
LORA = 512
CONV_W = 31
CONV_HALF = CONV_W // 2
CONV_W_PAD = 32
HALO = 16
LN_EPS = 1e-5
RMS_EPS = 1e-6
ALPHA = 2.0 ** 0.25
SCALE = float(D_QK) ** -0.5
LOG2_E = 1.4426950408889634
LN_2 = 0.6931471805599453
Q_SCALE = SCALE * LOG2_E
ROPE_BASE = 10000.0
ADAM_LR, ADAM_B1, ADAM_B2, ADAM_EPS, ADAM_WD, ADAM_STEP = 0.001, 0.9, 0.999, 1e-08, 0.01, 10

N_DEV = 8
LANES = 128
VMEM_LIMIT_V7X = 56 * 1024 * 1024

NN = (((1,), (0,)), ((), ()))
NT = (((1,), (1,)), ((), ()))
TN = (((0,), (0,)), ((), ()))


def _call(body, *, name, grid, in_specs, out_specs, out_shape, scratch=(), aliases=None):
    params = pltpu.CompilerParams(dimension_semantics=("arbitrary",) * len(grid),
                                  vmem_limit_bytes=VMEM_LIMIT_V7X)
    return pl.pallas_call(body, name=name, grid=grid, in_specs=in_specs, out_specs=out_specs,
                          out_shape=out_shape, scratch_shapes=scratch, compiler_params=params,
                          input_output_aliases=aliases or {})


def _tile(n, pref):
    if n <= pref:
        return n
    t = (pref // LANES) * LANES
    while t > LANES and n % t:
        t -= LANES
    assert n % t == 0, (n, pref)
    return t


def _sds(shape, dtype):
    return jax.ShapeDtypeStruct(shape, dtype)


def _ln_stats(z):
    mu = jnp.mean(z, axis=-1, keepdims=True)
    zc = z - mu
    var = jnp.mean(zc * zc, axis=-1, keepdims=True)
    rstd = lax.rsqrt(var + LN_EPS)
    return zc * rstd, rstd


def _ln_bwd(dy, xhat, rstd, g):
    gd = dy * g
    m1 = jnp.mean(gd, axis=-1, keepdims=True)
    m2 = jnp.mean(gd * xhat, axis=-1, keepdims=True)
    return rstd * (gd - m1 - xhat * m2)


def _rms(x, g):
    return x * lax.rsqrt(jnp.mean(x * x, axis=-1, keepdims=True) + RMS_EPS) * g


def _rms_bwd(dy, x, g):
    r = lax.rsqrt(jnp.mean(x * x, axis=-1, keepdims=True) + RMS_EPS)
    dxn = dy * g
    dx = r * dxn - x * (r * r * r) * jnp.mean(dxn * x, axis=-1, keepdims=True)
    dg = jnp.sum(dy * x * r, axis=0, keepdims=True)
    return dx, dg


def _sigmoid(x):
    return 1.0 / (1.0 + jnp.exp(-x))


def _rope(x, cos_p, sin_a, sin_b):
    return x * cos_p + pltpu.roll(x, 96, 1) * sin_a + pltpu.roll(x, 32, 1) * sin_b


def _unrope(d, cos_p, sin_a, sin_b):
    return d * cos_p - pltpu.roll(d, 96, 1) * sin_a - pltpu.roll(d, 32, 1) * sin_b


def _colsum(v):
    return jnp.sum(v, axis=0, keepdims=True)


def _acc_out(ref, val, first):
    if first is False:
        ref[...] += val
        return

    @pl.when(first)
    def _():
        ref[...] = val

    @pl.when(jnp.logical_not(first))
    def _():
        ref[...] += val


class _Rows:
    def __init__(self, ref, sl):
        self.ref, self.sl = ref, sl

    def __getitem__(self, idx):
        assert idx is Ellipsis
        return self.ref[self.sl, :]

    def __setitem__(self, idx, val):
        assert idx is Ellipsis
        self.ref[self.sl, :] = val


def _matmul(name, a, b, *, dims, grid, a_spec, b_spec, acc_shape, outs, epilogue, extras=(), ep_rows=None, into=None):
    nk = grid[2]
    ne, no = len(extras), len(outs)
    tm = acc_shape[0]
    n_in = 2 + ne + (0 if into is None else 1)

    def finish(acc_rows, ex, out):
        first = pl.program_id(0) == 0
        if ep_rows is None or ep_rows >= tm:
            epilogue(acc_rows(slice(None)), ex, out, first)
            return
        for r0 in range(0, tm, ep_rows):
            sl = slice(r0, r0 + ep_rows)
            view = lambda r: _Rows(r, sl) if r.shape[0] == tm else r
            epilogue(acc_rows(sl), [view(r) for r in ex], [view(r) for r in out], first if r0 == 0 else False)

    def body(*refs):
        a_ref, b_ref = refs[0], refs[1]
        ex = refs[2:2 + ne]
        out = refs[n_in:n_in + no]
        if nk == 1:
            part = lax.dot_general(a_ref[...], b_ref[...], dims, preferred_element_type=F32)
            finish(lambda sl: part[sl, :], ex, out)
        else:
            acc = refs[n_in + no]
            k = pl.program_id(2)

            @pl.when(k == 0)
            def _():
                acc[...] = jnp.zeros_like(acc)

            acc[...] += lax.dot_general(a_ref[...], b_ref[...], dims, preferred_element_type=F32)

            @pl.when(k == nk - 1)
            def _():
                finish(lambda sl: acc[sl, :], ex, out)

    scratch = [] if nk == 1 else [pltpu.VMEM(acc_shape, F32)]
    ins = [a, b] + [e for e, _ in extras]
    in_specs = [a_spec, b_spec] + [s for _, s in extras]
    aliases = {}
    if into is not None:
        ins.append(into)
        in_specs.append(pl.BlockSpec(memory_space=pl.ANY))
        aliases = {n_in - 1: 0}
    return _call(body, name=name, grid=grid, in_specs=in_specs, out_specs=[s for _, s in outs],
                 out_shape=[o for o, _ in outs], scratch=scratch, aliases=aliases)(*ins)


def _store(dtype=F32):
    def ep(acc, ex, out, first):
        out[0][...] = acc.astype(dtype)
    return ep


def _mesh_pos():
    return lax.axis_index("x"), lax.axis_index("y"), lax.axis_index("c")


def _flip(v, bit):
    return 1 - v if bit else v


_HBM = pl.BlockSpec(memory_space=pltpu.HBM)
_SEM = pl.BlockSpec(memory_space=pltpu.SEMAPHORE)
_EFFECT = pltpu.SideEffectType.DATAFLOW_SIDE_EFFECTING


def _my_slot():
    x, y, c = _mesh_pos()
    return 4 * x + 2 * y + c


def _exchange_copies(srcs, lands, send_sems, recv_sems, stacked, receives=True):
    x, y, c = _mesh_pos()
    me = 4 * x + 2 * y + c
    pairs = []
    for w in range(len(srcs)):
        for k in range(1, N_DEV):
            peer = (_flip(x, k & 4), _flip(y, k & 2), _flip(c, k & 1))
            peer_slot = 4 * peer[0] + 2 * peer[1] + peer[2]
            to_peer = srcs[w].at[peer_slot] if stacked[w] else srcs[w]
            mine = srcs[w].at[me] if stacked[w] else srcs[w]
            s = w * (N_DEV - 1) + k - 1
            sems = dict(send_sem=send_sems.at[s], recv_sem=recv_sems.at[s],
                        device_id=peer, device_id_type=pl.DeviceIdType.MESH)
            send = pltpu.make_async_remote_copy(src_ref=to_peer, dst_ref=lands[w].at[me], **sems)
            recv = pltpu.make_async_remote_copy(src_ref=mine, dst_ref=lands[w].at[peer_slot], **sems) if receives else None
            pairs.append((send, recv))
    return pairs


def _exchange_start(name, srcs, stacked):
    n = len(srcs)
    land_shapes = [s.shape if st else (N_DEV,) + s.shape for s, st in zip(srcs, stacked)]

    def body(*refs):
        src, land = refs[:n], refs[n:2 * n]
        send_sems, recv_sems = refs[2 * n], refs[2 * n + 1]
        token = refs[-1]
        for send, _ in _exchange_copies(src, land, send_sems, recv_sems, stacked, receives=False):
            send.start()
        token[...] = jnp.zeros_like(token)

    hbm = lambda a: pltpu.with_memory_space_constraint(a, pltpu.HBM)
    outs = pl.pallas_call(
        body, name=name,
        out_shape=(pltpu.SemaphoreType.DMA((n * (N_DEV - 1),)), pltpu.SemaphoreType.DMA((n * (N_DEV - 1),)),
                   *[pltpu.HBM(s.shape, s.dtype) for s in srcs],
                   *[pltpu.HBM(ls, s.dtype) for ls, s in zip(land_shapes, srcs)],
                   _sds((8, LANES), F32)),
        in_specs=[_HBM] * (2 * n),
        out_specs=(_SEM, _SEM, *[_HBM] * (2 * n), pl.BlockSpec(memory_space=pltpu.VMEM)),
        input_output_aliases={i: 2 + i for i in range(2 * n)},
        compiler_params=pltpu.CompilerParams(has_side_effects=_EFFECT),
    )(*[hbm(s) for s in srcs], *[hbm(lax.empty(ls, s.dtype)) for ls, s in zip(land_shapes, srcs)])
    return outs[0], outs[1], list(outs[2:2 + n]), list(outs[2 + n:2 + 2 * n]), outs[-1][0, 0]


def _exchange_wait(name, started, stacked, after):
    srcs, lands = _wait_call(name, started, stacked, after)
    me = _my_slot()
    full = []
    for src, land, st in zip(srcs, lands, stacked):
        own = lax.dynamic_index_in_dim(src, me, 0, keepdims=True) if st else src[None]
        full.append(lax.dynamic_update_index_in_dim(land, own, me, 0))
    return full


def _wait_call(name, started, stacked, after):
    send_sems, recv_sems, srcs, lands, _ = started
    n = len(srcs)

    def body(*refs):
        src, land = refs[:n], refs[n:2 * n]
        s_sems, r_sems = refs[2 * n], refs[2 * n + 1]
        for send, recv in _exchange_copies(src, land, s_sems, r_sems, stacked):
            send.wait_send()
            recv.wait_recv()

    outs = pl.pallas_call(
        body, name=name,
        out_shape=tuple(pltpu.HBM(a.shape, a.dtype) for a in srcs + lands),
        in_specs=[_HBM] * (2 * n) + [_SEM, _SEM, pl.BlockSpec(memory_space=pl.ANY)],
        out_specs=[_HBM] * (2 * n),
        input_output_aliases={i: i for i in range(2 * n)},
        compiler_params=pltpu.CompilerParams(has_side_effects=_EFFECT),
    )(*srcs, *lands, send_sems, recv_sems, after)
    return outs[:n], outs[n:]


def _adamw(name, parts, w, m, v):
    rows, cols = w.shape
    cap = max(8, (LANES * 1024) // cols)
    tr = rows
    if rows > cap:
        tr = (cap // 8) * 8
        while rows % tr:
            tr -= 8
    c1 = 1.0 / (1.0 - ADAM_B1 ** ADAM_STEP)
    c2 = 1.0 / (1.0 - ADAM_B2 ** ADAM_STEP)

    def body(p_ref, w_ref, m_ref, v_ref, g_o, d_o, m_o, v_o):
        g = p_ref[0]
        for s in range(1, N_DEV):
            g = g + p_ref[s]
        mn = ADAM_B1 * m_ref[...] + (1.0 - ADAM_B1) * g
        vn = ADAM_B2 * v_ref[...] + (1.0 - ADAM_B2) * (g * g)
        g_o[...] = g
        m_o[...] = mn
        v_o[...] = vn
        d_o[...] = -ADAM_LR * ((mn * c1) / (jnp.sqrt(vn * c2) + ADAM_EPS) + ADAM_WD * w_ref[...])

    blk = pl.BlockSpec((tr, cols), lambda i: (i, 0))
    return _call(body, name=name, grid=(rows // tr,),
                 in_specs=[pl.BlockSpec((N_DEV, tr, cols), lambda i: (0, i, 0)), blk, blk, blk],
                 out_specs=[blk] * 4, out_shape=[_sds((rows, cols), F32)] * 4)(parts, w, m, v)


def _rope_tables(pos_col, inv_freq):
    T = pos_col.shape[0]
    tm = _tile(T, 1024)

    def body(p_ref, f_ref, c_o, sa_o, sb_o):
        ang = p_ref[...].astype(F32) * f_ref[...]
        lane = lax.broadcasted_iota(jnp.int32, ang.shape, 1)
        cs, sn = jnp.cos(ang), jnp.sin(ang)
        c_o[...] = jnp.where(lane < D_ROPE, cs, 0.0)
        sa_o[...] = jnp.where(lane < D_ROPE // 2, -sn, 0.0)
        sb_o[...] = jnp.where((lane >= D_ROPE // 2) & (lane < D_ROPE), sn, 0.0)

    blk = pl.BlockSpec((tm, LANES), lambda i: (i, 0))
    return _call(body, name="rope_tables", grid=(T // tm,),
                 in_specs=[pl.BlockSpec((tm, 1), lambda i: (i, 0)), pl.BlockSpec((1, LANES), lambda i: (0, 0))],
                 out_specs=[blk] * 3, out_shape=[_sds((T, LANES), F32)] * 3)(pos_col, inv_freq)


def _ln_in(x, g, b):
    T, D = x.shape
    tm = _tile(T, 512)

    def body(x_ref, g_ref, b_ref, o32, o16):
        xhat, _ = _ln_stats(x_ref[...])
        y = xhat * g_ref[...] + b_ref[...]
        o32[...] = y
        o16[...] = y.astype(BF16)

    blk = pl.BlockSpec((tm, D), lambda i: (i, 0))
    vec = pl.BlockSpec((1, D), lambda i: (0, 0))
    return _call(body, name="ln_in", grid=(T // tm,), in_specs=[blk, vec, vec], out_specs=[blk, blk],
                 out_shape=[_sds((T, D), F32), _sds((T, D), BF16)])(x, g, b)


def _mid(h, g_cq, g_ckv, tabs, C):
    T = h.shape[0]
    tm = _tile(T, 256)
    cq_blk, kr_blk = (2 * C) // LORA, (2 * C + 2 * LORA) // LANES

    def body(a_ref, gt_ref, cq_ref, ckv_ref, kr_ref, gq_ref, gkv_ref, cp, sa, sb, u_o, cqn_o, ckvn_o, kr_o):
        u_o[...] = a_ref[...] * _sigmoid(gt_ref[...])
        cqn_o[...] = _rms(cq_ref[...], gq_ref[...]).astype(BF16)
        ckvn_o[...] = _rms(ckv_ref[...], gkv_ref[...]).astype(BF16)
        kr_o[...] = _rope(kr_ref[...], cp[...], sa[...], sb[...]).astype(BF16)

    def col(w, j):
        return pl.BlockSpec((tm, w), lambda i: (i, j))

    vec = pl.BlockSpec((1, LORA), lambda i: (0, 0))
    return _call(body, name="mid_norm_glu", grid=(T // tm,),
                 in_specs=[col(C, 0), col(C, 1), col(LORA, cq_blk), col(LORA, cq_blk + 1), col(LANES, kr_blk),
                           vec, vec, col(LANES, 0), col(LANES, 0), col(LANES, 0)],
                 out_specs=[col(C, 0), col(LORA, 0), col(LORA, 0), col(LANES, 0)],
                 out_shape=[_sds((T, C), F32), _sds((T, LORA), BF16), _sds((T, LORA), BF16), _sds((T, LANES), BF16)],
                 )(h, h, h, h, h, g_cq, g_ckv, *tabs)


def _q_proj(cqn, w_uq_p, tabs):
    T = cqn.shape[0]
    tm = _tile(T, HEAD_ROWS)

    def body(c_ref, w_ref, cp, sa, sb, o_ref):
        q = jnp.dot(c_ref[...], w_ref[...], preferred_element_type=F32)
        o_ref[:, :D_NOPE] = (q[:, :D_NOPE] * Q_SCALE).astype(BF16)
        o_ref[:, D_NOPE:] = (_rope(q[:, D_NOPE:], cp[...], sa[...], sb[...]) * Q_SCALE).astype(BF16)

    tab = pl.BlockSpec((tm, LANES), lambda i, h: (i, 0))
    return _call(body, name="q_proj_rope", grid=(T // tm, N_HEADS),
                 in_specs=[pl.BlockSpec((tm, LORA), lambda i, h: (i, 0)),
                           pl.BlockSpec((None, LORA, D_HEAD_PAD), lambda i, h: (h, 0, 0)), tab, tab, tab],
                 out_specs=pl.BlockSpec((None, tm, D_HEAD_PAD), lambda i, h: (h, i, 0)),
                 out_shape=_sds((N_HEADS, T, D_HEAD_PAD), BF16))(cqn, w_uq_p, *tabs)


def _kv_proj(ckvn, w_ukv, kr):
    T = ckvn.shape[0]
    tm = _tile(T, HEAD_ROWS)

    def body(c_ref, w_ref, kr_ref, k_o, v_o):
        kv = jnp.dot(c_ref[...], w_ref[...], preferred_element_type=F32)
        k_o[:, :D_NOPE] = kv[:, :D_NOPE].astype(BF16)
        k_o[:, D_NOPE:] = kr_ref[...]
        v_o[:, :D_V] = kv[:, D_NOPE:].astype(BF16)
        v_o[:, D_V:] = jnp.ones((tm, D_V), BF16)

    return _call(body, name="kv_proj", grid=(T // tm, N_HEADS),
                 in_specs=[pl.BlockSpec((tm, LORA), lambda i, h: (i, 0)),
                           pl.BlockSpec((None, LORA, D_NOPE + D_V), lambda i, h: (h, 0, 0)),
                           pl.BlockSpec((tm, LANES), lambda i, h: (i, 0))],
                 out_specs=[pl.BlockSpec((None, tm, D_HEAD_PAD), lambda i, h: (h, i, 0)),
                            pl.BlockSpec((None, tm, 2 * D_V), lambda i, h: (h, i, 0))],
                 out_shape=[_sds((N_HEADS, T, D_HEAD_PAD), BF16), _sds((N_HEADS, T, 2 * D_V), BF16)])(ckvn, w_ukv, kr)


def _flash_fwd(q, k, v1, out_cols):
    _, T, _ = q.shape
    tq, tk = _tile(T, FLASH_TQ), _tile(T, FLASH_TK)
    nkv, reps = T // tk, tk // LANES

    def body(q_ref, k_ref, v_ref, o_ref, lse_ref, m_sc, acc_sc):
        m_sc[...] = jnp.full_like(m_sc, -jnp.inf)
        acc_sc[...] = jnp.zeros_like(acc_sc)
        qv = q_ref[...]

        def rows(j):
            return pl.ds(pl.multiple_of(j * tk, tk), tk)

        def scores(j):
            return lax.dot_general(qv, k_ref[rows(j), :], NT, preferred_element_type=F32)

        def update(s, j):
            m_prev = m_sc[...]
            m_new = jnp.maximum(m_prev, jnp.max(s, axis=1, keepdims=True))
            a = jnp.exp2(m_prev - m_new)
            p = jnp.exp2(s - jnp.tile(m_new, (1, reps)))
            pv = jnp.dot(p.astype(BF16), v_ref[rows(j), :], preferred_element_type=F32)
            acc_sc[...] = jnp.tile(a, (1, 2)) * acc_sc[...] + pv
            m_sc[...] = m_new

        if nkv % 2:
            def step(j, carry):
                update(scores(j), j)
                return carry

            lax.fori_loop(0, nkv, step, 0)
        else:
            def pair(jj, s_even):
                s_odd = scores(2 * jj + 1)
                update(s_even, 2 * jj)
                s_next = scores(2 * jj + 2)
                update(s_odd, 2 * jj + 1)
                return s_next

            s_even = lax.fori_loop(0, nkv // 2 - 1, pair, scores(0))
            s_odd = scores(nkv - 1)
            update(s_even, nkv - 2)
            update(s_odd, nkv - 1)
        acc = acc_sc[...]
        l = acc[:, D_V:]
        o_ref[...] = (acc[:, :D_V] / l).astype(BF16)
        lse_ref[...] = m_sc[...] + jnp.log(l) * LOG2_E

    return _call(body, name="flash_fwd", grid=(N_HEADS, T // tq),
                 in_specs=[pl.BlockSpec((None, tq, D_HEAD_PAD), lambda h, i: (h, i, 0)),
                           pl.BlockSpec((None, T, D_HEAD_PAD), lambda h, i: (h, 0, 0)),
                           pl.BlockSpec((None, T, 2 * D_V), lambda h, i: (h, 0, 0))],
                 out_specs=[pl.BlockSpec((tq, D_V), lambda h, i: (i, h)),
                            pl.BlockSpec((None, tq, LANES), lambda h, i: (h, i, 0))],
                 out_shape=[_sds((T, out_cols), BF16), _sds((N_HEADS, T, LANES), F32)],
                 scratch=[pltpu.VMEM((tq, LANES), F32), pltpu.VMEM((tq, 2 * D_V), F32)])(q, k, v1)


def _halo_specs(tm, cb, n_t):
    r = tm // HALO
    return [pl.BlockSpec((HALO, cb), lambda jc, i: (jnp.maximum(i * r - 1, 0), jc)),
            pl.BlockSpec((tm, cb), lambda jc, i: (i, jc)),
            pl.BlockSpec((HALO, cb), lambda jc, i: (jnp.minimum((i + 1) * r, n_t * r - 1), jc))]


def _fill_ext(ext, prev_ref, cur_ref, next_ref, i, n_t, tm):
    ext[0:HALO, :] = jnp.where(i > 0, prev_ref[...], 0.0)
    ext[HALO:HALO + tm, :] = cur_ref[...]
    ext[HALO + tm:, :] = jnp.where(i < n_t - 1, next_ref[...], 0.0)


HEAD_ROWS = 2048
FLASH_TQ = 512
FLASH_TK = 512
EPILOGUE_ROWS = 128
CONV_ROWS = 64


def _conv_fwd(u, w_pad, bias):
    T, C = u.shape
    tm, cb = _tile(T, 256), _tile(C, 256)
    n_t = T // tm
    rb = min(CONV_ROWS, tm)

    def body(up, uc, un, w_ref, b_ref, c_o, ext):
        i = pl.program_id(1)
        _fill_ext(ext, up, uc, un, i, n_t, tm)
        for r0 in range(0, tm, rb):
            acc = jnp.zeros((rb, cb), F32) + b_ref[...]
            for k in range(CONV_W):
                acc = acc + w_ref[k:k + 1, :] * ext[r0 + k + 1:r0 + k + 1 + rb, :]
            c_o[r0:r0 + rb, :] = acc

    return _call(body, name="conv_fwd", grid=(C // cb, n_t),
                 in_specs=_halo_specs(tm, cb, n_t) + [pl.BlockSpec((CONV_W_PAD, cb), lambda jc, i: (0, jc)),
                                                      pl.BlockSpec((1, cb), lambda jc, i: (0, jc))],
                 out_specs=pl.BlockSpec((tm, cb), lambda jc, i: (i, jc)),
                 out_shape=_sds((T, C), F32),
                 scratch=[pltpu.VMEM((tm + 2 * HALO, cb), F32)])(u, u, u, w_pad, bias)


def _conv_post(c, g, b, cat):
    T, C = c.shape
    assert cat.shape == (T, 2 * C)
    tm = _tile(T, 512)

    def body(c_ref, g_ref, b_ref, cat_ref, o_ref):
        xhat, _ = _ln_stats(c_ref[...])
        y = xhat * g_ref[...] + b_ref[...]
        o_ref[...] = (y * _sigmoid(y)).astype(BF16)

    blk = pl.BlockSpec((tm, C), lambda i: (i, 0))
    vec = pl.BlockSpec((1, C), lambda i: (0, 0))
    return _call(body, name="conv_ln_silu", grid=(T // tm,),
                 in_specs=[blk, vec, vec, pl.BlockSpec(memory_space=pl.ANY)],
                 out_specs=pl.BlockSpec((tm, C), lambda i: (i, 1)),
                 out_shape=_sds(cat.shape, BF16), aliases={3: 0})(c, g, b, cat)


def _conv_post_bwd(dcat, c, g, b):
    T, C = c.shape
    tm = _tile(T, 512)

    def body(d_ref, c_ref, g_ref, b_ref, dc_o, dg_o, db_o):
        first = pl.program_id(0) == 0
        xhat, rstd = _ln_stats(c_ref[...])
        y = xhat * g_ref[...] + b_ref[...]
        sg = _sigmoid(y)
        dy = d_ref[...] * (sg * (1.0 + y * (1.0 - sg)))
        _acc_out(dg_o, _colsum(dy * xhat), first)
        _acc_out(db_o, _colsum(dy), first)
        dc_o[...] = _ln_bwd(dy, xhat, rstd, g_ref[...])

    blk = pl.BlockSpec((tm, C), lambda i: (i, 0))
    vec = pl.BlockSpec((1, C), lambda i: (0, 0))
    return _call(body, name="conv_ln_silu_bwd", grid=(T // tm,),
                 in_specs=[pl.BlockSpec((tm, C), lambda i: (i, 1)), blk, vec, vec],
                 out_specs=[blk, vec, vec],
                 out_shape=[_sds((T, C), F32), _sds((1, C), F32), _sds((1, C), F32)])(dcat, c, g, b)


def _conv_bwd(dc, u, w_pad):
    T, C = u.shape
    tm, cb = _tile(T, 256), _tile(C, 256)
    n_t = T // tm
    rb = min(CONV_ROWS, tm)

    def body(dp, dcur, dn, up, uc, un, w_ref, du_o, dw_o, db_o, dext, uext, dw_sc):
        i = pl.program_id(1)
        _fill_ext(dext, dp, dcur, dn, i, n_t, tm)
        _fill_ext(uext, up, uc, un, i, n_t, tm)

        @pl.when(i == 0)
        def _():
            dw_sc[...] = jnp.zeros_like(dw_sc)

        for r0 in range(0, tm, rb):
            acc = jnp.zeros((rb, cb), F32)
            d_here = dcur[r0:r0 + rb, :]
            for k in range(CONV_W):
                acc = acc + w_ref[k:k + 1, :] * dext[r0 + 2 * HALO - 1 - k:r0 + 2 * HALO - 1 - k + rb, :]
                prod = d_here * uext[r0 + k + 1:r0 + k + 1 + rb, :]
                dw_sc[k] += jnp.sum(prod.reshape(rb // 8, 8, cb), axis=0)
            du_o[r0:r0 + rb, :] = acc
        dw_sc[CONV_W] += jnp.sum(dcur[...].reshape(tm // 8, 8, cb), axis=0)

        @pl.when(i == n_t - 1)
        def _():
            red = jnp.sum(dw_sc[...], axis=1)
            row = lax.broadcasted_iota(jnp.int32, red.shape, 0)
            dw_o[...] = jnp.where(row < CONV_W, red, 0.0)
            db_o[...] = jnp.sum(jnp.where(row == CONV_W, red, 0.0), axis=0, keepdims=True)

    return _call(body, name="conv_bwd", grid=(C // cb, n_t),
                 in_specs=_halo_specs(tm, cb, n_t) + _halo_specs(tm, cb, n_t)
                 + [pl.BlockSpec((CONV_W_PAD, cb), lambda jc, i: (0, jc))],
                 out_specs=[pl.BlockSpec((tm, cb), lambda jc, i: (i, jc)),
                            pl.BlockSpec((CONV_W_PAD, cb), lambda jc, i: (0, jc)),
                            pl.BlockSpec((1, cb), lambda jc, i: (0, jc))],
                 out_shape=[_sds((T, C), F32), _sds((CONV_W_PAD, C), F32), _sds((1, C), F32)],
                 scratch=[pltpu.VMEM((tm + 2 * HALO, cb), F32), pltpu.VMEM((tm + 2 * HALO, cb), F32),
                          pltpu.VMEM((CONV_W_PAD, 8, cb), F32)])(dc, dc, dc, u, u, u, w_pad)


def _glu_bwd(du, h, C):
    T = du.shape[0]
    tm = _tile(T, 512)

    def body(du_ref, a_ref, gt_ref, o_ref):
        sg = _sigmoid(gt_ref[...])
        du_v = du_ref[...]
        o_ref[:, :C] = (du_v * sg).astype(BF16)
        o_ref[:, C:] = (du_v * a_ref[...] * sg * (1.0 - sg)).astype(BF16)

    return _call(body, name="glu_bwd", grid=(T // tm,),
                 in_specs=[pl.BlockSpec((tm, C), lambda i: (i, 0)), pl.BlockSpec((tm, C), lambda i: (i, 0)),
                           pl.BlockSpec((tm, C), lambda i: (i, 1))],
                 out_specs=pl.BlockSpec((tm, 2 * C), lambda i: (i, 0)),
                 out_shape=_sds(h.shape, BF16))(du, h, h)


def _attn_delta(dcat, attn):
    T = attn.shape[0]
    tm = _tile(T, HEAD_ROWS)

    def body(d_ref, o_ref, dl_o, dob_o):
        d = d_ref[...]
        dl = jnp.sum(d * o_ref[...].astype(F32), axis=1, keepdims=True)
        dl_o[...] = jnp.broadcast_to(dl, (tm, LANES))
        dob_o[...] = d.astype(BF16)

    blk = pl.BlockSpec((tm, D_V), lambda i, h: (i, h))
    hblk = pl.BlockSpec((None, tm, D_V), lambda i, h: (h, i, 0))
    return _call(body, name="attn_delta", grid=(T // tm, N_HEADS), in_specs=[blk, blk], out_specs=[hblk, hblk],
                 out_shape=[_sds((N_HEADS, T, LANES), F32), _sds((N_HEADS, T, D_V), BF16)])(dcat, attn)


def _flash_bwd(q, k, v1, do, lse, delta):
    _, T, _ = q.shape
    tq, tk = _tile(T, FLASH_TQ), _tile(T, FLASH_TK)
    nkv, reps = T // tk, tk // LANES

    def body(q_ref, do_ref, lse_ref, dl_ref, k_ref, v_ref, dq_o, dk_o, dv_o, dq_sc):
        @pl.when(pl.program_id(1) == 0)
        def _():
            dk_o[...] = jnp.zeros_like(dk_o)
            dv_o[...] = jnp.zeros_like(dv_o)

        qv, dov = q_ref[...], do_ref[...]
        lse_t = jnp.tile(lse_ref[...], (1, reps))
        dl_t = jnp.tile(dl_ref[...], (1, reps))
        dq_sc[...] = jnp.zeros_like(dq_sc)

        def rows(j):
            return pl.ds(pl.multiple_of(j * tk, tk), tk)

        def scores(j):
            s = lax.dot_general(qv, k_ref[rows(j), :], NT, preferred_element_type=F32)
            dp = lax.dot_general(dov, v_ref[rows(j), :D_V], NT, preferred_element_type=F32)
            return s, dp

        def update(s_dp, j):
            s, dp = s_dp
            p = jnp.exp2(s - lse_t)
            ds = (p * (dp - dl_t)).astype(BF16)
            dv_o[rows(j), :] += lax.dot_general(p.astype(BF16), dov, TN, preferred_element_type=F32)
            dk_o[rows(j), :] += lax.dot_general(ds, qv, TN, preferred_element_type=F32)
            dq_sc[...] += jnp.dot(ds, k_ref[rows(j), :], preferred_element_type=F32)

        def step(j, carry):
            update(scores(j), j)
            return carry

        lax.fori_loop(0, nkv, step, 0)
        dq_o[...] = dq_sc[...]

    def tile(w):
        return pl.BlockSpec((None, tq, w), lambda h, i: (h, i, 0))

    def whole(w):
        return pl.BlockSpec((None, T, w), lambda h, i: (h, 0, 0))

    return _call(body, name="flash_bwd", grid=(N_HEADS, T // tq),
                 in_specs=[tile(D_HEAD_PAD), tile(D_V), tile(LANES), tile(LANES), whole(D_HEAD_PAD), whole(2 * D_V)],
                 out_specs=[tile(D_HEAD_PAD), whole(D_HEAD_PAD), whole(D_V)],
                 out_shape=[_sds((N_HEADS, T, D_HEAD_PAD), F32), _sds((N_HEADS, T, D_HEAD_PAD), F32),
                            _sds((N_HEADS, T, D_V), F32)],
                 scratch=[pltpu.VMEM((tq, D_HEAD_PAD), F32)])(q, do, lse, delta, k, v1)


def _dq_post(dq, tabs):
    _, T, _ = dq.shape
    tm = _tile(T, HEAD_ROWS)

    def body(d_ref, cp, sa, sb, o_ref):
        d = d_ref[...] * SCALE
        o_ref[:, :D_NOPE] = d[:, :D_NOPE].astype(BF16)
        o_ref[:, D_NOPE:] = _unrope(d[:, D_NOPE:], cp[...], sa[...], sb[...]).astype(BF16)

    blk = pl.BlockSpec((None, tm, D_HEAD_PAD), lambda i, h: (h, i, 0))
    tab = pl.BlockSpec((tm, LANES), lambda i, h: (i, 0))
    return _call(body, name="dq_unrope", grid=(T // tm, N_HEADS), in_specs=[blk, tab, tab, tab], out_specs=blk,
                 out_shape=_sds(dq.shape, BF16))(dq, *tabs)


def _dk_post(dk, dv, tabs, dh, kr_blk):
    _, T, _ = dk.shape
    tm = _tile(T, 1024)

    def body(dk_ref, dv_ref, cp, sa, sb, dh_ref, dkv_o, dkr_o, sc):
        h = pl.program_id(1)
        d = dk_ref[...] * LN_2
        dkv_o[:, :D_NOPE] = d[:, :D_NOPE].astype(BF16)
        dkv_o[:, D_NOPE:] = dv_ref[...].astype(BF16)

        @pl.when(h == 0)
        def _():
            sc[...] = d[:, D_NOPE:]

        @pl.when(h > 0)
        def _():
            sc[...] += d[:, D_NOPE:]

        @pl.when(h == N_HEADS - 1)
        def _():
            dkr_o[...] = _unrope(sc[...], cp[...], sa[...], sb[...]).astype(BF16)

    tab = pl.BlockSpec((tm, LANES), lambda i, h: (i, 0))
    return _call(body, name="dk_unrope", grid=(T // tm, N_HEADS),
                 in_specs=[pl.BlockSpec((None, tm, D_HEAD_PAD), lambda i, h: (h, i, 0)),
                           pl.BlockSpec((None, tm, D_V), lambda i, h: (h, i, 0)), tab, tab, tab,
                           pl.BlockSpec(memory_space=pl.ANY)],
                 out_specs=[pl.BlockSpec((None, tm, D_HEAD_PAD), lambda i, h: (h, i, 0)),
                            pl.BlockSpec((tm, LANES), lambda i, h: (i, kr_blk))],
                 out_shape=[_sds((N_HEADS, T, D_HEAD_PAD), BF16), _sds(dh.shape, BF16)],
                 scratch=[pltpu.VMEM((tm, LANES), F32)], aliases={5: 1})(dk, dv, *tabs, dh)


def _latent_bwd(name, dproj, w_heads, h, col_blk, g, dh):
    _, T, _ = dproj.shape
    tm = _tile(T, 1024)

    def ep(acc, ex, out, first):
        dx, dg = _rms_bwd(acc, ex[0][...], ex[1][...])
        out[0][...] = dx.astype(BF16)
        _acc_out(out[1], dg, first)

    return _matmul(name, dproj, w_heads, dims=NT, grid=(T // tm, 1, N_HEADS),
                   a_spec=pl.BlockSpec((None, tm, D_HEAD_PAD), lambda i, j, k: (k, i, 0)),
                   b_spec=pl.BlockSpec((None, LORA, D_HEAD_PAD), lambda i, j, k: (k, 0, 0)),
                   acc_shape=(tm, LORA),
                   extras=[(h, pl.BlockSpec((tm, LORA), lambda i, j, k: (i, col_blk))),
                           (g, pl.BlockSpec((1, LORA), lambda i, j, k: (0, 0)))],
                   outs=[(_sds(dh.shape, BF16), pl.BlockSpec((tm, LORA), lambda i, j, k: (i, col_blk))),
                         (_sds((1, LORA), F32), pl.BlockSpec((1, LORA), lambda i, j, k: (0, 0)))],
                   epilogue=ep, into=dh)


def _head_weight_grad(name, latent, dproj):
    _, T, _ = dproj.shape
    tk = _tile(T, 2048)
    return _matmul(name, latent, dproj, dims=TN, grid=(N_HEADS, 1, T // tk),
                   a_spec=pl.BlockSpec((tk, LORA), lambda i, j, k: (k, 0)),
                   b_spec=pl.BlockSpec((None, tk, D_HEAD_PAD), lambda i, j, k: (i, k, 0)),
                   acc_shape=(LORA, D_HEAD_PAD),
                   outs=[(_sds((N_HEADS, LORA, D_HEAD_PAD), F32),
                          pl.BlockSpec((None, LORA, D_HEAD_PAD), lambda i, j, k: (i, 0, 0)))],
                   epilogue=_store())[0]


def _weight_grad(name, a, b, tm_pref=1024, tn_pref=1024, stacked_cols=None):
    T, M = a.shape
    N = b.shape[1]
    tk = _tile(T, 2048)
    tm = _tile(M, tm_pref)
    if stacked_cols is None:
        tn = _tile(N, tn_pref)
        out = (_sds((M, N), F32), pl.BlockSpec((tm, tn), lambda i, j, k: (i, j)))
    else:
        tn = _tile(stacked_cols, tn_pref)
        per = stacked_cols // tn
        out = (_sds((N // stacked_cols, M, stacked_cols), F32),
               pl.BlockSpec((None, tm, tn), lambda i, j, k: (j // per, i, j % per)))
    return _matmul(name, a, b, dims=TN, grid=(M // tm, N // tn, T // tk),
                   a_spec=pl.BlockSpec((tk, tm), lambda i, j, k: (k, i)),
                   b_spec=pl.BlockSpec((tk, tn), lambda i, j, k: (k, j)),
                   acc_shape=(tm, tn), outs=[out], epilogue=_store())[0]


def _small_names():
    return ["ln_in_g", "ln_in_b", "g_cq", "g_ckv", "conv_b", "g_conv_ln", "b_conv_ln", "g_ln1", "b_ln1", "g_ln2", "b_ln2"]


def _pack(vecs):
    flat = jnp.concatenate([v.reshape(-1) for v in vecs])
    assert flat.shape[0] % (8 * LANES) == 0
    return flat.reshape(-1, LANES)


def _unpack(packed, like):
    flat, out, off = packed.reshape(-1), [], 0
    for v in like:
        out.append(flat[off:off + v.size].reshape(v.shape))
        off += v.size
    return out


def kernel(x, positions, ln_in_g, ln_in_b, w_in, g_cq, w_uq, g_ckv, w_uk, w_uv, conv_w, conv_b, g_conv_ln, b_conv_ln, w_out, g_ln1, b_ln1, w_ff1, w_ff2, g_ln2, b_ln2, loss_target, m_ln_in_g, m_ln_in_b, m_w_in, m_g_cq, m_w_uq, m_g_ckv, m_w_uk, m_w_uv, m_conv_w, m_conv_b, m_g_conv_ln, m_b_conv_ln, m_w_out, m_g_ln1, m_b_ln1, m_w_ff1, m_w_ff2, m_g_ln2, m_b_ln2, v_ln_in_g, v_ln_in_b, v_w_in, v_g_cq, v_w_uq, v_g_ckv, v_w_uk, v_w_uv, v_conv_w, v_conv_b, v_g_conv_ln, v_b_conv_ln, v_w_out, v_g_ln1, v_b_ln1, v_w_ff1, v_w_ff2, v_g_ln2, v_b_ln2):
    args = dict(locals())
    T, D = x.shape[1], x.shape[2]
    C = D - N_HEADS * D_V
    Fs = w_ff1.shape[2]
    F = N_DEV * Fs
    n_in = N_DEV * w_in.shape[2]
    n_in_p = 2 * C + 2 * LORA + LANES
    assert n_in == 2 * LORA + D_ROPE + 2 * C and w_uq.shape[2] == D_QK and conv_w.shape[2] * N_DEV == C

    xs, tgt = x[0], loss_target[0]
    row = lambda v_: v_.reshape(1, -1)

    gather = lambda k_: [False] * k_
    ag_in = _exchange_start("ag_in_start", [w_in[0].astype(BF16)], gather(1))
    ag_heads = _exchange_start("ag_heads_start", [w_uq[0].astype(BF16), w_uk[0].astype(BF16), w_uv[0].astype(BF16),
                                                  conv_w[0]], gather(4))
    ag_ff = _exchange_start("ag_ff_start", [w_out[0].astype(BF16), w_ff1[0].astype(BF16), w_ff2[0].astype(BF16)],
                            gather(3))
    started = ag_in[4] + ag_heads[4] + ag_ff[4]

    half = D_ROPE // 2
    inv_freq = ROPE_BASE ** (-jnp.arange(half, dtype=F32) * (2.0 / D_ROPE))
    inv_freq = jnp.tile(inv_freq, LANES // half).reshape(1, LANES)
    tabs = _rope_tables(positions.reshape(T, 1), inv_freq)

    x0, x0b = _ln_in(xs, row(ln_in_g) + started, row(ln_in_b))

    (g_w_in,) = _exchange_wait("ag_in_wait", ag_in, gather(1), after=x0b)
    w_in_f = jnp.transpose(g_w_in, (1, 0, 2)).reshape(D, n_in)
    s_cq, s_ckv, s_kr, s_a, s_g = 0, LORA, 2 * LORA, 2 * LORA + D_ROPE, 2 * LORA + D_ROPE + C
    w_in_p = jnp.concatenate([w_in_f[:, s_a:s_g], w_in_f[:, s_g:], w_in_f[:, s_cq:s_ckv], w_in_f[:, s_ckv:s_kr],
                              w_in_f[:, s_kr:s_a], jnp.zeros((D, LANES - D_ROPE), BF16)], axis=1)

    tm, tn = _tile(T, 512), _tile(n_in_p, 640)
    h = _matmul("h_proj", x0b, w_in_p, dims=NN, grid=(T // tm, n_in_p // tn, 1),
                a_spec=pl.BlockSpec((tm, D), lambda i, j, k: (i, 0)),
                b_spec=pl.BlockSpec((D, tn), lambda i, j, k: (0, j)), acc_shape=(tm, tn),
                outs=[(_sds((T, n_in_p), F32), pl.BlockSpec((tm, tn), lambda i, j, k: (i, j)))],
                epilogue=_store())[0]

    u, cqn, ckvn, kr = _mid(h, g_cq, g_ckv, tabs, C)
    g_w_uq, g_w_uk, g_w_uv, g_conv_w = _exchange_wait("ag_heads_wait", ag_heads, gather(4), after=cqn)
    w_uq_p = jnp.pad(g_w_uq, ((0, 0), (0, 0), (0, D_HEAD_PAD - D_QK)))
    w_ukv = jnp.concatenate([g_w_uk, g_w_uv], axis=2)
    conv_w_f = jnp.pad(jnp.transpose(g_conv_w, (1, 0, 2)).reshape(CONV_W, C), ((0, CONV_W_PAD - CONV_W), (0, 0)))
    q = _q_proj(cqn, w_uq_p, tabs)
    kf, vf = _kv_proj(ckvn, w_ukv, kr)
    attn, lse = _flash_fwd(q, kf, vf, D)
    conv_c = _conv_fwd(u, conv_w_f, conv_b)
    cat = _conv_post(conv_c, g_conv_ln, b_conv_ln, attn)
    g_w_out, g_w_ff1, g_w_ff2 = _exchange_wait("ag_ff_wait", ag_ff, gather(3), after=cat)
    w_out_f = g_w_out.reshape(D, D)
    w_ff2_f = g_w_ff2.reshape(F, D)

    def ep_ln1(acc, ex, out, first):
        z1 = ALPHA * ex[0][...] + acc
        xhat, _ = _ln_stats(z1)
        x1 = xhat * ex[1][...] + ex[2][...]
        out[0][...] = z1
        out[1][...] = x1
        out[2][...] = x1.astype(BF16)

    tm = _tile(T, 256)
    rowblk = pl.BlockSpec((tm, D), lambda i, j, k: (i, 0))
    vecD = pl.BlockSpec((1, D), lambda i, j, k: (0, 0))
    z1, x1, x1b = _matmul("mix_ln1", cat, w_out_f, dims=NN, grid=(T // tm, 1, 1), a_spec=rowblk,
                          b_spec=pl.BlockSpec((D, D), lambda i, j, k: (0, 0)), acc_shape=(tm, D),
                          extras=[(x0, rowblk), (g_ln1, vecD), (b_ln1, vecD)],
                          outs=[(_sds((T, D), F32), rowblk), (_sds((T, D), F32), rowblk), (_sds((T, D), BF16), rowblk)],
                          epilogue=ep_ln1, ep_rows=EPILOGUE_ROWS)

    def ep_ff1(acc, ex, out, first):
        r = jnp.maximum(acc, 0.0)
        out[0][...] = (r * r).astype(BF16)
        out[1][...] = r.astype(BF16)

    tm, tn = _tile(T, 512), _tile(Fs, 1024)
    per = Fs // tn
    fblk = pl.BlockSpec((tm, tn), lambda i, j, k: (i, j))
    f_act, r_act = _matmul("ff1_relu2", x1b, g_w_ff1, dims=NN, grid=(T // tm, F // tn, 1),
                           a_spec=pl.BlockSpec((tm, D), lambda i, j, k: (i, 0)),
                           b_spec=pl.BlockSpec((None, D, tn), lambda i, j, k: (j // per, 0, j % per)),
                           acc_shape=(tm, tn), outs=[(_sds((T, F), BF16), fblk), (_sds((T, F), BF16), fblk)],
                           epilogue=ep_ff1)

    def ep_ln2(acc, ex, out, first):
        g2 = ex[2][...]
        z2 = ALPHA * ex[0][...] + acc
        xhat, rstd = _ln_stats(z2)
        err = xhat * g2 + ex[3][...] - ex[1][...]
        part = 0.5 * jnp.sum(jnp.mean(err * err, axis=-1, keepdims=True))
        _acc_out(out[2], jnp.zeros((8, LANES), F32) + part, first)
        dy = err * (1.0 / D)
        _acc_out(out[3], _colsum(dy * xhat), first)
        _acc_out(out[4], _colsum(dy), first)
        dz2 = _ln_bwd(dy, xhat, rstd, g2)
        out[0][...] = dz2
        out[1][...] = dz2.astype(BF16)

    tm, tk = _tile(T, 512), _tile(F, 1024)
    rowblk = pl.BlockSpec((tm, D), lambda i, j, k: (i, 0))
    dz2, dz2b, loss_blk, dg_ln2, db_ln2 = _matmul(
        "ff2_ln2_loss", f_act, w_ff2_f, dims=NN, grid=(T // tm, 1, F // tk),
        a_spec=pl.BlockSpec((tm, tk), lambda i, j, k: (i, k)), b_spec=pl.BlockSpec((tk, D), lambda i, j, k: (k, 0)),
        acc_shape=(tm, D), extras=[(x1, rowblk), (tgt, rowblk), (g_ln2, vecD), (b_ln2, vecD)],
        outs=[(_sds((T, D), F32), rowblk), (_sds((T, D), BF16), rowblk),
              (_sds((8, LANES), F32), pl.BlockSpec((8, LANES), lambda i, j, k: (0, 0))),
              (_sds((1, D), F32), vecD), (_sds((1, D), F32), vecD)],
        epilogue=ep_ln2, ep_rows=EPILOGUE_ROWS)
    loss = lax.psum(loss_blk[0, 0], ("x", "y", "c"))

    def ep_dpre(acc, ex, out, first):
        out[0][...] = (acc * (2.0 * ex[0][...].astype(F32))).astype(BF16)

    tm, tn = _tile(T, 512), _tile(F, 1024)
    fblk = pl.BlockSpec((tm, tn), lambda i, j, k: (i, j))
    dpre = _matmul("ff2_dgrad", dz2b, w_ff2_f, dims=NT, grid=(T // tm, F // tn, 1),
                   a_spec=pl.BlockSpec((tm, D), lambda i, j, k: (i, 0)), b_spec=pl.BlockSpec((tn, D), lambda i, j, k: (j, 0)),
                   acc_shape=(tm, tn), extras=[(r_act, fblk)], outs=[(_sds((T, F), BF16), fblk)], epilogue=ep_dpre)[0]

    dw_ff2 = _weight_grad("ff2_wgrad", f_act, dz2b).reshape(N_DEV, Fs, D)
    dw_ff1 = _weight_grad("ff1_wgrad", x1b, dpre, stacked_cols=Fs)
    scatter = lambda k_: [True] * k_
    rs_ff = _exchange_start("rs_ff_start", [dw_ff2, dw_ff1], scatter(2))

    def ep_ln1_bwd(acc, ex, out, first):
        dx1 = ALPHA * ex[0][...] + acc
        xhat, rstd = _ln_stats(ex[1][...])
        _acc_out(out[2], _colsum(dx1 * xhat), first)
        _acc_out(out[3], _colsum(dx1), first)
        dz1 = _ln_bwd(dx1, xhat, rstd, ex[2][...])
        out[0][...] = dz1
        out[1][...] = dz1.astype(BF16)

    tm, tk = _tile(T, 512), _tile(Fs, 1024)
    per = Fs // tk
    rowblk = pl.BlockSpec((tm, D), lambda i, j, k: (i, 0))
    dz1, dz1b, dg_ln1, db_ln1 = _matmul(
        "ff1_dgrad_ln1_bwd", dpre, g_w_ff1, dims=NT, grid=(T // tm, 1, F // tk),
        a_spec=pl.BlockSpec((tm, tk), lambda i, j, k: (i, k)),
        b_spec=pl.BlockSpec((None, D, tk), lambda i, j, k: (k // per, 0, k % per)),
        acc_shape=(tm, D), extras=[(dz2, rowblk), (z1, rowblk), (g_ln1 + rs_ff[4], vecD)],
        outs=[(_sds((T, D), F32), rowblk), (_sds((T, D), BF16), rowblk), (_sds((1, D), F32), vecD), (_sds((1, D), F32), vecD)],
        epilogue=ep_ln1_bwd, ep_rows=EPILOGUE_ROWS)

    dw_out = _weight_grad("out_wgrad", cat, dz1b).reshape(N_DEV, D // N_DEV, D)
    rs_out = _exchange_start("rs_out_start", [dw_out], scatter(1))
    tm, tn = _tile(T, 512), _tile(D, 1024)
    dcat = _matmul("out_dgrad", dz1b, w_out_f, dims=NT, grid=(T // tm, D // tn, 1),
                   a_spec=pl.BlockSpec((tm, D), lambda i, j, k: (i, 0)), b_spec=pl.BlockSpec((tn, D), lambda i, j, k: (j, 0)),
                   acc_shape=(tm, tn), outs=[(_sds((T, D), F32), pl.BlockSpec((tm, tn), lambda i, j, k: (i, j)))],
                   epilogue=_store())[0]

    dc, dg_conv_ln, db_conv_ln = _conv_post_bwd(dcat, conv_c, g_conv_ln + rs_out[4], b_conv_ln)
    du, dconv_w_p, dconv_b = _conv_bwd(dc, u, conv_w_f)
    dh = _glu_bwd(du, h, C)

    delta, do_heads = _attn_delta(dcat, cat)
    dq, dk, dv = _flash_bwd(q, kf, vf, do_heads, lse, delta)
    dq_raw = _dq_post(dq, tabs)
    cq_blk = (2 * C) // LORA
    dkv, dh = _dk_post(dk, dv, tabs, dh, (2 * C + 2 * LORA) // LANES)
    dh, dg_cq = _latent_bwd("q_dgrad_rms_bwd", dq_raw, w_uq_p, h, cq_blk, g_cq, dh)
    dh, dg_ckv = _latent_bwd("kv_dgrad_rms_bwd", dkv, w_ukv, h, cq_blk + 1, g_ckv, dh)
    dw_uq = _head_weight_grad("uq_wgrad", cqn, dq_raw)[:, :, :D_QK]
    dw_ukv = _head_weight_grad("ukv_wgrad", ckvn, dkv)
    dw_uk, dw_uv = dw_ukv[:, :, :D_NOPE], dw_ukv[:, :, D_NOPE:]

    dw_in_p = _weight_grad("in_wgrad", x0b, dh, tn_pref=640)
    dw_in_f = jnp.concatenate([dw_in_p[:, 2 * C:2 * C + 2 * LORA + D_ROPE], dw_in_p[:, :2 * C]], axis=1)
    dw_in = jnp.transpose(dw_in_f.reshape(D, N_DEV, n_in // N_DEV), (1, 0, 2))
    dconv_w = jnp.transpose(dconv_w_p[:CONV_W].reshape(CONV_W, N_DEV, C // N_DEV), (1, 0, 2))
    rs_in = _exchange_start("rs_in_start", [dw_in, dw_uq, dw_uk, dw_uv, dconv_w], scatter(5))

    def ep_ln_in_bwd(acc, ex, out, first):
        dx0 = ALPHA * ex[0][...] + acc
        xhat, rstd = _ln_stats(ex[1][...])
        _acc_out(out[1], _colsum(dx0 * xhat), first)
        _acc_out(out[2], _colsum(dx0), first)
        out[0][...] = _ln_bwd(dx0, xhat, rstd, ex[2][...])

    tm, tk = _tile(T, 512), _tile(n_in_p, 640)
    rowblk = pl.BlockSpec((tm, D), lambda i, j, k: (i, 0))
    grad_x, dg_ln_in, db_ln_in = _matmul(
        "in_dgrad_ln_in_bwd", dh, w_in_p, dims=NT, grid=(T // tm, 1, n_in_p // tk),
        a_spec=pl.BlockSpec((tm, tk), lambda i, j, k: (i, k)), b_spec=pl.BlockSpec((D, tk), lambda i, j, k: (0, k)),
        acc_shape=(tm, D), extras=[(dz1, rowblk), (xs, rowblk), (row(ln_in_g) + rs_in[4], vecD)],
        outs=[(_sds((T, D), F32), rowblk), (_sds((1, D), F32), vecD), (_sds((1, D), F32), vecD)],
        epilogue=ep_ln_in_bwd, ep_rows=EPILOGUE_ROWS)

    small = dict(ln_in_g=dg_ln_in, ln_in_b=db_ln_in, g_cq=dg_cq, g_ckv=dg_ckv, conv_b=dconv_b, g_conv_ln=dg_conv_ln,
                 b_conv_ln=db_conv_ln, g_ln1=dg_ln1, b_ln1=db_ln1, g_ln2=dg_ln2, b_ln2=db_ln2)
    names = _small_names()
    rs_small = _exchange_start("rs_small_start", [_pack([small[n] for n in names])], gather(1))
    res = {}

    def update(group, parts, after):
        last = after
        for n, p in zip(group, parts):
            outs_n = _adamw("adamw_" + n, p, args[n][0], args["m_" + n][0], args["v_" + n][0])
            res[n] = [o.reshape(args[n].shape) for o in outs_n]
            last = outs_n[0]
        return last

    done = update(["w_ff2", "w_ff1"], _exchange_wait("rs_ff_wait", rs_ff, scatter(2), after=grad_x), grad_x)
    done = update(["w_out"], _exchange_wait("rs_out_wait", rs_out, scatter(1), after=done), done)
    done = update(["w_in", "w_uq", "w_uk", "w_uv", "conv_w"],
                  _exchange_wait("rs_in_wait", rs_in, scatter(5), after=done), done)
    (small_parts,) = _exchange_wait("rs_small_wait", rs_small, gather(1), after=done)
    packed = _adamw("adamw_small", small_parts, _pack([args[n] for n in names]), _pack([args["m_" + n] for n in names]),
                    _pack([args["v_" + n] for n in names]))
    like = [args[n] for n in names]
    unpacked = [_unpack(p, like) for p in packed]
    for i, n in enumerate(names):
        res[n] = [unpacked[kind][i] for kind in range(4)]

    order = ["ln_in_g", "ln_in_b", "w_in", "g_cq", "w_uq", "g_ckv", "w_uk", "w_uv", "conv_w", "conv_b", "g_conv_ln",
             "b_conv_ln", "w_out", "g_ln1", "b_ln1", "w_ff1", "w_ff2", "g_ln2", "b_ln2"]
    outs = [loss, grad_x.reshape(x.shape)]
    for kind in range(4):
        outs += [res[n][kind] for n in order]
    return tuple(outs)
```

```python
import jax
import jax.numpy as jnp
from jax import lax
from jax.experimental import pallas as pl
from jax.experimental.pallas import tpu as pltpu

F32 = jnp.float32
BF16 = jnp.bfloat16

N_HEADS = 8
D_NOPE = 128
D_ROPE = 64
D_V = 128
D_QK = D_NOPE + D_ROPE
D_HEAD_PAD = 256
LORA = 512
CONV_W = 31
CONV_HALF = CONV_W // 2
CONV_W_PAD = 32
HALO = 16
LN_EPS = 1e-5
RMS_EPS = 1e-6
ALPHA = 2.0 ** 0.25
SCALE = float(D_QK) ** -0.5
LOG2_E = 1.4426950408889634
LN_2 = 0.6931471805599453
Q_SCALE = SCALE * LOG2_E
ROPE_BASE = 10000.0
ADAM_LR, ADAM_B1, ADAM_B2, ADAM_EPS, ADAM_WD, ADAM_STEP = 0.001, 0.9, 0.999, 1e-08, 0.01, 10

N_DEV = 8
LANES = 128
VMEM_LIMIT_V7X = 56 * 1024 * 1024

NN = (((1,), (0,)), ((), ()))
NT = (((1,), (1,)), ((), ()))
TN = (((0,), (0,)), ((), ()))


def _call(body, *, name, grid, in_specs, out_specs, out_shape, scratch=(), aliases=None):
    params = pltpu.CompilerParams(dimension_semantics=("arbitrary",) * len(grid),
                                  vmem_limit_bytes=VMEM_LIMIT_V7X)
    return pl.pallas_call(body, name=name, grid=grid, in_specs=in_specs, out_specs=out_specs,
                          out_shape=out_shape, scratch_shapes=scratch, compiler_params=params,
                          input_output_aliases=aliases or {})


def _tile(n, pref):
    if n <= pref:
        return n
    t = (pref // LANES) * LANES
    while t > LANES and n % t:
        t -= LANES
    assert n % t == 0, (n, pref)
    return t


def _sds(shape, dtype):
    return jax.ShapeDtypeStruct(shape, dtype)


def _ln_stats(z):
    mu = jnp.mean(z, axis=-1, keepdims=True)
    zc = z - mu
    var = jnp.mean(zc * zc, axis=-1, keepdims=True)
    rstd = lax.rsqrt(var + LN_EPS)
    return zc * rstd, rstd


def _ln_bwd(dy, xhat, rstd, g):
    gd = dy * g
    m1 = jnp.mean(gd, axis=-1, keepdims=True)
    m2 = jnp.mean(gd * xhat, axis=-1, keepdims=True)
    return rstd * (gd - m1 - xhat * m2)


def _rms(x, g):
    return x * lax.rsqrt(jnp.mean(x * x, axis=-1, keepdims=True) + RMS_EPS) * g


def _rms_bwd(dy, x, g):
    r = lax.rsqrt(jnp.mean(x * x, axis=-1, keepdims=True) + RMS_EPS)
    dxn = dy * g
    dx = r * dxn - x * (r * r * r) * jnp.mean(dxn * x, axis=-1, keepdims=True)
    dg = jnp.sum(dy * x * r, axis=0, keepdims=True)
    return dx, dg


def _sigmoid(x):
    return 1.0 / (1.0 + jnp.exp(-x))


def _rope(x, cos_p, sin_a, sin_b):
    return x * cos_p + pltpu.roll(x, 96, 1) * sin_a + pltpu.roll(x, 32, 1) * sin_b


def _unrope(d, cos_p, sin_a, sin_b):
    return d * cos_p - pltpu.roll(d, 96, 1) * sin_a - pltpu.roll(d, 32, 1) * sin_b


def _colsum(v):
    return jnp.sum(v, axis=0, keepdims=True)


def _acc_out(ref, val, first):
    if first is False:
        ref[...] += val
        return

    @pl.when(first)
    def _():
        ref[...] = val

    @pl.when(jnp.logical_not(first))
    def _():
        ref[...] += val


class _Rows:
    def __init__(self, ref, sl):
        self.ref, self.sl = ref, sl

    def __getitem__(self, idx):
        assert idx is Ellipsis
        return self.ref[self.sl, :]

    def __setitem__(self, idx, val):
        assert idx is Ellipsis
        self.ref[self.sl, :] = val


def _matmul(name, a, b, *, dims, grid, a_spec, b_spec, acc_shape, outs, epilogue, extras=(), ep_rows=None, into=None):
    nk = grid[2]
    ne, no = len(extras), len(outs)
    tm = acc_shape[0]
    n_in = 2 + ne + (0 if into is None else 1)

    def finish(acc_rows, ex, out):
        first = pl.program_id(0) == 0
        if ep_rows is None or ep_rows >= tm:
            epilogue(acc_rows(slice(None)), ex, out, first)
            return
        for r0 in range(0, tm, ep_rows):
            sl = slice(r0, r0 + ep_rows)
            view = lambda r: _Rows(r, sl) if r.shape[0] == tm else r
            epilogue(acc_rows(sl), [view(r) for r in ex], [view(r) for r in out], first if r0 == 0 else False)

    def body(*refs):
        a_ref, b_ref = refs[0], refs[1]
        ex = refs[2:2 + ne]
        out = refs[n_in:n_in + no]
        if nk == 1:
            part = lax.dot_general(a_ref[...], b_ref[...], dims, preferred_element_type=F32)
            finish(lambda sl: part[sl, :], ex, out)
        else:
            acc = refs[n_in + no]
            k = pl.program_id(2)

            @pl.when(k == 0)
            def _():
                acc[...] = jnp.zeros_like(acc)

            acc[...] += lax.dot_general(a_ref[...], b_ref[...], dims, preferred_element_type=F32)

            @pl.when(k == nk - 1)
            def _():
                finish(lambda sl: acc[sl, :], ex, out)

    scratch = [] if nk == 1 else [pltpu.VMEM(acc_shape, F32)]
    ins = [a, b] + [e for e, _ in extras]
    in_specs = [a_spec, b_spec] + [s for _, s in extras]
    aliases = {}
    if into is not None:
        ins.append(into)
        in_specs.append(pl.BlockSpec(memory_space=pl.ANY))
        aliases = {n_in - 1: 0}
    return _call(body, name=name, grid=grid, in_specs=in_specs, out_specs=[s for _, s in outs],
                 out_shape=[o for o, _ in outs], scratch=scratch, aliases=aliases)(*ins)


def _store(dtype=F32):
    def ep(acc, ex, out, first):
        out[0][...] = acc.astype(dtype)
    return ep


def _mesh_pos():
    return lax.axis_index("x"), lax.axis_index("y"), lax.axis_index("c")


def _flip(v, bit):
    return 1 - v if bit else v


_HBM = pl.BlockSpec(memory_space=pltpu.HBM)
_SEM = pl.BlockSpec(memory_space=pltpu.SEMAPHORE)
_EFFECT = pltpu.SideEffectType.DATAFLOW_SIDE_EFFECTING


def _my_slot():
    x, y, c = _mesh_pos()
    return 4 * x + 2 * y + c


def _exchange_copies(srcs, lands, send_sems, recv_sems, stacked, receives=True):
    x, y, c = _mesh_pos()
    me = 4 * x + 2 * y + c
    pairs = []
    for w in range(len(srcs)):
        for k in range(1, N_DEV):
            peer = (_flip(x, k & 4), _flip(y, k & 2), _flip(c, k & 1))
            peer_slot = 4 * peer[0] + 2 * peer[1] + peer[2]
            to_peer = srcs[w].at[peer_slot] if stacked[w] else srcs[w]
            mine = srcs[w].at[me] if stacked[w] else srcs[w]
            s = w * (N_DEV - 1) + k - 1
            sems = dict(send_sem=send_sems.at[s], recv_sem=recv_sems.at[s],
                        device_id=peer, device_id_type=pl.DeviceIdType.MESH)
            send = pltpu.make_async_remote_copy(src_ref=to_peer, dst_ref=lands[w].at[me], **sems)
            recv = pltpu.make_async_remote_copy(src_ref=mine, dst_ref=lands[w].at[peer_slot], **sems) if receives else None
            pairs.append((send, recv))
    return pairs


def _exchange_start(name, srcs, stacked):
    n = len(srcs)
    land_shapes = [s.shape if st else (N_DEV,) + s.shape for s, st in zip(srcs, stacked)]

    def body(*refs):
        src, land = refs[:n], refs[n:2 * n]
        send_sems, recv_sems = refs[2 * n], refs[2 * n + 1]
        token = refs[-1]
        for send, _ in _exchange_copies(src, land, send_sems, recv_sems, stacked, receives=False):
            send.start()
        token[...] = jnp.zeros_like(token)

    hbm = lambda a: pltpu.with_memory_space_constraint(a, pltpu.HBM)
    outs = pl.pallas_call(
        body, name=name,
        out_shape=(pltpu.SemaphoreType.DMA((n * (N_DEV - 1),)), pltpu.SemaphoreType.DMA((n * (N_DEV - 1),)),
                   *[pltpu.HBM(s.shape, s.dtype) for s in srcs],
                   *[pltpu.HBM(ls, s.dtype) for ls, s in zip(land_shapes, srcs)],
                   _sds((8, LANES), F32)),
        in_specs=[_HBM] * (2 * n),
        out_specs=(_SEM, _SEM, *[_HBM] * (2 * n), pl.BlockSpec(memory_space=pltpu.VMEM)),
        input_output_aliases={i: 2 + i for i in range(2 * n)},
        compiler_params=pltpu.CompilerParams(has_side_effects=_EFFECT),
    )(*[hbm(s) for s in srcs], *[hbm(lax.empty(ls, s.dtype)) for ls, s in zip(land_shapes, srcs)])
    return outs[0], outs[1], list(outs[2:2 + n]), list(outs[2 + n:2 + 2 * n]), outs[-1][0, 0]


def _exchange_wait(name, started, stacked, after):
    srcs, lands = _wait_call(name, started, stacked, after)
    me = _my_slot()
    full = []
    for src, land, st in zip(srcs, lands, stacked):
        own = lax.dynamic_index_in_dim(src, me, 0, keepdims=True) if st else src[None]
        full.append(lax.dynamic_update_index_in_dim(land, own, me, 0))
    return full


def _wait_call(name, started, stacked, after):
    send_sems, recv_sems, srcs, lands, _ = started
    n = len(srcs)

    def body(*refs):
        src, land = refs[:n], refs[n:2 * n]
        s_sems, r_sems = refs[2 * n], refs[2 * n + 1]
        for send, recv in _exchange_copies(src, land, s_sems, r_sems, stacked):
            send.wait_send()
            recv.wait_recv()

    outs = pl.pallas_call(
        body, name=name,
        out_shape=tuple(pltpu.HBM(a.shape, a.dtype) for a in srcs + lands),
        in_specs=[_HBM] * (2 * n) + [_SEM, _SEM, pl.BlockSpec(memory_space=pl.ANY)],
        out_specs=[_HBM] * (2 * n),
        input_output_aliases={i: i for i in range(2 * n)},
        compiler_params=pltpu.CompilerParams(has_side_effects=_EFFECT),
    )(*srcs, *lands, send_sems, recv_sems, after)
    return outs[:n], outs[n:]


def _adamw(name, parts, w, m, v):
    rows, cols = w.shape
    cap = max(8, (LANES * 1024) // cols)
    tr = rows
    if rows > cap:
        tr = (cap // 8) * 8
        while rows % tr:
            tr -= 8
    c1 = 1.0 / (1.0 - ADAM_B1 ** ADAM_STEP)
    c2 = 1.0 / (1.0 - ADAM_B2 ** ADAM_STEP)

    def body(p_ref, w_ref, m_ref, v_ref, g_o, d_o, m_o, v_o):
        g = p_ref[0]
        for s in range(1, N_DEV):
            g = g + p_ref[s]
        mn = ADAM_B1 * m_ref[...] + (1.0 - ADAM_B1) * g
        vn = ADAM_B2 * v_ref[...] + (1.0 - ADAM_B2) * (g * g)
        g_o[...] = g
        m_o[...] = mn
        v_o[...] = vn
        d_o[...] = -ADAM_LR * ((mn * c1) / (jnp.sqrt(vn * c2) + ADAM_EPS) + ADAM_WD * w_ref[...])

    blk = pl.BlockSpec((tr, cols), lambda i: (i, 0))
    return _call(body, name=name, grid=(rows // tr,),
                 in_specs=[pl.BlockSpec((N_DEV, tr, cols), lambda i: (0, i, 0)), blk, blk, blk],
                 out_specs=[blk] * 4, out_shape=[_sds((rows, cols), F32)] * 4)(parts, w, m, v)


def _rope_tables(pos_col, inv_freq):
    T = pos_col.shape[0]
    tm = _tile(T, 1024)

    def body(p_ref, f_ref, c_o, sa_o, sb_o):
        ang = p_ref[...].astype(F32) * f_ref[...]
        lane = lax.broadcasted_iota(jnp.int32, ang.shape, 1)
        cs, sn = jnp.cos(ang), jnp.sin(ang)
        c_o[...] = jnp.where(lane < D_ROPE, cs, 0.0)
        sa_o[...] = jnp.where(lane < D_ROPE // 2, -sn, 0.0)
        sb_o[...] = jnp.where((lane >= D_ROPE // 2) & (lane < D_ROPE), sn, 0.0)

    blk = pl.BlockSpec((tm, LANES), lambda i: (i, 0))
    return _call(body, name="rope_tables", grid=(T // tm,),
                 in_specs=[pl.BlockSpec((tm, 1), lambda i: (i, 0)), pl.BlockSpec((1, LANES), lambda i: (0, 0))],
                 out_specs=[blk] * 3, out_shape=[_sds((T, LANES), F32)] * 3)(pos_col, inv_freq)


def _ln_in(x, g, b):
    T, D = x.shape
    tm = _tile(T, 512)

    def body(x_ref, g_ref, b_ref, o32, o16):
        xhat, _ = _ln_stats(x_ref[...])
        y = xhat * g_ref[...] + b_ref[...]
        o32[...] = y
        o16[...] = y.astype(BF16)

    blk = pl.BlockSpec((tm, D), lambda i: (i, 0))
    vec = pl.BlockSpec((1, D), lambda i: (0, 0))
    return _call(body, name="ln_in", grid=(T // tm,), in_specs=[blk, vec, vec], out_specs=[blk, blk],
                 out_shape=[_sds((T, D), F32), _sds((T, D), BF16)])(x, g, b)


def _mid(h, g_cq, g_ckv, tabs, C):
    T = h.shape[0]
    tm = _tile(T, 256)
    cq_blk, kr_blk = (2 * C) // LORA, (2 * C + 2 * LORA) // LANES

    def body(a_ref, gt_ref, cq_ref, ckv_ref, kr_ref, gq_ref, gkv_ref, cp, sa, sb, u_o, cqn_o, ckvn_o, kr_o):
        u_o[...] = a_ref[...] * _sigmoid(gt_ref[...])
        cqn_o[...] = _rms(cq_ref[...], gq_ref[...]).astype(BF16)
        ckvn_o[...] = _rms(ckv_ref[...], gkv_ref[...]).astype(BF16)
        kr_o[...] = _rope(kr_ref[...], cp[...], sa[...], sb[...]).astype(BF16)

    def col(w, j):
        return pl.BlockSpec((tm, w), lambda i: (i, j))

    vec = pl.BlockSpec((1, LORA), lambda i: (0, 0))
    return _call(body, name="mid_norm_glu", grid=(T // tm,),
                 in_specs=[col(C, 0), col(C, 1), col(LORA, cq_blk), col(LORA, cq_blk + 1), col(LANES, kr_blk),
                           vec, vec, col(LANES, 0), col(LANES, 0), col(LANES, 0)],
                 out_specs=[col(C, 0), col(LORA, 0), col(LORA, 0), col(LANES, 0)],
                 out_shape=[_sds((T, C), F32), _sds((T, LORA), BF16), _sds((T, LORA), BF16), _sds((T, LANES), BF16)],
                 )(h, h, h, h, h, g_cq, g_ckv, *tabs)


def _q_proj(cqn, w_uq_p, tabs):
    T = cqn.shape[0]
    tm = _tile(T, HEAD_ROWS)

    def body(c_ref, w_ref, cp, sa, sb, o_ref):
        q = jnp.dot(c_ref[...], w_ref[...], preferred_element_type=F32)
        o_ref[:, :D_NOPE] = (q[:, :D_NOPE] * Q_SCALE).astype(BF16)
        o_ref[:, D_NOPE:] = (_rope(q[:, D_NOPE:], cp[...], sa[...], sb[...]) * Q_SCALE).astype(BF16)

    tab = pl.BlockSpec((tm, LANES), lambda i, h: (i, 0))
    return _call(body, name="q_proj_rope", grid=(T // tm, N_HEADS),
                 in_specs=[pl.BlockSpec((tm, LORA), lambda i, h: (i, 0)),
                           pl.BlockSpec((None, LORA, D_HEAD_PAD), lambda i, h: (h, 0, 0)), tab, tab, tab],
                 out_specs=pl.BlockSpec((None, tm, D_HEAD_PAD), lambda i, h: (h, i, 0)),
                 out_shape=_sds((N_HEADS, T, D_HEAD_PAD), BF16))(cqn, w_uq_p, *tabs)


def _kv_proj(ckvn, w_ukv, kr):
    T = ckvn.shape[0]
    tm = _tile(T, HEAD_ROWS)

    def body(c_ref, w_ref, kr_ref, k_o, v_o):
        kv = jnp.dot(c_ref[...], w_ref[...], preferred_element_type=F32)
        k_o[:, :D_NOPE] = kv[:, :D_NOPE].astype(BF16)
        k_o[:, D_NOPE:] = kr_ref[...]
        v_o[:, :D_V] = kv[:, D_NOPE:].astype(BF16)
        v_o[:, D_V:] = jnp.ones((tm, D_V), BF16)

    return _call(body, name="kv_proj", grid=(T // tm, N_HEADS),
                 in_specs=[pl.BlockSpec((tm, LORA), lambda i, h: (i, 0)),
                           pl.BlockSpec((None, LORA, D_NOPE + D_V), lambda i, h: (h, 0, 0)),
                           pl.BlockSpec((tm, LANES), lambda i, h: (i, 0))],
                 out_specs=[pl.BlockSpec((None, tm, D_HEAD_PAD), lambda i, h: (h, i, 0)),
                            pl.BlockSpec((None, tm, 2 * D_V), lambda i, h: (h, i, 0))],
                 out_shape=[_sds((N_HEADS, T, D_HEAD_PAD), BF16), _sds((N_HEADS, T, 2 * D_V), BF16)])(ckvn, w_ukv, kr)


def _flash_fwd(q, k, v1, out_cols):
    _, T, _ = q.shape
    tq, tk = _tile(T, FLASH_TQ), _tile(T, FLASH_TK)
    nkv, reps = T // tk, tk // LANES

    def body(q_ref, k_ref, v_ref, o_ref, lse_ref, m_sc, acc_sc):
        m_sc[...] = jnp.full_like(m_sc, -jnp.inf)
        acc_sc[...] = jnp.zeros_like(acc_sc)
        qv = q_ref[...]

        def rows(j):
            return pl.ds(pl.multiple_of(j * tk, tk), tk)

        def scores(j):
            return lax.dot_general(qv, k_ref[rows(j), :], NT, preferred_element_type=F32)

        def update(s, j):
            m_prev = m_sc[...]
            m_new = jnp.maximum(m_prev, jnp.max(s, axis=1, keepdims=True))
            a = jnp.exp2(m_prev - m_new)
            p = jnp.exp2(s - jnp.tile(m_new, (1, reps)))
            pv = jnp.dot(p.astype(BF16), v_ref[rows(j), :], preferred_element_type=F32)
            acc_sc[...] = jnp.tile(a, (1, 2)) * acc_sc[...] + pv
            m_sc[...] = m_new

        def step(j, carry):
            update(scores(j), j)
            return carry

        lax.fori_loop(0, nkv, step, 0, unroll=FLASH_UNROLL if nkv % FLASH_UNROLL == 0 else 1)
        acc = acc_sc[...]
        l = acc[:, D_V:]
        o_ref[...] = (acc[:, :D_V] / l).astype(BF16)
        lse_ref[...] = m_sc[...] + jnp.log(l) * LOG2_E

    return _call(body, name="flash_fwd", grid=(N_HEADS, T // tq),
                 in_specs=[pl.BlockSpec((None, tq, D_HEAD_PAD), lambda h, i: (h, i, 0)),
                           pl.BlockSpec((None, T, D_HEAD_PAD), lambda h, i: (h, 0, 0)),
                           pl.BlockSpec((None, T, 2 * D_V), lambda h, i: (h, 0, 0))],
                 out_specs=[pl.BlockSpec((tq, D_V), lambda h, i: (i, h)),
                            pl.BlockSpec((None, tq, LANES), lambda h, i: (h, i, 0))],
                 out_shape=[_sds((T, out_cols), BF16), _sds((N_HEADS, T, LANES), F32)],
                 scratch=[pltpu.VMEM((tq, LANES), F32), pltpu.VMEM((tq, 2 * D_V), F32)])(q, k, v1)


def _halo_specs(tm, cb, n_t):
    r = tm // HALO
    return [pl.BlockSpec((HALO, cb), lambda jc, i: (jnp.maximum(i * r - 1, 0), jc)),
            pl.BlockSpec((tm, cb), lambda jc, i: (i, jc)),
            pl.BlockSpec((HALO, cb), lambda jc, i: (jnp.minimum((i + 1) * r, n_t * r - 1), jc))]


def _fill_ext(ext, prev_ref, cur_ref, next_ref, i, n_t, tm):
    ext[0:HALO, :] = jnp.where(i > 0, prev_ref[...], 0.0)
    ext[HALO:HALO + tm, :] = cur_ref[...]
    ext[HALO + tm:, :] = jnp.where(i < n_t - 1, next_ref[...], 0.0)


HEAD_ROWS = 2048
FLASH_TQ = 512
FLASH_TK = 512
FLASH_UNROLL = 8
EPILOGUE_ROWS = 128
CONV_ROWS = 64


def _conv_fwd(u, w_pad, bias):
    T, C = u.shape
    tm, cb = _tile(T, 256), _tile(C, 256)
    n_t = T // tm
    rb = min(CONV_ROWS, tm)

    def body(up, uc, un, w_ref, b_ref, c_o, ext):
        i = pl.program_id(1)
        _fill_ext(ext, up, uc, un, i, n_t, tm)
        for r0 in range(0, tm, rb):
            acc = jnp.zeros((rb, cb), F32) + b_ref[...]
            for k in range(CONV_W):
                acc = acc + w_ref[k:k + 1, :] * ext[r0 + k + 1:r0 + k + 1 + rb, :]
            c_o[r0:r0 + rb, :] = acc

    return _call(body, name="conv_fwd", grid=(C // cb, n_t),
                 in_specs=_halo_specs(tm, cb, n_t) + [pl.BlockSpec((CONV_W_PAD, cb), lambda jc, i: (0, jc)),
                                                      pl.BlockSpec((1, cb), lambda jc, i: (0, jc))],
                 out_specs=pl.BlockSpec((tm, cb), lambda jc, i: (i, jc)),
                 out_shape=_sds((T, C), F32),
                 scratch=[pltpu.VMEM((tm + 2 * HALO, cb), F32)])(u, u, u, w_pad, bias)


def _conv_post(c, g, b, cat):
    T, C = c.shape
    assert cat.shape == (T, 2 * C)
    tm = _tile(T, 512)

    def body(c_ref, g_ref, b_ref, cat_ref, o_ref):
        xhat, _ = _ln_stats(c_ref[...])
        y = xhat * g_ref[...] + b_ref[...]
        o_ref[...] = (y * _sigmoid(y)).astype(BF16)

    blk = pl.BlockSpec((tm, C), lambda i: (i, 0))
    vec = pl.BlockSpec((1, C), lambda i: (0, 0))
    return _call(body, name="conv_ln_silu", grid=(T // tm,),
                 in_specs=[blk, vec, vec, pl.BlockSpec(memory_space=pl.ANY)],
                 out_specs=pl.BlockSpec((tm, C), lambda i: (i, 1)),
                 out_shape=_sds(cat.shape, BF16), aliases={3: 0})(c, g, b, cat)


def _conv_post_bwd(dcat, c, g, b):
    T, C = c.shape
    tm = _tile(T, 512)

    def body(d_ref, c_ref, g_ref, b_ref, dc_o, dg_o, db_o):
        first = pl.program_id(0) == 0
        xhat, rstd = _ln_stats(c_ref[...])
        y = xhat * g_ref[...] + b_ref[...]
        sg = _sigmoid(y)
        dy = d_ref[...] * (sg * (1.0 + y * (1.0 - sg)))
        _acc_out(dg_o, _colsum(dy * xhat), first)
        _acc_out(db_o, _colsum(dy), first)
        dc_o[...] = _ln_bwd(dy, xhat, rstd, g_ref[...])

    blk = pl.BlockSpec((tm, C), lambda i: (i, 0))
    vec = pl.BlockSpec((1, C), lambda i: (0, 0))
    return _call(body, name="conv_ln_silu_bwd", grid=(T // tm,),
                 in_specs=[pl.BlockSpec((tm, C), lambda i: (i, 1)), blk, vec, vec],
                 out_specs=[blk, vec, vec],
                 out_shape=[_sds((T, C), F32), _sds((1, C), F32), _sds((1, C), F32)])(dcat, c, g, b)


def _conv_bwd(dc, u, w_pad):
    T, C = u.shape
    tm, cb = _tile(T, 256), _tile(C, 256)
    n_t = T // tm
    rb = min(CONV_ROWS, tm)

    def body(dp, dcur, dn, up, uc, un, w_ref, du_o, dw_o, db_o, dext, uext, dw_sc):
        i = pl.program_id(1)
        _fill_ext(dext, dp, dcur, dn, i, n_t, tm)
        _fill_ext(uext, up, uc, un, i, n_t, tm)

        @pl.when(i == 0)
        def _():
            dw_sc[...] = jnp.zeros_like(dw_sc)

        for r0 in range(0, tm, rb):
            acc = jnp.zeros((rb, cb), F32)
            d_here = dcur[r0:r0 + rb, :]
            for k in range(CONV_W):
                acc = acc + w_ref[k:k + 1, :] * dext[r0 + 2 * HALO - 1 - k:r0 + 2 * HALO - 1 - k + rb, :]
                prod = d_here * uext[r0 + k + 1:r0 + k + 1 + rb, :]
                dw_sc[k] += jnp.sum(prod.reshape(rb // 8, 8, cb), axis=0)
            du_o[r0:r0 + rb, :] = acc
        dw_sc[CONV_W] += jnp.sum(dcur[...].reshape(tm // 8, 8, cb), axis=0)

        @pl.when(i == n_t - 1)
        def _():
            red = jnp.sum(dw_sc[...], axis=1)
            row = lax.broadcasted_iota(jnp.int32, red.shape, 0)
            dw_o[...] = jnp.where(row < CONV_W, red, 0.0)
            db_o[...] = jnp.sum(jnp.where(row == CONV_W, red, 0.0), axis=0, keepdims=True)

    return _call(body, name="conv_bwd", grid=(C // cb, n_t),
                 in_specs=_halo_specs(tm, cb, n_t) + _halo_specs(tm, cb, n_t)
                 + [pl.BlockSpec((CONV_W_PAD, cb), lambda jc, i: (0, jc))],
                 out_specs=[pl.BlockSpec((tm, cb), lambda jc, i: (i, jc)),
                            pl.BlockSpec((CONV_W_PAD, cb), lambda jc, i: (0, jc)),
                            pl.BlockSpec((1, cb), lambda jc, i: (0, jc))],
                 out_shape=[_sds((T, C), F32), _sds((CONV_W_PAD, C), F32), _sds((1, C), F32)],
                 scratch=[pltpu.VMEM((tm + 2 * HALO, cb), F32), pltpu.VMEM((tm + 2 * HALO, cb), F32),
                          pltpu.VMEM((CONV_W_PAD, 8, cb), F32)])(dc, dc, dc, u, u, u, w_pad)


def _glu_bwd(du, h, C):
    T = du.shape[0]
    tm = _tile(T, 512)

    def body(du_ref, a_ref, gt_ref, o_ref):
        sg = _sigmoid(gt_ref[...])
        du_v = du_ref[...]
        o_ref[:, :C] = (du_v * sg).astype(BF16)
        o_ref[:, C:] = (du_v * a_ref[...] * sg * (1.0 - sg)).astype(BF16)

    return _call(body, name="glu_bwd", grid=(T // tm,),
                 in_specs=[pl.BlockSpec((tm, C), lambda i: (i, 0)), pl.BlockSpec((tm, C), lambda i: (i, 0)),
                           pl.BlockSpec((tm, C), lambda i: (i, 1))],
                 out_specs=pl.BlockSpec((tm, 2 * C), lambda i: (i, 0)),
                 out_shape=_sds(h.shape, BF16))(du, h, h)


def _attn_delta(dcat, attn):
    T = attn.shape[0]
    tm = _tile(T, HEAD_ROWS)

    def body(d_ref, o_ref, dl_o, dob_o):
        d = d_ref[...]
        dl = jnp.sum(d * o_ref[...].astype(F32), axis=1, keepdims=True)
        dl_o[...] = jnp.broadcast_to(dl, (tm, LANES))
        dob_o[...] = d.astype(BF16)

    blk = pl.BlockSpec((tm, D_V), lambda i, h: (i, h))
    hblk = pl.BlockSpec((None, tm, D_V), lambda i, h: (h, i, 0))
    return _call(body, name="attn_delta", grid=(T // tm, N_HEADS), in_specs=[blk, blk], out_specs=[hblk, hblk],
                 out_shape=[_sds((N_HEADS, T, LANES), F32), _sds((N_HEADS, T, D_V), BF16)])(dcat, attn)


def _flash_bwd(q, k, v1, do, lse, delta):
    _, T, _ = q.shape
    tq, tk = _tile(T, FLASH_TQ), _tile(T, FLASH_TK)
    nkv, reps = T // tk, tk // LANES

    def body(q_ref, do_ref, lse_ref, dl_ref, k_ref, v_ref, dq_o, dk_o, dv_o, dq_sc):
        @pl.when(pl.program_id(1) == 0)
        def _():
            dk_o[...] = jnp.zeros_like(dk_o)
            dv_o[...] = jnp.zeros_like(dv_o)

        qv, dov = q_ref[...], do_ref[...]
        lse_t = jnp.tile(lse_ref[...], (1, reps))
        dl_t = jnp.tile(dl_ref[...], (1, reps))
        dq_sc[...] = jnp.zeros_like(dq_sc)

        def rows(j):
            return pl.ds(pl.multiple_of(j * tk, tk), tk)

        def scores(j):
            s = lax.dot_general(qv, k_ref[rows(j), :], NT, preferred_element_type=F32)
            dp = lax.dot_general(dov, v_ref[rows(j), :D_V], NT, preferred_element_type=F32)
            return s, dp

        def update(s_dp, j):
            s, dp = s_dp
            p = jnp.exp2(s - lse_t)
            ds = (p * (dp - dl_t)).astype(BF16)
            dv_o[rows(j), :] += lax.dot_general(p.astype(BF16), dov, TN, preferred_element_type=F32)
            dk_o[rows(j), :] += lax.dot_general(ds, qv, TN, preferred_element_type=F32)
            dq_sc[...] += jnp.dot(ds, k_ref[rows(j), :], preferred_element_type=F32)

        def step(j, carry):
            update(scores(j), j)
            return carry

        lax.fori_loop(0, nkv, step, 0, unroll=FLASH_UNROLL if nkv % FLASH_UNROLL == 0 else 1)
        dq_o[...] = dq_sc[...]

    def tile(w):
        return pl.BlockSpec((None, tq, w), lambda h, i: (h, i, 0))

    def whole(w):
        return pl.BlockSpec((None, T, w), lambda h, i: (h, 0, 0))

    return _call(body, name="flash_bwd", grid=(N_HEADS, T // tq),
                 in_specs=[tile(D_HEAD_PAD), tile(D_V), tile(LANES), tile(LANES), whole(D_HEAD_PAD), whole(2 * D_V)],
                 out_specs=[tile(D_HEAD_PAD), whole(D_HEAD_PAD), whole(D_V)],
                 out_shape=[_sds((N_HEADS, T, D_HEAD_PAD), F32), _sds((N_HEADS, T, D_HEAD_PAD), F32),
                            _sds((N_HEADS, T, D_V), F32)],
                 scratch=[pltpu.VMEM((tq, D_HEAD_PAD), F32)])(q, do, lse, delta, k, v1)


def _dq_post(dq, tabs):
    _, T, _ = dq.shape
    tm = _tile(T, HEAD_ROWS)

    def body(d_ref, cp, sa, sb, o_ref):
        d = d_ref[...] * SCALE
        o_ref[:, :D_NOPE] = d[:, :D_NOPE].astype(BF16)
        o_ref[:, D_NOPE:] = _unrope(d[:, D_NOPE:], cp[...], sa[...], sb[...]).astype(BF16)

    blk = pl.BlockSpec((None, tm, D_HEAD_PAD), lambda i, h: (h, i, 0))
    tab = pl.BlockSpec((tm, LANES), lambda i, h: (i, 0))
    return _call(body, name="dq_unrope", grid=(T // tm, N_HEADS), in_specs=[blk, tab, tab, tab], out_specs=blk,
                 out_shape=_sds(dq.shape, BF16))(dq, *tabs)


def _dk_post(dk, dv, tabs, dh, kr_blk):
    _, T, _ = dk.shape
    tm = _tile(T, 1024)

    def body(dk_ref, dv_ref, cp, sa, sb, dh_ref, dkv_o, dkr_o, sc):
        h = pl.program_id(1)
        d = dk_ref[...] * LN_2
        dkv_o[:, :D_NOPE] = d[:, :D_NOPE].astype(BF16)
        dkv_o[:, D_NOPE:] = dv_ref[...].astype(BF16)

        @pl.when(h == 0)
        def _():
            sc[...] = d[:, D_NOPE:]

        @pl.when(h > 0)
        def _():
            sc[...] += d[:, D_NOPE:]

        @pl.when(h == N_HEADS - 1)
        def _():
            dkr_o[...] = _unrope(sc[...], cp[...], sa[...], sb[...]).astype(BF16)

    tab = pl.BlockSpec((tm, LANES), lambda i, h: (i, 0))
    return _call(body, name="dk_unrope", grid=(T // tm, N_HEADS),
                 in_specs=[pl.BlockSpec((None, tm, D_HEAD_PAD), lambda i, h: (h, i, 0)),
                           pl.BlockSpec((None, tm, D_V), lambda i, h: (h, i, 0)), tab, tab, tab,
                           pl.BlockSpec(memory_space=pl.ANY)],
                 out_specs=[pl.BlockSpec((None, tm, D_HEAD_PAD), lambda i, h: (h, i, 0)),
                            pl.BlockSpec((tm, LANES), lambda i, h: (i, kr_blk))],
                 out_shape=[_sds((N_HEADS, T, D_HEAD_PAD), BF16), _sds(dh.shape, BF16)],
                 scratch=[pltpu.VMEM((tm, LANES), F32)], aliases={5: 1})(dk, dv, *tabs, dh)


def _latent_bwd(name, dproj, w_heads, h, col_blk, g, dh):
    _, T, _ = dproj.shape
    tm = _tile(T, 1024)

    def ep(acc, ex, out, first):
        dx, dg = _rms_bwd(acc, ex[0][...], ex[1][...])
        out[0][...] = dx.astype(BF16)
        _acc_out(out[1], dg, first)

    return _matmul(name, dproj, w_heads, dims=NT, grid=(T // tm, 1, N_HEADS),
                   a_spec=pl.BlockSpec((None, tm, D_HEAD_PAD), lambda i, j, k: (k, i, 0)),
                   b_spec=pl.BlockSpec((None, LORA, D_HEAD_PAD), lambda i, j, k: (k, 0, 0)),
                   acc_shape=(tm, LORA),
                   extras=[(h, pl.BlockSpec((tm, LORA), lambda i, j, k: (i, col_blk))),
                           (g, pl.BlockSpec((1, LORA), lambda i, j, k: (0, 0)))],
                   outs=[(_sds(dh.shape, BF16), pl.BlockSpec((tm, LORA), lambda i, j, k: (i, col_blk))),
                         (_sds((1, LORA), F32), pl.BlockSpec((1, LORA), lambda i, j, k: (0, 0)))],
                   epilogue=ep, into=dh)


def _head_weight_grad(name, latent, dproj):
    _, T, _ = dproj.shape
    tk = _tile(T, 2048)
    return _matmul(name, latent, dproj, dims=TN, grid=(N_HEADS, 1, T // tk),
                   a_spec=pl.BlockSpec((tk, LORA), lambda i, j, k: (k, 0)),
                   b_spec=pl.BlockSpec((None, tk, D_HEAD_PAD), lambda i, j, k: (i, k, 0)),
                   acc_shape=(LORA, D_HEAD_PAD),
                   outs=[(_sds((N_HEADS, LORA, D_HEAD_PAD), F32),
                          pl.BlockSpec((None, LORA, D_HEAD_PAD), lambda i, j, k: (i, 0, 0)))],
                   epilogue=_store())[0]


def _weight_grad(name, a, b, tm_pref=1024, tn_pref=1024, stacked_cols=None):
    T, M = a.shape
    N = b.shape[1]
    tk = _tile(T, 2048)
    tm = _tile(M, tm_pref)
    if stacked_cols is None:
        tn = _tile(N, tn_pref)
        out = (_sds((M, N), F32), pl.BlockSpec((tm, tn), lambda i, j, k: (i, j)))
    else:
        tn = _tile(stacked_cols, tn_pref)
        per = stacked_cols // tn
        out = (_sds((N // stacked_cols, M, stacked_cols), F32),
               pl.BlockSpec((None, tm, tn), lambda i, j, k: (j // per, i, j % per)))
    return _matmul(name, a, b, dims=TN, grid=(M // tm, N // tn, T // tk),
                   a_spec=pl.BlockSpec((tk, tm), lambda i, j, k: (k, i)),
                   b_spec=pl.BlockSpec((tk, tn), lambda i, j, k: (k, j)),
                   acc_shape=(tm, tn), outs=[out], epilogue=_store())[0]


def _small_names():
    return ["ln_in_g", "ln_in_b", "g_cq", "g_ckv", "conv_b", "g_conv_ln", "b_conv_ln", "g_ln1", "b_ln1", "g_ln2", "b_ln2"]


def _pack(vecs):
    flat = jnp.concatenate([v.reshape(-1) for v in vecs])
    assert flat.shape[0] % (8 * LANES) == 0
    return flat.reshape(-1, LANES)


def _unpack(packed, like):
    flat, out, off = packed.reshape(-1), [], 0
    for v in like:
        out.append(flat[off:off + v.size].reshape(v.shape))
        off += v.size
    return out


def kernel(x, positions, ln_in_g, ln_in_b, w_in, g_cq, w_uq, g_ckv, w_uk, w_uv, conv_w, conv_b, g_conv_ln, b_conv_ln, w_out, g_ln1, b_ln1, w_ff1, w_ff2, g_ln2, b_ln2, loss_target, m_ln_in_g, m_ln_in_b, m_w_in, m_g_cq, m_w_uq, m_g_ckv, m_w_uk, m_w_uv, m_conv_w, m_conv_b, m_g_conv_ln, m_b_conv_ln, m_w_out, m_g_ln1, m_b_ln1, m_w_ff1, m_w_ff2, m_g_ln2, m_b_ln2, v_ln_in_g, v_ln_in_b, v_w_in, v_g_cq, v_w_uq, v_g_ckv, v_w_uk, v_w_uv, v_conv_w, v_conv_b, v_g_conv_ln, v_b_conv_ln, v_w_out, v_g_ln1, v_b_ln1, v_w_ff1, v_w_ff2, v_g_ln2, v_b_ln2):
    args = dict(locals())
    T, D = x.shape[1], x.shape[2]
    C = D - N_HEADS * D_V
    Fs = w_ff1.shape[2]
    F = N_DEV * Fs
    n_in = N_DEV * w_in.shape[2]
    n_in_p = 2 * C + 2 * LORA + LANES
    assert n_in == 2 * LORA + D_ROPE + 2 * C and w_uq.shape[2] == D_QK and conv_w.shape[2] * N_DEV == C

    xs, tgt = x[0], loss_target[0]
    row = lambda v_: v_.reshape(1, -1)

    gather = lambda k_: [False] * k_
    ag_in = _exchange_start("ag_in_start", [w_in[0].astype(BF16)], gather(1))
    ag_heads = _exchange_start("ag_heads_start", [w_uq[0].astype(BF16), w_uk[0].astype(BF16), w_uv[0].astype(BF16),
                                                  conv_w[0]], gather(4))
    ag_ff = _exchange_start("ag_ff_start", [w_out[0].astype(BF16), w_ff1[0].astype(BF16), w_ff2[0].astype(BF16)],
                            gather(3))
    started = ag_in[4] + ag_heads[4] + ag_ff[4]

    half = D_ROPE // 2
    inv_freq = ROPE_BASE ** (-jnp.arange(half, dtype=F32) * (2.0 / D_ROPE))
    inv_freq = jnp.tile(inv_freq, LANES // half).reshape(1, LANES)
    tabs = _rope_tables(positions.reshape(T, 1), inv_freq)

    x0, x0b = _ln_in(xs, row(ln_in_g) + started, row(ln_in_b))

    (g_w_in,) = _exchange_wait("ag_in_wait", ag_in, gather(1), after=x0b)
    w_in_f = jnp.transpose(g_w_in, (1, 0, 2)).reshape(D, n_in)
    s_cq, s_ckv, s_kr, s_a, s_g = 0, LORA, 2 * LORA, 2 * LORA + D_ROPE, 2 * LORA + D_ROPE + C
    w_in_p = jnp.concatenate([w_in_f[:, s_a:s_g], w_in_f[:, s_g:], w_in_f[:, s_cq:s_ckv], w_in_f[:, s_ckv:s_kr],
                              w_in_f[:, s_kr:s_a], jnp.zeros((D, LANES - D_ROPE), BF16)], axis=1)

    tm, tn = _tile(T, 1024), _tile(n_in_p, 640)
    h = _matmul("h_proj", x0b, w_in_p, dims=NN, grid=(T // tm, n_in_p // tn, 1),
                a_spec=pl.BlockSpec((tm, D), lambda i, j, k: (i, 0)),
                b_spec=pl.BlockSpec((D, tn), lambda i, j, k: (0, j)), acc_shape=(tm, tn),
                outs=[(_sds((T, n_in_p), F32), pl.BlockSpec((tm, tn), lambda i, j, k: (i, j)))],
                epilogue=_store())[0]

    u, cqn, ckvn, kr = _mid(h, g_cq, g_ckv, tabs, C)
    g_w_uq, g_w_uk, g_w_uv, g_conv_w = _exchange_wait("ag_heads_wait", ag_heads, gather(4), after=cqn)
    w_uq_p = jnp.pad(g_w_uq, ((0, 0), (0, 0), (0, D_HEAD_PAD - D_QK)))
    w_ukv = jnp.concatenate([g_w_uk, g_w_uv], axis=2)
    conv_w_f = jnp.pad(jnp.transpose(g_conv_w, (1, 0, 2)).reshape(CONV_W, C), ((0, CONV_W_PAD - CONV_W), (0, 0)))
    q = _q_proj(cqn, w_uq_p, tabs)
    kf, vf = _kv_proj(ckvn, w_ukv, kr)
    attn, lse = _flash_fwd(q, kf, vf, D)
    conv_c = _conv_fwd(u, conv_w_f, conv_b)
    cat = _conv_post(conv_c, g_conv_ln, b_conv_ln, attn)
    g_w_out, g_w_ff1, g_w_ff2 = _exchange_wait("ag_ff_wait", ag_ff, gather(3), after=cat)
    w_out_f = g_w_out.reshape(D, D)
    w_ff2_f = g_w_ff2.reshape(F, D)

    def ep_ln1(acc, ex, out, first):
        z1 = ALPHA * ex[0][...] + acc
        xhat, _ = _ln_stats(z1)
        x1 = xhat * ex[1][...] + ex[2][...]
        out[0][...] = z1
        out[1][...] = x1
        out[2][...] = x1.astype(BF16)

    tm = _tile(T, 256)
    rowblk = pl.BlockSpec((tm, D), lambda i, j, k: (i, 0))
    vecD = pl.BlockSpec((1, D), lambda i, j, k: (0, 0))
    z1, x1, x1b = _matmul("mix_ln1", cat, w_out_f, dims=NN, grid=(T // tm, 1, 1), a_spec=rowblk,
                          b_spec=pl.BlockSpec((D, D), lambda i, j, k: (0, 0)), acc_shape=(tm, D),
                          extras=[(x0, rowblk), (g_ln1, vecD), (b_ln1, vecD)],
                          outs=[(_sds((T, D), F32), rowblk), (_sds((T, D), F32), rowblk), (_sds((T, D), BF16), rowblk)],
                          epilogue=ep_ln1, ep_rows=EPILOGUE_ROWS)

    def ep_ff1(acc, ex, out, first):
        r = jnp.maximum(acc, 0.0)
        out[0][...] = (r * r).astype(BF16)
        out[1][...] = r.astype(BF16)

    tm, tn = _tile(T, 1024), _tile(Fs, 1024)
    per = Fs // tn
    fblk = pl.BlockSpec((tm, tn), lambda i, j, k: (i, j))
    f_act, r_act = _matmul("ff1_relu2", x1b, g_w_ff1, dims=NN, grid=(T // tm, F // tn, 1),
                           a_spec=pl.BlockSpec((tm, D), lambda i, j, k: (i, 0)),
                           b_spec=pl.BlockSpec((None, D, tn), lambda i, j, k: (j // per, 0, j % per)),
                           acc_shape=(tm, tn), outs=[(_sds((T, F), BF16), fblk), (_sds((T, F), BF16), fblk)],
                           epilogue=ep_ff1)

    def ep_ln2(acc, ex, out, first):
        g2 = ex[2][...]
        z2 = ALPHA * ex[0][...] + acc
        xhat, rstd = _ln_stats(z2)
        err = xhat * g2 + ex[3][...] - ex[1][...]
        part = 0.5 * jnp.sum(jnp.mean(err * err, axis=-1, keepdims=True))
        _acc_out(out[2], jnp.zeros((8, LANES), F32) + part, first)
        dy = err * (1.0 / D)
        _acc_out(out[3], _colsum(dy * xhat), first)
        _acc_out(out[4], _colsum(dy), first)
        dz2 = _ln_bwd(dy, xhat, rstd, g2)
        out[0][...] = dz2
        out[1][...] = dz2.astype(BF16)

    tm, tk = _tile(T, 512), _tile(F, 1024)
    rowblk = pl.BlockSpec((tm, D), lambda i, j, k: (i, 0))
    dz2, dz2b, loss_blk, dg_ln2, db_ln2 = _matmul(
        "ff2_ln2_loss", f_act, w_ff2_f, dims=NN, grid=(T // tm, 1, F // tk),
        a_spec=pl.BlockSpec((tm, tk), lambda i, j, k: (i, k)), b_spec=pl.BlockSpec((tk, D), lambda i, j, k: (k, 0)),
        acc_shape=(tm, D), extras=[(x1, rowblk), (tgt, rowblk), (g_ln2, vecD), (b_ln2, vecD)],
        outs=[(_sds((T, D), F32), rowblk), (_sds((T, D), BF16), rowblk),
              (_sds((8, LANES), F32), pl.BlockSpec((8, LANES), lambda i, j, k: (0, 0))),
              (_sds((1, D), F32), vecD), (_sds((1, D), F32), vecD)],
        epilogue=ep_ln2, ep_rows=EPILOGUE_ROWS)
    loss = lax.psum(loss_blk[0, 0], ("x", "y", "c"))

    def ep_dpre(acc, ex, out, first):
        out[0][...] = (acc * (2.0 * ex[0][...].astype(F32))).astype(BF16)

    tm, tn = _tile(T, 1024), _tile(F, 1024)
    fblk = pl.BlockSpec((tm, tn), lambda i, j, k: (i, j))
    dpre = _matmul("ff2_dgrad", dz2b, w_ff2_f, dims=NT, grid=(T // tm, F // tn, 1),
                   a_spec=pl.BlockSpec((tm, D), lambda i, j, k: (i, 0)), b_spec=pl.BlockSpec((tn, D), lambda i, j, k: (j, 0)),
                   acc_shape=(tm, tn), extras=[(r_act, fblk)], outs=[(_sds((T, F), BF16), fblk)], epilogue=ep_dpre)[0]

    dw_ff2 = _weight_grad("ff2_wgrad", f_act, dz2b).reshape(N_DEV, Fs, D)
    dw_ff1 = _weight_grad("ff1_wgrad", x1b, dpre, stacked_cols=Fs)
    scatter = lambda k_: [True] * k_
    rs_ff = _exchange_start("rs_ff_start", [dw_ff2, dw_ff1], scatter(2))

    def ep_ln1_bwd(acc, ex, out, first):
        dx1 = ALPHA * ex[0][...] + acc
        xhat, rstd = _ln_stats(ex[1][...])
        _acc_out(out[2], _colsum(dx1 * xhat), first)
        _acc_out(out[3], _colsum(dx1), first)
        dz1 = _ln_bwd(dx1, xhat, rstd, ex[2][...])
        out[0][...] = dz1
        out[1][...] = dz1.astype(BF16)

    tm, tk = _tile(T, 512), _tile(Fs, 1024)
    per = Fs // tk
    rowblk = pl.BlockSpec((tm, D), lambda i, j, k: (i, 0))
    dz1, dz1b, dg_ln1, db_ln1 = _matmul(
        "ff1_dgrad_ln1_bwd", dpre, g_w_ff1, dims=NT, grid=(T // tm, 1, F // tk),
        a_spec=pl.BlockSpec((tm, tk), lambda i, j, k: (i, k)),
        b_spec=pl.BlockSpec((None, D, tk), lambda i, j, k: (k // per, 0, k % per)),
        acc_shape=(tm, D), extras=[(dz2, rowblk), (z1, rowblk), (g_ln1 + rs_ff[4], vecD)],
        outs=[(_sds((T, D), F32), rowblk), (_sds((T, D), BF16), rowblk), (_sds((1, D), F32), vecD), (_sds((1, D), F32), vecD)],
        epilogue=ep_ln1_bwd, ep_rows=EPILOGUE_ROWS)

    dw_out = _weight_grad("out_wgrad", cat, dz1b).reshape(N_DEV, D // N_DEV, D)
    rs_out = _exchange_start("rs_out_start", [dw_out], scatter(1))
    tm, tn = _tile(T, 1024), _tile(D, 1024)
    dcat = _matmul("out_dgrad", dz1b, w_out_f, dims=NT, grid=(T // tm, D // tn, 1),
                   a_spec=pl.BlockSpec((tm, D), lambda i, j, k: (i, 0)), b_spec=pl.BlockSpec((tn, D), lambda i, j, k: (j, 0)),
                   acc_shape=(tm, tn), outs=[(_sds((T, D), F32), pl.BlockSpec((tm, tn), lambda i, j, k: (i, j)))],
                   epilogue=_store())[0]

    dc, dg_conv_ln, db_conv_ln = _conv_post_bwd(dcat, conv_c, g_conv_ln + rs_out[4], b_conv_ln)
    du, dconv_w_p, dconv_b = _conv_bwd(dc, u, conv_w_f)
    dh = _glu_bwd(du, h, C)

    delta, do_heads = _attn_delta(dcat, cat)
    dq, dk, dv = _flash_bwd(q, kf, vf, do_heads, lse, delta)
    dq_raw = _dq_post(dq, tabs)
    cq_blk = (2 * C) // LORA
    dkv, dh = _dk_post(dk, dv, tabs, dh, (2 * C + 2 * LORA) // LANES)
    dh, dg_cq = _latent_bwd("q_dgrad_rms_bwd", dq_raw, w_uq_p, h, cq_blk, g_cq, dh)
    dh, dg_ckv = _latent_bwd("kv_dgrad_rms_bwd", dkv, w_ukv, h, cq_blk + 1, g_ckv, dh)
    dw_uq = _head_weight_grad("uq_wgrad", cqn, dq_raw)[:, :, :D_QK]
    dw_ukv = _head_weight_grad("ukv_wgrad", ckvn, dkv)
    dw_uk, dw_uv = dw_ukv[:, :, :D_NOPE], dw_ukv[:, :, D_NOPE:]

    dw_in_p = _weight_grad("in_wgrad", x0b, dh, tn_pref=640)
    dw_in_f = jnp.concatenate([dw_in_p[:, 2 * C:2 * C + 2 * LORA + D_ROPE], dw_in_p[:, :2 * C]], axis=1)
    dw_in = jnp.transpose(dw_in_f.reshape(D, N_DEV, n_in // N_DEV), (1, 0, 2))
    dconv_w = jnp.transpose(dconv_w_p[:CONV_W].reshape(CONV_W, N_DEV, C // N_DEV), (1, 0, 2))
    rs_in = _exchange_start("rs_in_start", [dw_in, dw_uq, dw_uk, dw_uv, dconv_w], scatter(5))

    def ep_ln_in_bwd(acc, ex, out, first):
        dx0 = ALPHA * ex[0][...] + acc
        xhat, rstd = _ln_stats(ex[1][...])
        _acc_out(out[1], _colsum(dx0 * xhat), first)
        _acc_out(out[2], _colsum(dx0), first)
        out[0][...] = _ln_bwd(dx0, xhat, rstd, ex[2][...])

    tm, tk = _tile(T, 512), _tile(n_in_p, 640)
    rowblk = pl.BlockSpec((tm, D), lambda i, j, k: (i, 0))
    grad_x, dg_ln_in, db_ln_in = _matmul(
        "in_dgrad_ln_in_bwd", dh, w_in_p, dims=NT, grid=(T // tm, 1, n_in_p // tk),
        a_spec=pl.BlockSpec((tm, tk), lambda i, j, k: (i, k)), b_spec=pl.BlockSpec((D, tk), lambda i, j, k: (0, k)),
        acc_shape=(tm, D), extras=[(dz1, rowblk), (xs, rowblk), (row(ln_in_g) + rs_in[4], vecD)],
        outs=[(_sds((T, D), F32), rowblk), (_sds((1, D), F32), vecD), (_sds((1, D), F32), vecD)],
        epilogue=ep_ln_in_bwd, ep_rows=EPILOGUE_ROWS)

    small = dict(ln_in_g=dg_ln_in, ln_in_b=db_ln_in, g_cq=dg_cq, g_ckv=dg_ckv, conv_b=dconv_b, g_conv_ln=dg_conv_ln,
                 b_conv_ln=db_conv_ln, g_ln1=dg_ln1, b_ln1=db_ln1, g_ln2=dg_ln2, b_ln2=db_ln2)
    names = _small_names()
    rs_small = _exchange_start("rs_small_start", [_pack([small[n] for n in names])], gather(1))
    res = {}

    def update(group, parts, after):
        last = after
        for n, p in zip(group, parts):
            outs_n = _adamw("adamw_" + n, p, args[n][0], args["m_" + n][0], args["v_" + n][0])
            res[n] = [o.reshape(args[n].shape) for o in outs_n]
            last = outs_n[0]
        return last

    done = update(["w_ff2", "w_ff1"], _exchange_wait("rs_ff_wait", rs_ff, scatter(2), after=grad_x), grad_x)
    done = update(["w_out"], _exchange_wait("rs_out_wait", rs_out, scatter(1), after=done), done)
    done = update(["w_in", "w_uq", "w_uk", "w_uv", "conv_w"],
                  _exchange_wait("rs_in_wait", rs_in, scatter(5), after=done), done)
    (small_parts,) = _exchange_wait("rs_small_wait", rs_small, gather(1), after=done)
    packed = _adamw("adamw_small", small_parts, _pack([args[n] for n in names]), _pack([args["m_" + n] for n in names]),
                    _pack([args["v_" + n] for n in names]))
    like = [args[n] for n in names]
    unpacked = [_unpack(p, like) for p in packed]
    for i, n in enumerate(names):
        res[n] = [unpacked[kind][i] for kind in range(4)]

    order = ["ln_in_g", "ln_in_b", "w_in", "g_cq", "w_uq", "g_ckv", "w_uk", "w_uv", "conv_w", "conv_b", "g_conv_ln",
             "b_conv_ln", "w_out", "g_ln1", "b_ln1", "w_ff1", "w_ff2", "g_ln2", "b_ln2"]
    outs = [loss, grad_x.reshape(x.shape)]
    for kind in range(4):
        outs += [res[n][kind] for n in order]
    return tuple(outs)
```

```python
import jax
import jax.numpy as jnp
from jax import lax
from jax.experimental import pallas as pl
from jax.experimental.pallas import tpu as pltpu

F32 = jnp.float32
BF16 = jnp.bfloat16

N_HEADS = 8
D_NOPE = 128
D_ROPE = 64
D_V = 128
D_QK = D_NOPE + D_ROPE
D_HEAD_PAD = 256
LORA = 512
CONV_W = 31
CONV_HALF = CONV_W // 2
CONV_W_PAD = 32
HALO = 16
LN_EPS = 1e-5
RMS_EPS = 1e-6
ALPHA = 2.0 ** 0.25
SCALE = float(D_QK) ** -0.5
LOG2_E = 1.4426950408889634
LN_2 = 0.6931471805599453
Q_SCALE = SCALE * LOG2_E
ROPE_BASE = 10000.0
ADAM_LR, ADAM_B1, ADAM_B2, ADAM_EPS, ADAM_WD, ADAM_STEP = 0.001, 0.9, 0.999, 1e-08, 0.01, 10

N_DEV = 8
LANES = 128
VMEM_LIMIT_V7X = 56 * 1024 * 1024

NN = (((1,), (0,)), ((), ()))
NT = (((1,), (1,)), ((), ()))
TN = (((0,), (0,)), ((), ()))


def _call(body, *, name, grid, in_specs, out_specs, out_shape, scratch=(), aliases=None):
    params = pltpu.CompilerParams(dimension_semantics=("arbitrary",) * len(grid),
                                  vmem_limit_bytes=VMEM_LIMIT_V7X)
    return pl.pallas_call(body, name=name, grid=grid, in_specs=in_specs, out_specs=out_specs,
                          out_shape=out_shape, scratch_shapes=scratch, compiler_params=params,
                          input_output_aliases=aliases or {})


def _tile(n, pref):
    if n <= pref:
        return n
    t = (pref // LANES) * LANES
    while t > LANES and n % t:
        t -= LANES
    assert n % t == 0, (n, pref)
    return t


def _sds(shape, dtype):
    return jax.ShapeDtypeStruct(shape, dtype)


def _ln_stats(z):
    mu = jnp.mean(z, axis=-1, keepdims=True)
    zc = z - mu
    var = jnp.mean(zc * zc, axis=-1, keepdims=True)
    rstd = lax.rsqrt(var + LN_EPS)
    return zc * rstd, rstd


def _ln_bwd(dy, xhat, rstd, g):
    gd = dy * g
    m1 = jnp.mean(gd, axis=-1, keepdims=True)
    m2 = jnp.mean(gd * xhat, axis=-1, keepdims=True)
    return rstd * (gd - m1 - xhat * m2)


def _rms(x, g):
    return x * lax.rsqrt(jnp.mean(x * x, axis=-1, keepdims=True) + RMS_EPS) * g


def _rms_bwd(dy, x, g):
    r = lax.rsqrt(jnp.mean(x * x, axis=-1, keepdims=True) + RMS_EPS)
    dxn = dy * g
    dx = r * dxn - x * (r * r * r) * jnp.mean(dxn * x, axis=-1, keepdims=True)
    dg = jnp.sum(dy * x * r, axis=0, keepdims=True)
    return dx, dg


def _sigmoid(x):
    return 1.0 / (1.0 + jnp.exp(-x))


def _rope(x, cos_p, sin_a, sin_b):
    return x * cos_p + pltpu.roll(x, 96, 1) * sin_a + pltpu.roll(x, 32, 1) * sin_b


def _unrope(d, cos_p, sin_a, sin_b):
    return d * cos_p - pltpu.roll(d, 96, 1) * sin_a - pltpu.roll(d, 32, 1) * sin_b


def _colsum(v):
    return jnp.sum(v, axis=0, keepdims=True)


def _acc_out(ref, val, first):
    if first is False:
        ref[...] += val
        return

    @pl.when(first)
    def _():
        ref[...] = val

    @pl.when(jnp.logical_not(first))
    def _():
        ref[...] += val


class _Rows:
    def __init__(self, ref, sl):
        self.ref, self.sl = ref, sl

    def __getitem__(self, idx):
        assert idx is Ellipsis
        return self.ref[self.sl, :]

    def __setitem__(self, idx, val):
        assert idx is Ellipsis
        self.ref[self.sl, :] = val


def _matmul(name, a, b, *, dims, grid, a_spec, b_spec, acc_shape, outs, epilogue, extras=(), ep_rows=None, into=None):
    nk = grid[2]
    ne, no = len(extras), len(outs)
    tm = acc_shape[0]
    n_in = 2 + ne + (0 if into is None else 1)

    def finish(acc_rows, ex, out):
        first = pl.program_id(0) == 0
        if ep_rows is None or ep_rows >= tm:
            epilogue(acc_rows(slice(None)), ex, out, first)
            return
        for r0 in range(0, tm, ep_rows):
            sl = slice(r0, r0 + ep_rows)
            view = lambda r: _Rows(r, sl) if r.shape[0] == tm else r
            epilogue(acc_rows(sl), [view(r) for r in ex], [view(r) for r in out], first if r0 == 0 else False)

    def body(*refs):
        a_ref, b_ref = refs[0], refs[1]
        ex = refs[2:2 + ne]
        out = refs[n_in:n_in + no]
        if nk == 1:
            part = lax.dot_general(a_ref[...], b_ref[...], dims, preferred_element_type=F32)
            finish(lambda sl: part[sl, :], ex, out)
        else:
            acc = refs[n_in + no]
            k = pl.program_id(2)

            @pl.when(k == 0)
            def _():
                acc[...] = jnp.zeros_like(acc)

            acc[...] += lax.dot_general(a_ref[...], b_ref[...], dims, preferred_element_type=F32)

            @pl.when(k == nk - 1)
            def _():
                finish(lambda sl: acc[sl, :], ex, out)

    scratch = [] if nk == 1 else [pltpu.VMEM(acc_shape, F32)]
    ins = [a, b] + [e for e, _ in extras]
    in_specs = [a_spec, b_spec] + [s for _, s in extras]
    aliases = {}
    if into is not None:
        ins.append(into)
        in_specs.append(pl.BlockSpec(memory_space=pl.ANY))
        aliases = {n_in - 1: 0}
    return _call(body, name=name, grid=grid, in_specs=in_specs, out_specs=[s for _, s in outs],
                 out_shape=[o for o, _ in outs], scratch=scratch, aliases=aliases)(*ins)


def _store(dtype=F32):
    def ep(acc, ex, out, first):
        out[0][...] = acc.astype(dtype)
    return ep


def _mesh_pos():
    return lax.axis_index("x"), lax.axis_index("y"), lax.axis_index("c")


def _flip(v, bit):
    return 1 - v if bit else v


_HBM = pl.BlockSpec(memory_space=pltpu.HBM)
_SEM = pl.BlockSpec(memory_space=pltpu.SEMAPHORE)
_EFFECT = pltpu.SideEffectType.DATAFLOW_SIDE_EFFECTING


def _my_slot():
    x, y, c = _mesh_pos()
    return 4 * x + 2 * y + c


def _exchange_copies(srcs, lands, send_sems, recv_sems, stacked, receives=True):
    x, y, c = _mesh_pos()
    me = 4 * x + 2 * y + c
    pairs = []
    for w in range(len(srcs)):
        for k in range(1, N_DEV):
            peer = (_flip(x, k & 4), _flip(y, k & 2), _flip(c, k & 1))
            peer_slot = 4 * peer[0] + 2 * peer[1] + peer[2]
            to_peer = srcs[w].at[peer_slot] if stacked[w] else srcs[w]
            mine = srcs[w].at[me] if stacked[w] else srcs[w]
            s = w * (N_DEV - 1) + k - 1
            sems = dict(send_sem=send_sems.at[s], recv_sem=recv_sems.at[s],
                        device_id=peer, device_id_type=pl.DeviceIdType.MESH)
            send = pltpu.make_async_remote_copy(src_ref=to_peer, dst_ref=lands[w].at[me], **sems)
            recv = pltpu.make_async_remote_copy(src_ref=mine, dst_ref=lands[w].at[peer_slot], **sems) if receives else None
            pairs.append((send, recv))
    return pairs


def _exchange_start(name, srcs, stacked):
    n = len(srcs)
    land_shapes = [s.shape if st else (N_DEV,) + s.shape for s, st in zip(srcs, stacked)]

    def body(*refs):
        src, land = refs[:n], refs[n:2 * n]
        send_sems, recv_sems = refs[2 * n], refs[2 * n + 1]
        token = refs[-1]
        for send, _ in _exchange_copies(src, land, send_sems, recv_sems, stacked, receives=False):
            send.start()
        token[...] = jnp.zeros_like(token)

    hbm = lambda a: pltpu.with_memory_space_constraint(a, pltpu.HBM)
    outs = pl.pallas_call(
        body, name=name,
        out_shape=(pltpu.SemaphoreType.DMA((n * (N_DEV - 1),)), pltpu.SemaphoreType.DMA((n * (N_DEV - 1),)),
                   *[pltpu.HBM(s.shape, s.dtype) for s in srcs],
                   *[pltpu.HBM(ls, s.dtype) for ls, s in zip(land_shapes, srcs)],
                   _sds((8, LANES), F32)),
        in_specs=[_HBM] * (2 * n),
        out_specs=(_SEM, _SEM, *[_HBM] * (2 * n), pl.BlockSpec(memory_space=pltpu.VMEM)),
        input_output_aliases={i: 2 + i for i in range(2 * n)},
        compiler_params=pltpu.CompilerParams(has_side_effects=_EFFECT),
    )(*[hbm(s) for s in srcs], *[hbm(lax.empty(ls, s.dtype)) for ls, s in zip(land_shapes, srcs)])
    return outs[0], outs[1], list(outs[2:2 + n]), list(outs[2 + n:2 + 2 * n]), outs[-1][0, 0]


def _exchange_wait(name, started, stacked, after):
    srcs, lands = _wait_call(name, started, stacked, after)
    me = _my_slot()
    full = []
    for src, land, st in zip(srcs, lands, stacked):
        own = lax.dynamic_index_in_dim(src, me, 0, keepdims=True) if st else src[None]
        full.append(lax.dynamic_update_index_in_dim(land, own, me, 0))
    return full


def _wait_call(name, started, stacked, after):
    send_sems, recv_sems, srcs, lands, _ = started
    n = len(srcs)

    def body(*refs):
        src, land = refs[:n], refs[n:2 * n]
        s_sems, r_sems = refs[2 * n], refs[2 * n + 1]
        for send, recv in _exchange_copies(src, land, s_sems, r_sems, stacked):
            send.wait_send()
            recv.wait_recv()

    outs = pl.pallas_call(
        body, name=name,
        out_shape=tuple(pltpu.HBM(a.shape, a.dtype) for a in srcs + lands),
        in_specs=[_HBM] * (2 * n) + [_SEM, _SEM, pl.BlockSpec(memory_space=pl.ANY)],
        out_specs=[_HBM] * (2 * n),
        input_output_aliases={i: i for i in range(2 * n)},
        compiler_params=pltpu.CompilerParams(has_side_effects=_EFFECT),
    )(*srcs, *lands, send_sems, recv_sems, after)
    return outs[:n], outs[n:]


def _adamw(name, parts, w, m, v):
    rows, cols = w.shape
    cap = max(8, (LANES * 1024) // cols)
    tr = rows
    if rows > cap:
        tr = (cap // 8) * 8
        while rows % tr:
            tr -= 8
    c1 = 1.0 / (1.0 - ADAM_B1 ** ADAM_STEP)
    c2 = 1.0 / (1.0 - ADAM_B2 ** ADAM_STEP)

    def body(p_ref, w_ref, m_ref, v_ref, g_o, d_o, m_o, v_o):
        g = p_ref[0]
        for s in range(1, N_DEV):
            g = g + p_ref[s]
        mn = ADAM_B1 * m_ref[...] + (1.0 - ADAM_B1) * g
        vn = ADAM_B2 * v_ref[...] + (1.0 - ADAM_B2) * (g * g)
        g_o[...] = g
        m_o[...] = mn
        v_o[...] = vn
        d_o[...] = -ADAM_LR * ((mn * c1) / (jnp.sqrt(vn * c2) + ADAM_EPS) + ADAM_WD * w_ref[...])

    blk = pl.BlockSpec((tr, cols), lambda i: (i, 0))
    return _call(body, name=name, grid=(rows // tr,),
                 in_specs=[pl.BlockSpec((N_DEV, tr, cols), lambda i: (0, i, 0)), blk, blk, blk],
                 out_specs=[blk] * 4, out_shape=[_sds((rows, cols), F32)] * 4)(parts, w, m, v)


def _rope_tables(pos_col, inv_freq):
    T = pos_col.shape[0]
    tm = _tile(T, 1024)

    def body(p_ref, f_ref, c_o, sa_o, sb_o):
        ang = p_ref[...].astype(F32) * f_ref[...]
        lane = lax.broadcasted_iota(jnp.int32, ang.shape, 1)
        cs, sn = jnp.cos(ang), jnp.sin(ang)
        c_o[...] = jnp.where(lane < D_ROPE, cs, 0.0)
        sa_o[...] = jnp.where(lane < D_ROPE // 2, -sn, 0.0)
        sb_o[...] = jnp.where((lane >= D_ROPE // 2) & (lane < D_ROPE), sn, 0.0)

    blk = pl.BlockSpec((tm, LANES), lambda i: (i, 0))
    return _call(body, name="rope_tables", grid=(T // tm,),
                 in_specs=[pl.BlockSpec((tm, 1), lambda i: (i, 0)), pl.BlockSpec((1, LANES), lambda i: (0, 0))],
                 out_specs=[blk] * 3, out_shape=[_sds((T, LANES), F32)] * 3)(pos_col, inv_freq)


def _ln_in(x, g, b):
    T, D = x.shape
    tm = _tile(T, 512)

    def body(x_ref, g_ref, b_ref, o32, o16):
        xhat, _ = _ln_stats(x_ref[...])
        y = xhat * g_ref[...] + b_ref[...]
        o32[...] = y
        o16[...] = y.astype(BF16)

    blk = pl.BlockSpec((tm, D), lambda i: (i, 0))
    vec = pl.BlockSpec((1, D), lambda i: (0, 0))
    return _call(body, name="ln_in", grid=(T // tm,), in_specs=[blk, vec, vec], out_specs=[blk, blk],
                 out_shape=[_sds((T, D), F32), _sds((T, D), BF16)])(x, g, b)


def _mid(h, g_cq, g_ckv, tabs, C):
    T = h.shape[0]
    tm = _tile(T, 256)
    cq_blk, kr_blk = (2 * C) // LORA, (2 * C + 2 * LORA) // LANES

    def body(a_ref, gt_ref, cq_ref, ckv_ref, kr_ref, gq_ref, gkv_ref, cp, sa, sb, u_o, cqn_o, ckvn_o, kr_o):
        u_o[...] = a_ref[...] * _sigmoid(gt_ref[...])
        cqn_o[...] = _rms(cq_ref[...], gq_ref[...]).astype(BF16)
        ckvn_o[...] = _rms(ckv_ref[...], gkv_ref[...]).astype(BF16)
        kr_o[...] = _rope(kr_ref[...], cp[...], sa[...], sb[...]).astype(BF16)

    def col(w, j):
        return pl.BlockSpec((tm, w), lambda i: (i, j))

    vec = pl.BlockSpec((1, LORA), lambda i: (0, 0))
    return _call(body, name="mid_norm_glu", grid=(T // tm,),
                 in_specs=[col(C, 0), col(C, 1), col(LORA, cq_blk), col(LORA, cq_blk + 1), col(LANES, kr_blk),
                           vec, vec, col(LANES, 0), col(LANES, 0), col(LANES, 0)],
                 out_specs=[col(C, 0), col(LORA, 0), col(LORA, 0), col(LANES, 0)],
                 out_shape=[_sds((T, C), F32), _sds((T, LORA), BF16), _sds((T, LORA), BF16), _sds((T, LANES), BF16)],
                 )(h, h, h, h, h, g_cq, g_ckv, *tabs)


def _q_proj(cqn, w_uq_p, tabs):
    T = cqn.shape[0]
    tm = _tile(T, HEAD_ROWS)

    def body(c_ref, w_ref, cp, sa, sb, o_ref):
        q = jnp.dot(c_ref[...], w_ref[...], preferred_element_type=F32)
        o_ref[:, :D_NOPE] = (q[:, :D_NOPE] * Q_SCALE).astype(BF16)
        o_ref[:, D_NOPE:] = (_rope(q[:, D_NOPE:], cp[...], sa[...], sb[...]) * Q_SCALE).astype(BF16)

    tab = pl.BlockSpec((tm, LANES), lambda i, h: (i, 0))
    return _call(body, name="q_proj_rope", grid=(T // tm, N_HEADS),
                 in_specs=[pl.BlockSpec((tm, LORA), lambda i, h: (i, 0)),
                           pl.BlockSpec((None, LORA, D_HEAD_PAD), lambda i, h: (h, 0, 0)), tab, tab, tab],
                 out_specs=pl.BlockSpec((None, tm, D_HEAD_PAD), lambda i, h: (h, i, 0)),
                 out_shape=_sds((N_HEADS, T, D_HEAD_PAD), BF16))(cqn, w_uq_p, *tabs)


def _kv_proj(ckvn, w_ukv, kr):
    T = ckvn.shape[0]
    tm = _tile(T, HEAD_ROWS)

    def body(c_ref, w_ref, kr_ref, k_o, v_o):
        kv = jnp.dot(c_ref[...], w_ref[...], preferred_element_type=F32)
        k_o[:, :D_NOPE] = kv[:, :D_NOPE].astype(BF16)
        k_o[:, D_NOPE:] = kr_ref[...]
        v_o[:, :D_V] = kv[:, D_NOPE:].astype(BF16)
        v_o[:, D_V:] = jnp.ones((tm, D_V), BF16)

    return _call(body, name="kv_proj", grid=(T // tm, N_HEADS),
                 in_specs=[pl.BlockSpec((tm, LORA), lambda i, h: (i, 0)),
                           pl.BlockSpec((None, LORA, D_NOPE + D_V), lambda i, h: (h, 0, 0)),
                           pl.BlockSpec((tm, LANES), lambda i, h: (i, 0))],
                 out_specs=[pl.BlockSpec((None, tm, D_HEAD_PAD), lambda i, h: (h, i, 0)),
                            pl.BlockSpec((None, tm, 2 * D_V), lambda i, h: (h, i, 0))],
                 out_shape=[_sds((N_HEADS, T, D_HEAD_PAD), BF16), _sds((N_HEADS, T, 2 * D_V), BF16)])(ckvn, w_ukv, kr)


def _flash_fwd(q, k, v1, out_cols):
    _, T, _ = q.shape
    tq, tk = _tile(T, FLASH_TQ), _tile(T, FLASH_TK)
    nkv, reps = T // tk, tk // LANES

    def body(q_ref, k_ref, v_ref, o_ref, lse_ref, m_sc, acc_sc):
        m_sc[...] = jnp.full_like(m_sc, -jnp.inf)
        acc_sc[...] = jnp.zeros_like(acc_sc)
        qv = q_ref[...]

        def rows(j):
            return pl.ds(pl.multiple_of(j * tk, tk), tk)

        def scores(j):
            return lax.dot_general(qv, k_ref[rows(j), :], NT, preferred_element_type=F32)

        def update(s, j):
            m_prev = m_sc[...]
            m_new = jnp.maximum(m_prev, jnp.max(s, axis=1, keepdims=True))
            a = jnp.exp2(m_prev - m_new)
            p = jnp.exp2(s - jnp.tile(m_new, (1, reps)))
            pv = jnp.dot(p.astype(BF16), v_ref[rows(j), :], preferred_element_type=F32)
            acc_sc[...] = jnp.tile(a, (1, 2)) * acc_sc[...] + pv
            m_sc[...] = m_new

        def step(j, carry):
            update(scores(j), j)
            return carry

        lax.fori_loop(0, nkv, step, 0, unroll=FLASH_UNROLL_FWD if nkv % FLASH_UNROLL_FWD == 0 else 1)
        acc = acc_sc[...]
        l = acc[:, D_V:]
        o_ref[...] = (acc[:, :D_V] / l).astype(BF16)
        lse_ref[...] = m_sc[...] + jnp.log(l) * LOG2_E

    return _call(body, name="flash_fwd", grid=(N_HEADS, T // tq),
                 in_specs=[pl.BlockSpec((None, tq, D_HEAD_PAD), lambda h, i: (h, i, 0)),
                           pl.BlockSpec((None, T, D_HEAD_PAD), lambda h, i: (h, 0, 0)),
                           pl.BlockSpec((None, T, 2 * D_V), lambda h, i: (h, 0, 0))],
                 out_specs=[pl.BlockSpec((tq, D_V), lambda h, i: (i, h)),
                            pl.BlockSpec((None, tq, LANES), lambda h, i: (h, i, 0))],
                 out_shape=[_sds((T, out_cols), BF16), _sds((N_HEADS, T, LANES), F32)],
                 scratch=[pltpu.VMEM((tq, LANES), F32), pltpu.VMEM((tq, 2 * D_V), F32)])(q, k, v1)


def _halo_specs(tm, cb, n_t):
    r = tm // HALO
    return [pl.BlockSpec((HALO, cb), lambda jc, i: (jnp.maximum(i * r - 1, 0), jc)),
            pl.BlockSpec((tm, cb), lambda jc, i: (i, jc)),
            pl.BlockSpec((HALO, cb), lambda jc, i: (jnp.minimum((i + 1) * r, n_t * r - 1), jc))]


def _fill_ext(ext, prev_ref, cur_ref, next_ref, i, n_t, tm):
    ext[0:HALO, :] = jnp.where(i > 0, prev_ref[...], 0.0)
    ext[HALO:HALO + tm, :] = cur_ref[...]
    ext[HALO + tm:, :] = jnp.where(i < n_t - 1, next_ref[...], 0.0)


HEAD_ROWS = 2048
FLASH_TQ = 512
FLASH_TK = 512
FLASH_UNROLL_FWD = 16
FLASH_UNROLL_BWD = 8
EPILOGUE_ROWS = 128
CONV_ROWS = 64


def _conv_fwd(u, w_pad, bias):
    T, C = u.shape
    tm, cb = _tile(T, 256), _tile(C, 256)
    n_t = T // tm
    rb = min(CONV_ROWS, tm)

    def body(up, uc, un, w_ref, b_ref, c_o, ext):
        i = pl.program_id(1)
        _fill_ext(ext, up, uc, un, i, n_t, tm)
        for r0 in range(0, tm, rb):
            acc = jnp.zeros((rb, cb), F32) + b_ref[...]
            for k in range(CONV_W):
                acc = acc + w_ref[k:k + 1, :] * ext[r0 + k + 1:r0 + k + 1 + rb, :]
            c_o[r0:r0 + rb, :] = acc

    return _call(body, name="conv_fwd", grid=(C // cb, n_t),
                 in_specs=_halo_specs(tm, cb, n_t) + [pl.BlockSpec((CONV_W_PAD, cb), lambda jc, i: (0, jc)),
                                                      pl.BlockSpec((1, cb), lambda jc, i: (0, jc))],
                 out_specs=pl.BlockSpec((tm, cb), lambda jc, i: (i, jc)),
                 out_shape=_sds((T, C), F32),
                 scratch=[pltpu.VMEM((tm + 2 * HALO, cb), F32)])(u, u, u, w_pad, bias)


def _conv_post(c, g, b, cat):
    T, C = c.shape
    assert cat.shape == (T, 2 * C)
    tm = _tile(T, 512)

    def body(c_ref, g_ref, b_ref, cat_ref, o_ref):
        xhat, _ = _ln_stats(c_ref[...])
        y = xhat * g_ref[...] + b_ref[...]
        o_ref[...] = (y * _sigmoid(y)).astype(BF16)

    blk = pl.BlockSpec((tm, C), lambda i: (i, 0))
    vec = pl.BlockSpec((1, C), lambda i: (0, 0))
    return _call(body, name="conv_ln_silu", grid=(T // tm,),
                 in_specs=[blk, vec, vec, pl.BlockSpec(memory_space=pl.ANY)],
                 out_specs=pl.BlockSpec((tm, C), lambda i: (i, 1)),
                 out_shape=_sds(cat.shape, BF16), aliases={3: 0})(c, g, b, cat)


def _conv_post_bwd(dcat, c, g, b):
    T, C = c.shape
    tm = _tile(T, 512)

    def body(d_ref, c_ref, g_ref, b_ref, dc_o, dg_o, db_o):
        first = pl.program_id(0) == 0
        xhat, rstd = _ln_stats(c_ref[...])
        y = xhat * g_ref[...] + b_ref[...]
        sg = _sigmoid(y)
        dy = d_ref[...] * (sg * (1.0 + y * (1.0 - sg)))
        _acc_out(dg_o, _colsum(dy * xhat), first)
        _acc_out(db_o, _colsum(dy), first)
        dc_o[...] = _ln_bwd(dy, xhat, rstd, g_ref[...])

    blk = pl.BlockSpec((tm, C), lambda i: (i, 0))
    vec = pl.BlockSpec((1, C), lambda i: (0, 0))
    return _call(body, name="conv_ln_silu_bwd", grid=(T // tm,),
                 in_specs=[pl.BlockSpec((tm, C), lambda i: (i, 1)), blk, vec, vec],
                 out_specs=[blk, vec, vec],
                 out_shape=[_sds((T, C), F32), _sds((1, C), F32), _sds((1, C), F32)])(dcat, c, g, b)


def _conv_bwd(dc, u, w_pad):
    T, C = u.shape
    tm, cb = _tile(T, 256), _tile(C, 256)
    n_t = T // tm
    rb = min(CONV_ROWS, tm)

    def body(dp, dcur, dn, up, uc, un, w_ref, du_o, dw_o, db_o, dext, uext, dw_sc):
        i = pl.program_id(1)
        _fill_ext(dext, dp, dcur, dn, i, n_t, tm)
        _fill_ext(uext, up, uc, un, i, n_t, tm)

        @pl.when(i == 0)
        def _():
            dw_sc[...] = jnp.zeros_like(dw_sc)

        for r0 in range(0, tm, rb):
            acc = jnp.zeros((rb, cb), F32)
            d_here = dcur[r0:r0 + rb, :]
            for k in range(CONV_W):
                acc = acc + w_ref[k:k + 1, :] * dext[r0 + 2 * HALO - 1 - k:r0 + 2 * HALO - 1 - k + rb, :]
                prod = d_here * uext[r0 + k + 1:r0 + k + 1 + rb, :]
                dw_sc[k] += jnp.sum(prod.reshape(rb // 8, 8, cb), axis=0)
            du_o[r0:r0 + rb, :] = acc
        dw_sc[CONV_W] += jnp.sum(dcur[...].reshape(tm // 8, 8, cb), axis=0)

        @pl.when(i == n_t - 1)
        def _():
            red = jnp.sum(dw_sc[...], axis=1)
            row = lax.broadcasted_iota(jnp.int32, red.shape, 0)
            dw_o[...] = jnp.where(row < CONV_W, red, 0.0)
            db_o[...] = jnp.sum(jnp.where(row == CONV_W, red, 0.0), axis=0, keepdims=True)

    return _call(body, name="conv_bwd", grid=(C // cb, n_t),
                 in_specs=_halo_specs(tm, cb, n_t) + _halo_specs(tm, cb, n_t)
                 + [pl.BlockSpec((CONV_W_PAD, cb), lambda jc, i: (0, jc))],
                 out_specs=[pl.BlockSpec((tm, cb), lambda jc, i: (i, jc)),
                            pl.BlockSpec((CONV_W_PAD, cb), lambda jc, i: (0, jc)),
                            pl.BlockSpec((1, cb), lambda jc, i: (0, jc))],
                 out_shape=[_sds((T, C), F32), _sds((CONV_W_PAD, C), F32), _sds((1, C), F32)],
                 scratch=[pltpu.VMEM((tm + 2 * HALO, cb), F32), pltpu.VMEM((tm + 2 * HALO, cb), F32),
                          pltpu.VMEM((CONV_W_PAD, 8, cb), F32)])(dc, dc, dc, u, u, u, w_pad)


def _glu_bwd(du, h, C):
    T = du.shape[0]
    tm = _tile(T, 512)

    def body(du_ref, a_ref, gt_ref, o_ref):
        sg = _sigmoid(gt_ref[...])
        du_v = du_ref[...]
        o_ref[:, :C] = (du_v * sg).astype(BF16)
        o_ref[:, C:] = (du_v * a_ref[...] * sg * (1.0 - sg)).astype(BF16)

    return _call(body, name="glu_bwd", grid=(T // tm,),
                 in_specs=[pl.BlockSpec((tm, C), lambda i: (i, 0)), pl.BlockSpec((tm, C), lambda i: (i, 0)),
                           pl.BlockSpec((tm, C), lambda i: (i, 1))],
                 out_specs=pl.BlockSpec((tm, 2 * C), lambda i: (i, 0)),
                 out_shape=_sds(h.shape, BF16))(du, h, h)


def _attn_delta(dcat, attn):
    T = attn.shape[0]
    tm = _tile(T, HEAD_ROWS)

    def body(d_ref, o_ref, dl_o, dob_o):
        d = d_ref[...]
        dl = jnp.sum(d * o_ref[...].astype(F32), axis=1, keepdims=True)
        dl_o[...] = jnp.broadcast_to(dl, (tm, LANES))
        dob_o[...] = d.astype(BF16)

    blk = pl.BlockSpec((tm, D_V), lambda i, h: (i, h))
    hblk = pl.BlockSpec((None, tm, D_V), lambda i, h: (h, i, 0))
    return _call(body, name="attn_delta", grid=(T // tm, N_HEADS), in_specs=[blk, blk], out_specs=[hblk, hblk],
                 out_shape=[_sds((N_HEADS, T, LANES), F32), _sds((N_HEADS, T, D_V), BF16)])(dcat, attn)


def _flash_bwd(q, k, v1, do, lse, delta):
    _, T, _ = q.shape
    tq, tk = _tile(T, FLASH_TQ), _tile(T, FLASH_TK)
    nkv, reps = T // tk, tk // LANES

    def body(q_ref, do_ref, lse_ref, dl_ref, k_ref, v_ref, dq_o, dk_o, dv_o, dq_sc):
        @pl.when(pl.program_id(1) == 0)
        def _():
            dk_o[...] = jnp.zeros_like(dk_o)
            dv_o[...] = jnp.zeros_like(dv_o)

        qv, dov = q_ref[...], do_ref[...]
        lse_t = jnp.tile(lse_ref[...], (1, reps))
        dl_t = jnp.tile(dl_ref[...], (1, reps))
        dq_sc[...] = jnp.zeros_like(dq_sc)

        def rows(j):
            return pl.ds(pl.multiple_of(j * tk, tk), tk)

        def scores(j):
            s = lax.dot_general(qv, k_ref[rows(j), :], NT, preferred_element_type=F32)
            dp = lax.dot_general(dov, v_ref[rows(j), :D_V], NT, preferred_element_type=F32)
            return s, dp

        def update(s_dp, j):
            s, dp = s_dp
            p = jnp.exp2(s - lse_t)
            ds = (p * (dp - dl_t)).astype(BF16)
            dv_o[rows(j), :] += lax.dot_general(p.astype(BF16), dov, TN, preferred_element_type=F32)
            dk_o[rows(j), :] += lax.dot_general(ds, qv, TN, preferred_element_type=F32)
            dq_sc[...] += jnp.dot(ds, k_ref[rows(j), :], preferred_element_type=F32)

        def step(j, carry):
            update(scores(j), j)
            return carry

        lax.fori_loop(0, nkv, step, 0, unroll=FLASH_UNROLL_BWD if nkv % FLASH_UNROLL_BWD == 0 else 1)
        dq_o[...] = dq_sc[...]

    def tile(w):
        return pl.BlockSpec((None, tq, w), lambda h, i: (h, i, 0))

    def whole(w):
        return pl.BlockSpec((None, T, w), lambda h, i: (h, 0, 0))

    return _call(body, name="flash_bwd", grid=(N_HEADS, T // tq),
                 in_specs=[tile(D_HEAD_PAD), tile(D_V), tile(LANES), tile(LANES), whole(D_HEAD_PAD), whole(2 * D_V)],
                 out_specs=[tile(D_HEAD_PAD), whole(D_HEAD_PAD), whole(D_V)],
                 out_shape=[_sds((N_HEADS, T, D_HEAD_PAD), F32), _sds((N_HEADS, T, D_HEAD_PAD), F32),
                            _sds((N_HEADS, T, D_V), F32)],
                 scratch=[pltpu.VMEM((tq, D_HEAD_PAD), F32)])(q, do, lse, delta, k, v1)


def _dq_post(dq, tabs):
    _, T, _ = dq.shape
    tm = _tile(T, HEAD_ROWS)

    def body(d_ref, cp, sa, sb, o_ref):
        d = d_ref[...] * SCALE
        o_ref[:, :D_NOPE] = d[:, :D_NOPE].astype(BF16)
        o_ref[:, D_NOPE:] = _unrope(d[:, D_NOPE:], cp[...], sa[...], sb[...]).astype(BF16)

    blk = pl.BlockSpec((None, tm, D_HEAD_PAD), lambda i, h: (h, i, 0))
    tab = pl.BlockSpec((tm, LANES), lambda i, h: (i, 0))
    return _call(body, name="dq_unrope", grid=(T // tm, N_HEADS), in_specs=[blk, tab, tab, tab],
                 out_specs=pl.BlockSpec((tm, D_HEAD_PAD), lambda i, h: (i, h)),
                 out_shape=_sds((T, N_HEADS * D_HEAD_PAD), BF16))(dq, *tabs)


def _dk_post(dk, dv, tabs, dh, kr_blk):
    _, T, _ = dk.shape
    tm = _tile(T, 1024)

    def body(dk_ref, dv_ref, cp, sa, sb, dh_ref, dkv_o, dkr_o, sc):
        h = pl.program_id(1)
        d = dk_ref[...] * LN_2
        dkv_o[:, :D_NOPE] = d[:, :D_NOPE].astype(BF16)
        dkv_o[:, D_NOPE:] = dv_ref[...].astype(BF16)

        @pl.when(h == 0)
        def _():
            sc[...] = d[:, D_NOPE:]

        @pl.when(h > 0)
        def _():
            sc[...] += d[:, D_NOPE:]

        @pl.when(h == N_HEADS - 1)
        def _():
            dkr_o[...] = _unrope(sc[...], cp[...], sa[...], sb[...]).astype(BF16)

    tab = pl.BlockSpec((tm, LANES), lambda i, h: (i, 0))
    return _call(body, name="dk_unrope", grid=(T // tm, N_HEADS),
                 in_specs=[pl.BlockSpec((None, tm, D_HEAD_PAD), lambda i, h: (h, i, 0)),
                           pl.BlockSpec((None, tm, D_V), lambda i, h: (h, i, 0)), tab, tab, tab,
                           pl.BlockSpec(memory_space=pl.ANY)],
                 out_specs=[pl.BlockSpec((tm, D_HEAD_PAD), lambda i, h: (i, h)),
                            pl.BlockSpec((tm, LANES), lambda i, h: (i, kr_blk))],
                 out_shape=[_sds((T, N_HEADS * D_HEAD_PAD), BF16), _sds(dh.shape, BF16)],
                 scratch=[pltpu.VMEM((tm, LANES), F32)], aliases={5: 1})(dk, dv, *tabs, dh)


def _latent_bwd(name, dproj, w_heads, h, col_blk, g, dh):
    T, HD = dproj.shape
    tm = _tile(T, 1024)

    def ep(acc, ex, out, first):
        dx, dg = _rms_bwd(acc, ex[0][...], ex[1][...])
        out[0][...] = dx.astype(BF16)
        _acc_out(out[1], dg, first)

    return _matmul(name, dproj, w_heads, dims=NT, grid=(T // tm, 1, 1),
                   a_spec=pl.BlockSpec((tm, HD), lambda i, j, k: (i, 0)),
                   b_spec=pl.BlockSpec((LORA, HD), lambda i, j, k: (0, 0)),
                   acc_shape=(tm, LORA),
                   extras=[(h, pl.BlockSpec((tm, LORA), lambda i, j, k: (i, col_blk))),
                           (g, pl.BlockSpec((1, LORA), lambda i, j, k: (0, 0)))],
                   outs=[(_sds(dh.shape, BF16), pl.BlockSpec((tm, LORA), lambda i, j, k: (i, col_blk))),
                         (_sds((1, LORA), F32), pl.BlockSpec((1, LORA), lambda i, j, k: (0, 0)))],
                   epilogue=ep, into=dh)


def _head_weight_grad(name, latent, dproj):
    T = dproj.shape[0]
    tk = _tile(T, 2048)
    return _matmul(name, latent, dproj, dims=TN, grid=(N_HEADS, 1, T // tk),
                   a_spec=pl.BlockSpec((tk, LORA), lambda i, j, k: (k, 0)),
                   b_spec=pl.BlockSpec((tk, D_HEAD_PAD), lambda i, j, k: (k, i)),
                   acc_shape=(LORA, D_HEAD_PAD),
                   outs=[(_sds((N_HEADS, LORA, D_HEAD_PAD), F32),
                          pl.BlockSpec((None, LORA, D_HEAD_PAD), lambda i, j, k: (i, 0, 0)))],
                   epilogue=_store())[0]


def _weight_grad(name, a, b, tm_pref=1024, tn_pref=1024, stacked_cols=None):
    T, M = a.shape
    N = b.shape[1]
    tk = _tile(T, 2048)
    tm = _tile(M, tm_pref)
    if stacked_cols is None:
        tn = _tile(N, tn_pref)
        out = (_sds((M, N), F32), pl.BlockSpec((tm, tn), lambda i, j, k: (i, j)))
    else:
        tn = _tile(stacked_cols, tn_pref)
        per = stacked_cols // tn
        out = (_sds((N // stacked_cols, M, stacked_cols), F32),
               pl.BlockSpec((None, tm, tn), lambda i, j, k: (j // per, i, j % per)))
    return _matmul(name, a, b, dims=TN, grid=(M // tm, N // tn, T // tk),
                   a_spec=pl.BlockSpec((tk, tm), lambda i, j, k: (k, i)),
                   b_spec=pl.BlockSpec((tk, tn), lambda i, j, k: (k, j)),
                   acc_shape=(tm, tn), outs=[out], epilogue=_store())[0]


def _small_names():
    return ["ln_in_g", "ln_in_b", "g_cq", "g_ckv", "conv_b", "g_conv_ln", "b_conv_ln", "g_ln1", "b_ln1", "g_ln2", "b_ln2"]


def _pack(vecs):
    flat = jnp.concatenate([v.reshape(-1) for v in vecs])
    assert flat.shape[0] % (8 * LANES) == 0
    return flat.reshape(-1, LANES)


def _unpack(packed, like):
    flat, out, off = packed.reshape(-1), [], 0
    for v in like:
        out.append(flat[off:off + v.size].reshape(v.shape))
        off += v.size
    return out


def kernel(x, positions, ln_in_g, ln_in_b, w_in, g_cq, w_uq, g_ckv, w_uk, w_uv, conv_w, conv_b, g_conv_ln, b_conv_ln, w_out, g_ln1, b_ln1, w_ff1, w_ff2, g_ln2, b_ln2, loss_target, m_ln_in_g, m_ln_in_b, m_w_in, m_g_cq, m_w_uq, m_g_ckv, m_w_uk, m_w_uv, m_conv_w, m_conv_b, m_g_conv_ln, m_b_conv_ln, m_w_out, m_g_ln1, m_b_ln1, m_w_ff1, m_w_ff2, m_g_ln2, m_b_ln2, v_ln_in_g, v_ln_in_b, v_w_in, v_g_cq, v_w_uq, v_g_ckv, v_w_uk, v_w_uv, v_conv_w, v_conv_b, v_g_conv_ln, v_b_conv_ln, v_w_out, v_g_ln1, v_b_ln1, v_w_ff1, v_w_ff2, v_g_ln2, v_b_ln2):
    args = dict(locals())
    T, D = x.shape[1], x.shape[2]
    C = D - N_HEADS * D_V
    Fs = w_ff1.shape[2]
    F = N_DEV * Fs
    n_in = N_DEV * w_in.shape[2]
    n_in_p = 2 * C + 2 * LORA + LANES
    assert n_in == 2 * LORA + D_ROPE + 2 * C and w_uq.shape[2] == D_QK and conv_w.shape[2] * N_DEV == C

    xs, tgt = x[0], loss_target[0]
    row = lambda v_: v_.reshape(1, -1)

    gather = lambda k_: [False] * k_
    ag_in = _exchange_start("ag_in_start", [w_in[0].astype(BF16)], gather(1))
    ag_heads = _exchange_start("ag_heads_start", [w_uq[0].astype(BF16), w_uk[0].astype(BF16), w_uv[0].astype(BF16),
                                                  conv_w[0]], gather(4))
    ag_ff = _exchange_start("ag_ff_start", [w_out[0].astype(BF16), w_ff1[0].astype(BF16), w_ff2[0].astype(BF16)],
                            gather(3))
    started = ag_in[4] + ag_heads[4] + ag_ff[4]

    half = D_ROPE // 2
    inv_freq = ROPE_BASE ** (-jnp.arange(half, dtype=F32) * (2.0 / D_ROPE))
    inv_freq = jnp.tile(inv_freq, LANES // half).reshape(1, LANES)
    tabs = _rope_tables(positions.reshape(T, 1), inv_freq)

    x0, x0b = _ln_in(xs, row(ln_in_g) + started, row(ln_in_b))

    (g_w_in,) = _exchange_wait("ag_in_wait", ag_in, gather(1), after=x0b)
    w_in_f = jnp.transpose(g_w_in, (1, 0, 2)).reshape(D, n_in)
    s_cq, s_ckv, s_kr, s_a, s_g = 0, LORA, 2 * LORA, 2 * LORA + D_ROPE, 2 * LORA + D_ROPE + C
    w_in_p = jnp.concatenate([w_in_f[:, s_a:s_g], w_in_f[:, s_g:], w_in_f[:, s_cq:s_ckv], w_in_f[:, s_ckv:s_kr],
                              w_in_f[:, s_kr:s_a], jnp.zeros((D, LANES - D_ROPE), BF16)], axis=1)

    tm, tn = _tile(T, 1024), _tile(n_in_p, 640)
    h = _matmul("h_proj", x0b, w_in_p, dims=NN, grid=(T // tm, n_in_p // tn, 1),
                a_spec=pl.BlockSpec((tm, D), lambda i, j, k: (i, 0)),
                b_spec=pl.BlockSpec((D, tn), lambda i, j, k: (0, j)), acc_shape=(tm, tn),
                outs=[(_sds((T, n_in_p), F32), pl.BlockSpec((tm, tn), lambda i, j, k: (i, j)))],
                epilogue=_store())[0]

    u, cqn, ckvn, kr = _mid(h, g_cq, g_ckv, tabs, C)
    g_w_uq, g_w_uk, g_w_uv, g_conv_w = _exchange_wait("ag_heads_wait", ag_heads, gather(4), after=cqn)
    w_uq_p = jnp.pad(g_w_uq, ((0, 0), (0, 0), (0, D_HEAD_PAD - D_QK)))
    w_ukv = jnp.concatenate([g_w_uk, g_w_uv], axis=2)
    conv_w_f = jnp.pad(jnp.transpose(g_conv_w, (1, 0, 2)).reshape(CONV_W, C), ((0, CONV_W_PAD - CONV_W), (0, 0)))
    q = _q_proj(cqn, w_uq_p, tabs)
    kf, vf = _kv_proj(ckvn, w_ukv, kr)
    attn, lse = _flash_fwd(q, kf, vf, D)
    conv_c = _conv_fwd(u, conv_w_f, conv_b)
    cat = _conv_post(conv_c, g_conv_ln, b_conv_ln, attn)
    g_w_out, g_w_ff1, g_w_ff2 = _exchange_wait("ag_ff_wait", ag_ff, gather(3), after=cat)
    w_out_f = g_w_out.reshape(D, D)
    w_ff2_f = g_w_ff2.reshape(F, D)

    def ep_ln1(acc, ex, out, first):
        z1 = ALPHA * ex[0][...] + acc
        xhat, _ = _ln_stats(z1)
        x1 = xhat * ex[1][...] + ex[2][...]
        out[0][...] = z1
        out[1][...] = x1
        out[2][...] = x1.astype(BF16)

    tm = _tile(T, 256)
    rowblk = pl.BlockSpec((tm, D), lambda i, j, k: (i, 0))
    vecD = pl.BlockSpec((1, D), lambda i, j, k: (0, 0))
    z1, x1, x1b = _matmul("mix_ln1", cat, w_out_f, dims=NN, grid=(T // tm, 1, 1), a_spec=rowblk,
                          b_spec=pl.BlockSpec((D, D), lambda i, j, k: (0, 0)), acc_shape=(tm, D),
                          extras=[(x0, rowblk), (g_ln1, vecD), (b_ln1, vecD)],
                          outs=[(_sds((T, D), F32), rowblk), (_sds((T, D), F32), rowblk), (_sds((T, D), BF16), rowblk)],
                          epilogue=ep_ln1, ep_rows=EPILOGUE_ROWS)

    def ep_ff1(acc, ex, out, first):
        r = jnp.maximum(acc, 0.0)
        out[0][...] = (r * r).astype(BF16)
        out[1][...] = r.astype(BF16)

    tm, tn = _tile(T, 1024), _tile(Fs, 1024)
    per = Fs // tn
    fblk = pl.BlockSpec((tm, tn), lambda i, j, k: (i, j))
    f_act, r_act = _matmul("ff1_relu2", x1b, g_w_ff1, dims=NN, grid=(T // tm, F // tn, 1),
                           a_spec=pl.BlockSpec((tm, D), lambda i, j, k: (i, 0)),
                           b_spec=pl.BlockSpec((None, D, tn), lambda i, j, k: (j // per, 0, j % per)),
                           acc_shape=(tm, tn), outs=[(_sds((T, F), BF16), fblk), (_sds((T, F), BF16), fblk)],
                           epilogue=ep_ff1)

    def ep_ln2(acc, ex, out, first):
        g2 = ex[2][...]
        z2 = ALPHA * ex[0][...] + acc
        xhat, rstd = _ln_stats(z2)
        err = xhat * g2 + ex[3][...] - ex[1][...]
        part = 0.5 * jnp.sum(jnp.mean(err * err, axis=-1, keepdims=True))
        _acc_out(out[2], jnp.zeros((8, LANES), F32) + part, first)
        dy = err * (1.0 / D)
        _acc_out(out[3], _colsum(dy * xhat), first)
        _acc_out(out[4], _colsum(dy), first)
        dz2 = _ln_bwd(dy, xhat, rstd, g2)
        out[0][...] = dz2
        out[1][...] = dz2.astype(BF16)

    tm, tk = _tile(T, 512), _tile(F, 1024)
    rowblk = pl.BlockSpec((tm, D), lambda i, j, k: (i, 0))
    dz2, dz2b, loss_blk, dg_ln2, db_ln2 = _matmul(
        "ff2_ln2_loss", f_act, w_ff2_f, dims=NN, grid=(T // tm, 1, F // tk),
        a_spec=pl.BlockSpec((tm, tk), lambda i, j, k: (i, k)), b_spec=pl.BlockSpec((tk, D), lambda i, j, k: (k, 0)),
        acc_shape=(tm, D), extras=[(x1, rowblk), (tgt, rowblk), (g_ln2, vecD), (b_ln2, vecD)],
        outs=[(_sds((T, D), F32), rowblk), (_sds((T, D), BF16), rowblk),
              (_sds((8, LANES), F32), pl.BlockSpec((8, LANES), lambda i, j, k: (0, 0))),
              (_sds((1, D), F32), vecD), (_sds((1, D), F32), vecD)],
        epilogue=ep_ln2, ep_rows=EPILOGUE_ROWS)
    loss = lax.psum(loss_blk[0, 0], ("x", "y", "c"))

    def ep_dpre(acc, ex, out, first):
        out[0][...] = (acc * (2.0 * ex[0][...].astype(F32))).astype(BF16)

    tm, tn = _tile(T, 1024), _tile(F, 1024)
    fblk = pl.BlockSpec((tm, tn), lambda i, j, k: (i, j))
    dpre = _matmul("ff2_dgrad", dz2b, w_ff2_f, dims=NT, grid=(T // tm, F // tn, 1),
                   a_spec=pl.BlockSpec((tm, D), lambda i, j, k: (i, 0)), b_spec=pl.BlockSpec((tn, D), lambda i, j, k: (j, 0)),
                   acc_shape=(tm, tn), extras=[(r_act, fblk)], outs=[(_sds((T, F), BF16), fblk)], epilogue=ep_dpre)[0]

    dw_ff2 = _weight_grad("ff2_wgrad", f_act, dz2b).reshape(N_DEV, Fs, D)
    dw_ff1 = _weight_grad("ff1_wgrad", x1b, dpre, stacked_cols=Fs)
    scatter = lambda k_: [True] * k_
    rs_ff = _exchange_start("rs_ff_start", [dw_ff2, dw_ff1], scatter(2))

    def ep_ln1_bwd(acc, ex, out, first):
        dx1 = ALPHA * ex[0][...] + acc
        xhat, rstd = _ln_stats(ex[1][...])
        _acc_out(out[2], _colsum(dx1 * xhat), first)
        _acc_out(out[3], _colsum(dx1), first)
        dz1 = _ln_bwd(dx1, xhat, rstd, ex[2][...])
        out[0][...] = dz1
        out[1][...] = dz1.astype(BF16)

    tm, tk = _tile(T, 512), _tile(Fs, 1024)
    per = Fs // tk
    rowblk = pl.BlockSpec((tm, D), lambda i, j, k: (i, 0))
    dz1, dz1b, dg_ln1, db_ln1 = _matmul(
        "ff1_dgrad_ln1_bwd", dpre, g_w_ff1, dims=NT, grid=(T // tm, 1, F // tk),
        a_spec=pl.BlockSpec((tm, tk), lambda i, j, k: (i, k)),
        b_spec=pl.BlockSpec((None, D, tk), lambda i, j, k: (k // per, 0, k % per)),
        acc_shape=(tm, D), extras=[(dz2, rowblk), (z1, rowblk), (g_ln1 + rs_ff[4], vecD)],
        outs=[(_sds((T, D), F32), rowblk), (_sds((T, D), BF16), rowblk), (_sds((1, D), F32), vecD), (_sds((1, D), F32), vecD)],
        epilogue=ep_ln1_bwd, ep_rows=EPILOGUE_ROWS)

    dw_out = _weight_grad("out_wgrad", cat, dz1b).reshape(N_DEV, D // N_DEV, D)
    rs_out = _exchange_start("rs_out_start", [dw_out], scatter(1))
    tm, tn = _tile(T, 1024), _tile(D, 1024)
    dcat = _matmul("out_dgrad", dz1b, w_out_f, dims=NT, grid=(T // tm, D // tn, 1),
                   a_spec=pl.BlockSpec((tm, D), lambda i, j, k: (i, 0)), b_spec=pl.BlockSpec((tn, D), lambda i, j, k: (j, 0)),
                   acc_shape=(tm, tn), outs=[(_sds((T, D), F32), pl.BlockSpec((tm, tn), lambda i, j, k: (i, j)))],
                   epilogue=_store())[0]

    dc, dg_conv_ln, db_conv_ln = _conv_post_bwd(dcat, conv_c, g_conv_ln + rs_out[4], b_conv_ln)
    du, dconv_w_p, dconv_b = _conv_bwd(dc, u, conv_w_f)
    dh = _glu_bwd(du, h, C)

    delta, do_heads = _attn_delta(dcat, cat)
    dq, dk, dv = _flash_bwd(q, kf, vf, do_heads, lse, delta)
    dq_raw = _dq_post(dq, tabs)
    cq_blk = (2 * C) // LORA
    dkv, dh = _dk_post(dk, dv, tabs, dh, (2 * C + 2 * LORA) // LANES)
    by_rank = lambda w_: jnp.transpose(w_, (1, 0, 2)).reshape(LORA, N_HEADS * D_HEAD_PAD)
    dh, dg_cq = _latent_bwd("q_dgrad_rms_bwd", dq_raw, by_rank(w_uq_p), h, cq_blk, g_cq, dh)
    dh, dg_ckv = _latent_bwd("kv_dgrad_rms_bwd", dkv, by_rank(w_ukv), h, cq_blk + 1, g_ckv, dh)

    dw_in_p = _weight_grad("in_wgrad", x0b, dh, tn_pref=640)
    dw_in_f = jnp.concatenate([dw_in_p[:, 2 * C:2 * C + 2 * LORA + D_ROPE], dw_in_p[:, :2 * C]], axis=1)
    dw_in = jnp.transpose(dw_in_f.reshape(D, N_DEV, n_in // N_DEV), (1, 0, 2))
    rs_in = _exchange_start("rs_in_start", [dw_in], scatter(1))

    def ep_ln_in_bwd(acc, ex, out, first):
        dx0 = ALPHA * ex[0][...] + acc
        xhat, rstd = _ln_stats(ex[1][...])
        _acc_out(out[1], _colsum(dx0 * xhat), first)
        _acc_out(out[2], _colsum(dx0), first)
        out[0][...] = _ln_bwd(dx0, xhat, rstd, ex[2][...])

    tm, tk = _tile(T, 512), _tile(n_in_p, 640)
    rowblk = pl.BlockSpec((tm, D), lambda i, j, k: (i, 0))
    grad_x, dg_ln_in, db_ln_in = _matmul(
        "in_dgrad_ln_in_bwd", dh, w_in_p, dims=NT, grid=(T // tm, 1, n_in_p // tk),
        a_spec=pl.BlockSpec((tm, tk), lambda i, j, k: (i, k)), b_spec=pl.BlockSpec((D, tk), lambda i, j, k: (0, k)),
        acc_shape=(tm, D), extras=[(dz1, rowblk), (xs, rowblk), (row(ln_in_g) + rs_in[4], vecD)],
        outs=[(_sds((T, D), F32), rowblk), (_sds((1, D), F32), vecD), (_sds((1, D), F32), vecD)],
        epilogue=ep_ln_in_bwd, ep_rows=EPILOGUE_ROWS)

    dw_uq = _head_weight_grad("uq_wgrad", cqn, dq_raw)[:, :, :D_QK]
    dw_ukv = _head_weight_grad("ukv_wgrad", ckvn, dkv)
    dw_uk, dw_uv = dw_ukv[:, :, :D_NOPE], dw_ukv[:, :, D_NOPE:]
    dconv_w = jnp.transpose(dconv_w_p[:CONV_W].reshape(CONV_W, N_DEV, C // N_DEV), (1, 0, 2))
    rs_heads = _exchange_start("rs_heads_start", [dw_uq, dw_uk, dw_uv, dconv_w], scatter(4))

    small = dict(ln_in_g=dg_ln_in, ln_in_b=db_ln_in, g_cq=dg_cq, g_ckv=dg_ckv, conv_b=dconv_b, g_conv_ln=dg_conv_ln,
                 b_conv_ln=db_conv_ln, g_ln1=dg_ln1, b_ln1=db_ln1, g_ln2=dg_ln2, b_ln2=db_ln2)
    names = _small_names()
    rs_small = _exchange_start("rs_small_start", [_pack([small[n] for n in names])], gather(1))
    res = {}

    def update(group, parts, after):
        last = after
        for n, p in zip(group, parts):
            outs_n = _adamw("adamw_" + n, p, args[n][0], args["m_" + n][0], args["v_" + n][0])
            res[n] = [o.reshape(args[n].shape) for o in outs_n]
            last = outs_n[0]
        return last

    done = update(["w_ff2", "w_ff1"], _exchange_wait("rs_ff_wait", rs_ff, scatter(2), after=grad_x), grad_x)
    done = update(["w_out"], _exchange_wait("rs_out_wait", rs_out, scatter(1), after=done), done)
    done = update(["w_in"], _exchange_wait("rs_in_wait", rs_in, scatter(1), after=done), done)
    done = update(["w_uq", "w_uk", "w_uv", "conv_w"],
                  _exchange_wait("rs_heads_wait", rs_heads, scatter(4), after=done), done)
    (small_parts,) = _exchange_wait("rs_small_wait", rs_small, gather(1), after=done)
    packed = _adamw("adamw_small", small_parts, _pack([args[n] for n in names]), _pack([args["m_" + n] for n in names]),
                    _pack([args["v_" + n] for n in names]))
    like = [args[n] for n in names]
    unpacked = [_unpack(p, like) for p in packed]
    for i, n in enumerate(names):
        res[n] = [unpacked[kind][i] for kind in range(4)]

    order = ["ln_in_g", "ln_in_b", "w_in", "g_cq", "w_uq", "g_ckv", "w_uk", "w_uv", "conv_w", "conv_b", "g_conv_ln",
             "b_conv_ln", "w_out", "g_ln1", "b_ln1", "w_ff1", "w_ff2", "g_ln2", "b_ln2"]
    outs = [loss, grad_x.reshape(x.shape)]
    for kind in range(4):
        outs += [res[n][kind] for n in order]
    return tuple(outs)
```

```python
import jax
import jax.numpy as jnp
from jax import lax
from jax.experimental import pallas as pl
from jax.experimental.pallas import tpu as pltpu

F32 = jnp.float32
BF16 = jnp.bfloat16

N_HEADS = 8
D_NOPE = 128
D_ROPE = 64
D_V = 128
D_QK = D_NOPE + D_ROPE
D_HEAD_PAD = 256
LORA = 512
CONV_W = 31
CONV_HALF = CONV_W // 2
CONV_W_PAD = 32
HALO = 16
LN_EPS = 1e-5
RMS_EPS = 1e-6
ALPHA = 2.0 ** 0.25
SCALE = float(D_QK) ** -0.5
LOG2_E = 1.4426950408889634
LN_2 = 0.6931471805599453
Q_SCALE = SCALE * LOG2_E
ROPE_BASE = 10000.0
ADAM_LR, ADAM_B1, ADAM_B2, ADAM_EPS, ADAM_WD, ADAM_STEP = 0.001, 0.9, 0.999, 1e-08, 0.01, 10

N_DEV = 8
LANES = 128
VMEM_LIMIT_V7X = 56 * 1024 * 1024

NN = (((1,), (0,)), ((), ()))
NT = (((1,), (1,)), ((), ()))
TN = (((0,), (0,)), ((), ()))


def _call(body, *, name, grid, in_specs, out_specs, out_shape, scratch=(), aliases=None):
    params = pltpu.CompilerParams(dimension_semantics=("arbitrary",) * len(grid),
                                  vmem_limit_bytes=VMEM_LIMIT_V7X)
    return pl.pallas_call(body, name=name, grid=grid, in_specs=in_specs, out_specs=out_specs,
                          out_shape=out_shape, scratch_shapes=scratch, compiler_params=params,
                          input_output_aliases=aliases or {})


def _tile(n, pref):
    if n <= pref:
        return n
    t = (pref // LANES) * LANES
    while t > LANES and n % t:
        t -= LANES
    assert n % t == 0, (n, pref)
    return t


def _sds(shape, dtype):
    return jax.ShapeDtypeStruct(shape, dtype)


def _ln_stats(z):
    mu = jnp.mean(z, axis=-1, keepdims=True)
    zc = z - mu
    var = jnp.mean(zc * zc, axis=-1, keepdims=True)
    rstd = lax.rsqrt(var + LN_EPS)
    return zc * rstd, rstd


def _ln_bwd(dy, xhat, rstd, g):
    gd = dy * g
    m1 = jnp.mean(gd, axis=-1, keepdims=True)
    m2 = jnp.mean(gd * xhat, axis=-1, keepdims=True)
    return rstd * (gd - m1 - xhat * m2)


def _rms(x, g):
    return x * lax.rsqrt(jnp.mean(x * x, axis=-1, keepdims=True) + RMS_EPS) * g


def _rms_bwd(dy, x, g):
    r = lax.rsqrt(jnp.mean(x * x, axis=-1, keepdims=True) + RMS_EPS)
    dxn = dy * g
    dx = r * dxn - x * (r * r * r) * jnp.mean(dxn * x, axis=-1, keepdims=True)
    dg = jnp.sum(dy * x * r, axis=0, keepdims=True)
    return dx, dg


def _sigmoid(x):
    return 1.0 / (1.0 + jnp.exp(-x))


def _rope(x, cos_p, sin_a, sin_b):
    return x * cos_p + pltpu.roll(x, 96, 1) * sin_a + pltpu.roll(x, 32, 1) * sin_b


def _unrope(d, cos_p, sin_a, sin_b):
    return d * cos_p - pltpu.roll(d, 96, 1) * sin_a - pltpu.roll(d, 32, 1) * sin_b


def _colsum(v):
    return jnp.sum(v, axis=0, keepdims=True)


def _acc_out(ref, val, first):
    if first is False:
        ref[...] += val
        return

    @pl.when(first)
    def _():
        ref[...] = val

    @pl.when(jnp.logical_not(first))
    def _():
        ref[...] += val


class _Rows:
    def __init__(self, ref, sl):
        self.ref, self.sl = ref, sl

    def __getitem__(self, idx):
        assert idx is Ellipsis
        return self.ref[self.sl, :]

    def __setitem__(self, idx, val):
        assert idx is Ellipsis
        self.ref[self.sl, :] = val


def _matmul(name, a, b, *, dims, grid, a_spec, b_spec, acc_shape, outs, epilogue, extras=(), ep_rows=None, into=None):
    nk = grid[2]
    ne, no = len(extras), len(outs)
    tm = acc_shape[0]
    n_in = 2 + ne + (0 if into is None else 1)

    def finish(acc_rows, ex, out):
        first = pl.program_id(0) == 0
        if ep_rows is None or ep_rows >= tm:
            epilogue(acc_rows(slice(None)), ex, out, first)
            return
        for r0 in range(0, tm, ep_rows):
            sl = slice(r0, r0 + ep_rows)
            view = lambda r: _Rows(r, sl) if r.shape[0] == tm else r
            epilogue(acc_rows(sl), [view(r) for r in ex], [view(r) for r in out], first if r0 == 0 else False)

    def body(*refs):
        a_ref, b_ref = refs[0], refs[1]
        ex = refs[2:2 + ne]
        out = refs[n_in:n_in + no]
        if nk == 1:
            part = lax.dot_general(a_ref[...], b_ref[...], dims, preferred_element_type=F32)
            finish(lambda sl: part[sl, :], ex, out)
        else:
            acc = refs[n_in + no]
            k = pl.program_id(2)

            @pl.when(k == 0)
            def _():
                acc[...] = jnp.zeros_like(acc)

            acc[...] += lax.dot_general(a_ref[...], b_ref[...], dims, preferred_element_type=F32)

            @pl.when(k == nk - 1)
            def _():
                finish(lambda sl: acc[sl, :], ex, out)

    scratch = [] if nk == 1 else [pltpu.VMEM(acc_shape, F32)]
    ins = [a, b] + [e for e, _ in extras]
    in_specs = [a_spec, b_spec] + [s for _, s in extras]
    aliases = {}
    if into is not None:
        ins.append(into)
        in_specs.append(pl.BlockSpec(memory_space=pl.ANY))
        aliases = {n_in - 1: 0}
    return _call(body, name=name, grid=grid, in_specs=in_specs, out_specs=[s for _, s in outs],
                 out_shape=[o for o, _ in outs], scratch=scratch, aliases=aliases)(*ins)


def _store(dtype=F32):
    def ep(acc, ex, out, first):
        out[0][...] = acc.astype(dtype)
    return ep


def _mesh_pos():
    return lax.axis_index("x"), lax.axis_index("y"), lax.axis_index("c")


def _flip(v, bit):
    return 1 - v if bit else v


_HBM = pl.BlockSpec(memory_space=pltpu.HBM)
_SEM = pl.BlockSpec(memory_space=pltpu.SEMAPHORE)
_EFFECT = pltpu.SideEffectType.DATAFLOW_SIDE_EFFECTING


def _my_slot():
    x, y, c = _mesh_pos()
    return 4 * x + 2 * y + c


def _exchange_copies(srcs, lands, send_sems, recv_sems, stacked, receives=True):
    x, y, c = _mesh_pos()
    me = 4 * x + 2 * y + c
    pairs = []
    for w in range(len(srcs)):
        for k in range(1, N_DEV):
            peer = (_flip(x, k & 4), _flip(y, k & 2), _flip(c, k & 1))
            peer_slot = 4 * peer[0] + 2 * peer[1] + peer[2]
            to_peer = srcs[w].at[peer_slot] if stacked[w] else srcs[w]
            mine = srcs[w].at[me] if stacked[w] else srcs[w]
            s = w * (N_DEV - 1) + k - 1
            sems = dict(send_sem=send_sems.at[s], recv_sem=recv_sems.at[s],
                        device_id=peer, device_id_type=pl.DeviceIdType.MESH)
            send = pltpu.make_async_remote_copy(src_ref=to_peer, dst_ref=lands[w].at[me], **sems)
            recv = pltpu.make_async_remote_copy(src_ref=mine, dst_ref=lands[w].at[peer_slot], **sems) if receives else None
            pairs.append((send, recv))
    return pairs


def _exchange_start(name, srcs, stacked):
    n = len(srcs)
    land_shapes = [s.shape if st else (N_DEV,) + s.shape for s, st in zip(srcs, stacked)]

    def body(*refs):
        src, land = refs[:n], refs[n:2 * n]
        send_sems, recv_sems = refs[2 * n], refs[2 * n + 1]
        token = refs[-1]
        for send, _ in _exchange_copies(src, land, send_sems, recv_sems, stacked, receives=False):
            send.start()
        token[...] = jnp.zeros_like(token)

    hbm = lambda a: pltpu.with_memory_space_constraint(a, pltpu.HBM)
    outs = pl.pallas_call(
        body, name=name,
        out_shape=(pltpu.SemaphoreType.DMA((n * (N_DEV - 1),)), pltpu.SemaphoreType.DMA((n * (N_DEV - 1),)),
                   *[pltpu.HBM(s.shape, s.dtype) for s in srcs],
                   *[pltpu.HBM(ls, s.dtype) for ls, s in zip(land_shapes, srcs)],
                   _sds((8, LANES), F32)),
        in_specs=[_HBM] * (2 * n),
        out_specs=(_SEM, _SEM, *[_HBM] * (2 * n), pl.BlockSpec(memory_space=pltpu.VMEM)),
        input_output_aliases={i: 2 + i for i in range(2 * n)},
        compiler_params=pltpu.CompilerParams(has_side_effects=_EFFECT),
    )(*[hbm(s) for s in srcs], *[hbm(lax.empty(ls, s.dtype)) for ls, s in zip(land_shapes, srcs)])
    return outs[0], outs[1], list(outs[2:2 + n]), list(outs[2 + n:2 + 2 * n]), outs[-1][0, 0]


def _exchange_wait(name, started, stacked, after):
    srcs, lands = _wait_call(name, started, stacked, after)
    me = _my_slot()
    full = []
    for src, land, st in zip(srcs, lands, stacked):
        own = lax.dynamic_index_in_dim(src, me, 0, keepdims=True) if st else src[None]
        full.append(lax.dynamic_update_index_in_dim(land, own, me, 0))
    return full


def _wait_call(name, started, stacked, after):
    send_sems, recv_sems, srcs, lands, _ = started
    n = len(srcs)

    def body(*refs):
        src, land = refs[:n], refs[n:2 * n]
        s_sems, r_sems = refs[2 * n], refs[2 * n + 1]
        for send, recv in _exchange_copies(src, land, s_sems, r_sems, stacked):
            send.wait_send()
            recv.wait_recv()

    outs = pl.pallas_call(
        body, name=name,
        out_shape=tuple(pltpu.HBM(a.shape, a.dtype) for a in srcs + lands),
        in_specs=[_HBM] * (2 * n) + [_SEM, _SEM, pl.BlockSpec(memory_space=pl.ANY)],
        out_specs=[_HBM] * (2 * n),
        input_output_aliases={i: i for i in range(2 * n)},
        compiler_params=pltpu.CompilerParams(has_side_effects=_EFFECT),
    )(*srcs, *lands, send_sems, recv_sems, after)
    return outs[:n], outs[n:]


def _adamw(name, parts, w, m, v):
    rows, cols = w.shape
    cap = max(8, (LANES * 1024) // cols)
    tr = rows
    if rows > cap:
        tr = (cap // 8) * 8
        while rows % tr:
            tr -= 8
    c1 = 1.0 / (1.0 - ADAM_B1 ** ADAM_STEP)
    c2 = 1.0 / (1.0 - ADAM_B2 ** ADAM_STEP)

    def body(p_ref, w_ref, m_ref, v_ref, g_o, d_o, m_o, v_o):
        g = p_ref[0]
        for s in range(1, N_DEV):
            g = g + p_ref[s]
        mn = ADAM_B1 * m_ref[...] + (1.0 - ADAM_B1) * g
        vn = ADAM_B2 * v_ref[...] + (1.0 - ADAM_B2) * (g * g)
        g_o[...] = g
        m_o[...] = mn
        v_o[...] = vn
        d_o[...] = -ADAM_LR * ((mn * c1) / (jnp.sqrt(vn * c2) + ADAM_EPS) + ADAM_WD * w_ref[...])

    blk = pl.BlockSpec((tr, cols), lambda i: (i, 0))
    return _call(body, name=name, grid=(rows // tr,),
                 in_specs=[pl.BlockSpec((N_DEV, tr, cols), lambda i: (0, i, 0)), blk, blk, blk],
                 out_specs=[blk] * 4, out_shape=[_sds((rows, cols), F32)] * 4)(parts, w, m, v)


def _rope_tables(pos_col, inv_freq):
    T = pos_col.shape[0]
    tm = _tile(T, 1024)

    def body(p_ref, f_ref, c_o, sa_o, sb_o):
        ang = p_ref[...].astype(F32) * f_ref[...]
        lane = lax.broadcasted_iota(jnp.int32, ang.shape, 1)
        cs, sn = jnp.cos(ang), jnp.sin(ang)
        c_o[...] = jnp.where(lane < D_ROPE, cs, 0.0)
        sa_o[...] = jnp.where(lane < D_ROPE // 2, -sn, 0.0)
        sb_o[...] = jnp.where((lane >= D_ROPE // 2) & (lane < D_ROPE), sn, 0.0)

    blk = pl.BlockSpec((tm, LANES), lambda i: (i, 0))
    return _call(body, name="rope_tables", grid=(T // tm,),
                 in_specs=[pl.BlockSpec((tm, 1), lambda i: (i, 0)), pl.BlockSpec((1, LANES), lambda i: (0, 0))],
                 out_specs=[blk] * 3, out_shape=[_sds((T, LANES), F32)] * 3)(pos_col, inv_freq)


def _ln_in(x, g, b):
    T, D = x.shape
    tm = _tile(T, 512)

    def body(x_ref, g_ref, b_ref, o32, o16):
        xhat, _ = _ln_stats(x_ref[...])
        y = xhat * g_ref[...] + b_ref[...]
        o32[...] = y
        o16[...] = y.astype(BF16)

    blk = pl.BlockSpec((tm, D), lambda i: (i, 0))
    vec = pl.BlockSpec((1, D), lambda i: (0, 0))
    return _call(body, name="ln_in", grid=(T // tm,), in_specs=[blk, vec, vec], out_specs=[blk, blk],
                 out_shape=[_sds((T, D), F32), _sds((T, D), BF16)])(x, g, b)


def _mid(h, g_cq, g_ckv, tabs, C):
    T = h.shape[0]
    tm = _tile(T, 256)
    cq_blk, kr_blk = (2 * C) // LORA, (2 * C + 2 * LORA) // LANES

    def body(a_ref, gt_ref, cq_ref, ckv_ref, kr_ref, gq_ref, gkv_ref, cp, sa, sb, u_o, cqn_o, ckvn_o, kr_o):
        u_o[...] = a_ref[...] * _sigmoid(gt_ref[...])
        cqn_o[...] = _rms(cq_ref[...], gq_ref[...]).astype(BF16)
        ckvn_o[...] = _rms(ckv_ref[...], gkv_ref[...]).astype(BF16)
        kr_o[...] = _rope(kr_ref[...], cp[...], sa[...], sb[...]).astype(BF16)

    def col(w, j):
        return pl.BlockSpec((tm, w), lambda i: (i, j))

    vec = pl.BlockSpec((1, LORA), lambda i: (0, 0))
    return _call(body, name="mid_norm_glu", grid=(T // tm,),
                 in_specs=[col(C, 0), col(C, 1), col(LORA, cq_blk), col(LORA, cq_blk + 1), col(LANES, kr_blk),
                           vec, vec, col(LANES, 0), col(LANES, 0), col(LANES, 0)],
                 out_specs=[col(C, 0), col(LORA, 0), col(LORA, 0), col(LANES, 0)],
                 out_shape=[_sds((T, C), F32), _sds((T, LORA), BF16), _sds((T, LORA), BF16), _sds((T, LANES), BF16)],
                 )(h, h, h, h, h, g_cq, g_ckv, *tabs)


def _q_proj(cqn, w_uq_p, tabs):
    T = cqn.shape[0]
    tm = _tile(T, HEAD_ROWS)

    def body(c_ref, w_ref, cp, sa, sb, o_ref):
        q = jnp.dot(c_ref[...], w_ref[...], preferred_element_type=F32)
        o_ref[:, :D_NOPE] = (q[:, :D_NOPE] * Q_SCALE).astype(BF16)
        o_ref[:, D_NOPE:] = (_rope(q[:, D_NOPE:], cp[...], sa[...], sb[...]) * Q_SCALE).astype(BF16)

    tab = pl.BlockSpec((tm, LANES), lambda i, h: (i, 0))
    return _call(body, name="q_proj_rope", grid=(T // tm, N_HEADS),
                 in_specs=[pl.BlockSpec((tm, LORA), lambda i, h: (i, 0)),
                           pl.BlockSpec((None, LORA, D_HEAD_PAD), lambda i, h: (h, 0, 0)), tab, tab, tab],
                 out_specs=pl.BlockSpec((None, tm, D_HEAD_PAD), lambda i, h: (h, i, 0)),
                 out_shape=_sds((N_HEADS, T, D_HEAD_PAD), BF16))(cqn, w_uq_p, *tabs)


def _kv_proj(ckvn, w_ukv, kr):
    T = ckvn.shape[0]
    tm = _tile(T, HEAD_ROWS)

    def body(c_ref, w_ref, kr_ref, k_o, v_o):
        kv = jnp.dot(c_ref[...], w_ref[...], preferred_element_type=F32)
        k_o[:, :D_NOPE] = kv[:, :D_NOPE].astype(BF16)
        k_o[:, D_NOPE:] = kr_ref[...]
        v_o[:, :D_V] = kv[:, D_NOPE:].astype(BF16)
        v_o[:, D_V:] = jnp.ones((tm, D_V), BF16)

    return _call(body, name="kv_proj", grid=(T // tm, N_HEADS),
                 in_specs=[pl.BlockSpec((tm, LORA), lambda i, h: (i, 0)),
                           pl.BlockSpec((None, LORA, D_NOPE + D_V), lambda i, h: (h, 0, 0)),
                           pl.BlockSpec((tm, LANES), lambda i, h: (i, 0))],
                 out_specs=[pl.BlockSpec((None, tm, D_HEAD_PAD), lambda i, h: (h, i, 0)),
                            pl.BlockSpec((None, tm, 2 * D_V), lambda i, h: (h, i, 0))],
                 out_shape=[_sds((N_HEADS, T, D_HEAD_PAD), BF16), _sds((N_HEADS, T, 2 * D_V), BF16)])(ckvn, w_ukv, kr)


def _flash_fwd(q, k, v1, out_cols):
    _, T, _ = q.shape
    tq, tk = _tile(T, FLASH_TQ), _tile(T, FLASH_TK)
    nkv, reps = T // tk, tk // LANES

    def body(q_ref, k_ref, v_ref, o_ref, lse_ref, m_sc, acc_sc):
        m_sc[...] = jnp.full_like(m_sc, -jnp.inf)
        acc_sc[...] = jnp.zeros_like(acc_sc)
        qv = q_ref[...]

        def rows(j):
            return pl.ds(pl.multiple_of(j * tk, tk), tk)

        def scores(j):
            return lax.dot_general(qv, k_ref[rows(j), :], NT, preferred_element_type=F32)

        def update(s, j):
            m_prev = m_sc[...]
            m_new = jnp.maximum(m_prev, jnp.max(s, axis=1, keepdims=True))
            a = jnp.exp2(m_prev - m_new)
            p = jnp.exp2(s - jnp.tile(m_new, (1, reps)))
            pv = jnp.dot(p.astype(BF16), v_ref[rows(j), :], preferred_element_type=F32)
            acc_sc[...] = jnp.tile(a, (1, 2)) * acc_sc[...] + pv
            m_sc[...] = m_new

        def step(j, carry):
            update(scores(j), j)
            return carry

        lax.fori_loop(0, nkv, step, 0, unroll=FLASH_UNROLL_FWD if nkv % FLASH_UNROLL_FWD == 0 else 1)
        acc = acc_sc[...]
        l = acc[:, D_V:]
        o_ref[...] = (acc[:, :D_V] / l).astype(BF16)
        lse_ref[...] = m_sc[...] + jnp.log(l) * LOG2_E

    return _call(body, name="flash_fwd", grid=(N_HEADS, T // tq),
                 in_specs=[pl.BlockSpec((None, tq, D_HEAD_PAD), lambda h, i: (h, i, 0)),
                           pl.BlockSpec((None, T, D_HEAD_PAD), lambda h, i: (h, 0, 0)),
                           pl.BlockSpec((None, T, 2 * D_V), lambda h, i: (h, 0, 0))],
                 out_specs=[pl.BlockSpec((tq, D_V), lambda h, i: (i, h)),
                            pl.BlockSpec((None, tq, LANES), lambda h, i: (h, i, 0))],
                 out_shape=[_sds((T, out_cols), BF16), _sds((N_HEADS, T, LANES), F32)],
                 scratch=[pltpu.VMEM((tq, LANES), F32), pltpu.VMEM((tq, 2 * D_V), F32)])(q, k, v1)


def _halo_specs(tm, cb, n_t):
    r = tm // HALO
    return [pl.BlockSpec((HALO, cb), lambda jc, i: (jnp.maximum(i * r - 1, 0), jc)),
            pl.BlockSpec((tm, cb), lambda jc, i: (i, jc)),
            pl.BlockSpec((HALO, cb), lambda jc, i: (jnp.minimum((i + 1) * r, n_t * r - 1), jc))]


def _fill_ext(ext, prev_ref, cur_ref, next_ref, i, n_t, tm):
    ext[0:HALO, :] = jnp.where(i > 0, prev_ref[...], 0.0)
    ext[HALO:HALO + tm, :] = cur_ref[...]
    ext[HALO + tm:, :] = jnp.where(i < n_t - 1, next_ref[...], 0.0)


HEAD_ROWS = 2048
FLASH_TQ = 512
FLASH_TK = 512
FLASH_UNROLL_FWD = 16
FLASH_UNROLL_BWD = 8
EPILOGUE_ROWS = 128
CONV_ROWS = 64


def _conv_fwd(u, w_pad, bias):
    T, C = u.shape
    tm, cb = _tile(T, 256), _tile(C, 256)
    n_t = T // tm
    rb = min(CONV_ROWS, tm)

    def body(up, uc, un, w_ref, b_ref, c_o, ext):
        i = pl.program_id(1)
        _fill_ext(ext, up, uc, un, i, n_t, tm)
        for r0 in range(0, tm, rb):
            acc = jnp.zeros((rb, cb), F32) + b_ref[...]
            for k in range(CONV_W):
                acc = acc + w_ref[k:k + 1, :] * ext[r0 + k + 1:r0 + k + 1 + rb, :]
            c_o[r0:r0 + rb, :] = acc

    return _call(body, name="conv_fwd", grid=(C // cb, n_t),
                 in_specs=_halo_specs(tm, cb, n_t) + [pl.BlockSpec((CONV_W_PAD, cb), lambda jc, i: (0, jc)),
                                                      pl.BlockSpec((1, cb), lambda jc, i: (0, jc))],
                 out_specs=pl.BlockSpec((tm, cb), lambda jc, i: (i, jc)),
                 out_shape=_sds((T, C), F32),
                 scratch=[pltpu.VMEM((tm + 2 * HALO, cb), F32)])(u, u, u, w_pad, bias)


def _conv_post(c, g, b, cat):
    T, C = c.shape
    assert cat.shape == (T, 2 * C)
    tm = _tile(T, 512)

    def body(c_ref, g_ref, b_ref, cat_ref, o_ref):
        xhat, _ = _ln_stats(c_ref[...])
        y = xhat * g_ref[...] + b_ref[...]
        o_ref[...] = (y * _sigmoid(y)).astype(BF16)

    blk = pl.BlockSpec((tm, C), lambda i: (i, 0))
    vec = pl.BlockSpec((1, C), lambda i: (0, 0))
    return _call(body, name="conv_ln_silu", grid=(T // tm,),
                 in_specs=[blk, vec, vec, pl.BlockSpec(memory_space=pl.ANY)],
                 out_specs=pl.BlockSpec((tm, C), lambda i: (i, 1)),
                 out_shape=_sds(cat.shape, BF16), aliases={3: 0})(c, g, b, cat)


def _conv_post_bwd(dcat, c, g, b):
    T, C = c.shape
    tm = _tile(T, 512)

    def body(d_ref, c_ref, g_ref, b_ref, dc_o, dg_o, db_o):
        first = pl.program_id(0) == 0
        xhat, rstd = _ln_stats(c_ref[...])
        y = xhat * g_ref[...] + b_ref[...]
        sg = _sigmoid(y)
        dy = d_ref[...] * (sg * (1.0 + y * (1.0 - sg)))
        _acc_out(dg_o, _colsum(dy * xhat), first)
        _acc_out(db_o, _colsum(dy), first)
        dc_o[...] = _ln_bwd(dy, xhat, rstd, g_ref[...])

    blk = pl.BlockSpec((tm, C), lambda i: (i, 0))
    vec = pl.BlockSpec((1, C), lambda i: (0, 0))
    return _call(body, name="conv_ln_silu_bwd", grid=(T // tm,),
                 in_specs=[pl.BlockSpec((tm, C), lambda i: (i, 1)), blk, vec, vec],
                 out_specs=[blk, vec, vec],
                 out_shape=[_sds((T, C), F32), _sds((1, C), F32), _sds((1, C), F32)])(dcat, c, g, b)


def _conv_bwd(dc, u, w_pad):
    T, C = u.shape
    tm, cb = _tile(T, 256), _tile(C, 256)
    n_t = T // tm
    rb = min(CONV_ROWS, tm)

    def body(dp, dcur, dn, up, uc, un, w_ref, du_o, dw_o, db_o, dext, uext, dw_sc):
        i = pl.program_id(1)
        _fill_ext(dext, dp, dcur, dn, i, n_t, tm)
        _fill_ext(uext, up, uc, un, i, n_t, tm)

        @pl.when(i == 0)
        def _():
            dw_sc[...] = jnp.zeros_like(dw_sc)

        for r0 in range(0, tm, rb):
            acc = jnp.zeros((rb, cb), F32)
            d_here = dcur[r0:r0 + rb, :]
            for k in range(CONV_W):
                acc = acc + w_ref[k:k + 1, :] * dext[r0 + 2 * HALO - 1 - k:r0 + 2 * HALO - 1 - k + rb, :]
                prod = d_here * uext[r0 + k + 1:r0 + k + 1 + rb, :]
                dw_sc[k] += jnp.sum(prod.reshape(rb // 8, 8, cb), axis=0)
            du_o[r0:r0 + rb, :] = acc
        dw_sc[CONV_W] += jnp.sum(dcur[...].reshape(tm // 8, 8, cb), axis=0)

        @pl.when(i == n_t - 1)
        def _():
            red = jnp.sum(dw_sc[...], axis=1)
            row = lax.broadcasted_iota(jnp.int32, red.shape, 0)
            dw_o[...] = jnp.where(row < CONV_W, red, 0.0)
            db_o[...] = jnp.sum(jnp.where(row == CONV_W, red, 0.0), axis=0, keepdims=True)

    return _call(body, name="conv_bwd", grid=(C // cb, n_t),
                 in_specs=_halo_specs(tm, cb, n_t) + _halo_specs(tm, cb, n_t)
                 + [pl.BlockSpec((CONV_W_PAD, cb), lambda jc, i: (0, jc))],
                 out_specs=[pl.BlockSpec((tm, cb), lambda jc, i: (i, jc)),
                            pl.BlockSpec((CONV_W_PAD, cb), lambda jc, i: (0, jc)),
                            pl.BlockSpec((1, cb), lambda jc, i: (0, jc))],
                 out_shape=[_sds((T, C), F32), _sds((CONV_W_PAD, C), F32), _sds((1, C), F32)],
                 scratch=[pltpu.VMEM((tm + 2 * HALO, cb), F32), pltpu.VMEM((tm + 2 * HALO, cb), F32),
                          pltpu.VMEM((CONV_W_PAD, 8, cb), F32)])(dc, dc, dc, u, u, u, w_pad)


def _glu_bwd(du, h, C):
    T = du.shape[0]
    tm = _tile(T, 512)

    def body(du_ref, a_ref, gt_ref, o_ref):
        sg = _sigmoid(gt_ref[...])
        du_v = du_ref[...]
        o_ref[:, :C] = (du_v * sg).astype(BF16)
        o_ref[:, C:] = (du_v * a_ref[...] * sg * (1.0 - sg)).astype(BF16)

    return _call(body, name="glu_bwd", grid=(T // tm,),
                 in_specs=[pl.BlockSpec((tm, C), lambda i: (i, 0)), pl.BlockSpec((tm, C), lambda i: (i, 0)),
                           pl.BlockSpec((tm, C), lambda i: (i, 1))],
                 out_specs=pl.BlockSpec((tm, 2 * C), lambda i: (i, 0)),
                 out_shape=_sds(h.shape, BF16))(du, h, h)


def _attn_delta(dcat, attn):
    T = attn.shape[0]
    tm = _tile(T, HEAD_ROWS)

    def body(d_ref, o_ref, dl_o, dob_o):
        d = d_ref[...]
        dl = jnp.sum(d * o_ref[...].astype(F32), axis=1, keepdims=True)
        dl_o[...] = jnp.broadcast_to(dl, (tm, LANES))
        dob_o[...] = d.astype(BF16)

    blk = pl.BlockSpec((tm, D_V), lambda i, h: (i, h))
    hblk = pl.BlockSpec((None, tm, D_V), lambda i, h: (h, i, 0))
    return _call(body, name="attn_delta", grid=(T // tm, N_HEADS), in_specs=[blk, blk], out_specs=[hblk, hblk],
                 out_shape=[_sds((N_HEADS, T, LANES), F32), _sds((N_HEADS, T, D_V), BF16)])(dcat, attn)


def _flash_bwd(q, k, v1, do, lse, delta):
    _, T, _ = q.shape
    tq, tk = _tile(T, FLASH_TQ), _tile(T, FLASH_TK)
    nkv, reps = T // tk, tk // LANES

    def body(q_ref, do_ref, lse_ref, dl_ref, k_ref, v_ref, dq_o, dk_o, dv_o, dq_sc):
        @pl.when(pl.program_id(1) == 0)
        def _():
            dk_o[...] = jnp.zeros_like(dk_o)
            dv_o[...] = jnp.zeros_like(dv_o)

        qv, dov = q_ref[...], do_ref[...]
        lse_t = jnp.tile(lse_ref[...], (1, reps))
        dl_t = jnp.tile(dl_ref[...], (1, reps))
        dq_sc[...] = jnp.zeros_like(dq_sc)

        def rows(j):
            return pl.ds(pl.multiple_of(j * tk, tk), tk)

        def scores(j):
            s = lax.dot_general(qv, k_ref[rows(j), :], NT, preferred_element_type=F32)
            dp = lax.dot_general(dov, v_ref[rows(j), :D_V], NT, preferred_element_type=F32)
            return s, dp

        def update(s_dp, j):
            s, dp = s_dp
            p = jnp.exp2(s - lse_t)
            ds = (p * (dp - dl_t)).astype(BF16)
            dv_o[rows(j), :] += lax.dot_general(p.astype(BF16), dov, TN, preferred_element_type=F32)
            dk_o[rows(j), :] += lax.dot_general(ds, qv, TN, preferred_element_type=F32)
            dq_sc[...] += jnp.dot(ds, k_ref[rows(j), :], preferred_element_type=F32)

        def step(j, carry):
            update(scores(j), j)
            return carry

        lax.fori_loop(0, nkv, step, 0, unroll=FLASH_UNROLL_BWD if nkv % FLASH_UNROLL_BWD == 0 else 1)
        dq_o[...] = dq_sc[...]

    def tile(w):
        return pl.BlockSpec((None, tq, w), lambda h, i: (h, i, 0))

    def whole(w):
        return pl.BlockSpec((None, T, w), lambda h, i: (h, 0, 0))

    return _call(body, name="flash_bwd", grid=(N_HEADS, T // tq),
                 in_specs=[tile(D_HEAD_PAD), tile(D_V), tile(LANES), tile(LANES), whole(D_HEAD_PAD), whole(2 * D_V)],
                 out_specs=[tile(D_HEAD_PAD), whole(D_HEAD_PAD), whole(D_V)],
                 out_shape=[_sds((N_HEADS, T, D_HEAD_PAD), F32), _sds((N_HEADS, T, D_HEAD_PAD), F32),
                            _sds((N_HEADS, T, D_V), F32)],
                 scratch=[pltpu.VMEM((tq, D_HEAD_PAD), F32)])(q, do, lse, delta, k, v1)


def _dq_post(dq, tabs):
    _, T, _ = dq.shape
    tm = _tile(T, HEAD_ROWS)

    def body(d_ref, cp, sa, sb, o_ref):
        d = d_ref[...] * SCALE
        o_ref[:, :D_NOPE] = d[:, :D_NOPE].astype(BF16)
        o_ref[:, D_NOPE:] = _unrope(d[:, D_NOPE:], cp[...], sa[...], sb[...]).astype(BF16)

    blk = pl.BlockSpec((None, tm, D_HEAD_PAD), lambda i, h: (h, i, 0))
    tab = pl.BlockSpec((tm, LANES), lambda i, h: (i, 0))
    return _call(body, name="dq_unrope", grid=(T // tm, N_HEADS), in_specs=[blk, tab, tab, tab],
                 out_specs=pl.BlockSpec((tm, D_HEAD_PAD), lambda i, h: (i, h)),
                 out_shape=_sds((T, N_HEADS * D_HEAD_PAD), BF16))(dq, *tabs)


def _dk_post(dk, dv, tabs, dh, kr_blk):
    _, T, _ = dk.shape
    tm = _tile(T, 1024)

    def body(dk_ref, dv_ref, cp, sa, sb, dh_ref, dkv_o, dkr_o, sc):
        h = pl.program_id(1)
        d = dk_ref[...] * LN_2
        dkv_o[:, :D_NOPE] = d[:, :D_NOPE].astype(BF16)
        dkv_o[:, D_NOPE:] = dv_ref[...].astype(BF16)

        @pl.when(h == 0)
        def _():
            sc[...] = d[:, D_NOPE:]

        @pl.when(h > 0)
        def _():
            sc[...] += d[:, D_NOPE:]

        @pl.when(h == N_HEADS - 1)
        def _():
            dkr_o[...] = _unrope(sc[...], cp[...], sa[...], sb[...]).astype(BF16)

    tab = pl.BlockSpec((tm, LANES), lambda i, h: (i, 0))
    return _call(body, name="dk_unrope", grid=(T // tm, N_HEADS),
                 in_specs=[pl.BlockSpec((None, tm, D_HEAD_PAD), lambda i, h: (h, i, 0)),
                           pl.BlockSpec((None, tm, D_V), lambda i, h: (h, i, 0)), tab, tab, tab,
                           pl.BlockSpec(memory_space=pl.ANY)],
                 out_specs=[pl.BlockSpec((tm, D_HEAD_PAD), lambda i, h: (i, h)),
                            pl.BlockSpec((tm, LANES), lambda i, h: (i, kr_blk))],
                 out_shape=[_sds((T, N_HEADS * D_HEAD_PAD), BF16), _sds(dh.shape, BF16)],
                 scratch=[pltpu.VMEM((tm, LANES), F32)], aliases={5: 1})(dk, dv, *tabs, dh)


def _latent_bwd(name, dproj, w_heads, h, col_blk, g, dh):
    T, HD = dproj.shape
    tm = _tile(T, 1024)

    def ep(acc, ex, out, first):
        dx, dg = _rms_bwd(acc, ex[0][...], ex[1][...])
        out[0][...] = dx.astype(BF16)
        _acc_out(out[1], dg, first)

    return _matmul(name, dproj, w_heads, dims=NT, grid=(T // tm, 1, 1),
                   a_spec=pl.BlockSpec((tm, HD), lambda i, j, k: (i, 0)),
                   b_spec=pl.BlockSpec((LORA, HD), lambda i, j, k: (0, 0)),
                   acc_shape=(tm, LORA),
                   extras=[(h, pl.BlockSpec((tm, LORA), lambda i, j, k: (i, col_blk))),
                           (g, pl.BlockSpec((1, LORA), lambda i, j, k: (0, 0)))],
                   outs=[(_sds(dh.shape, BF16), pl.BlockSpec((tm, LORA), lambda i, j, k: (i, col_blk))),
                         (_sds((1, LORA), F32), pl.BlockSpec((1, LORA), lambda i, j, k: (0, 0)))],
                   epilogue=ep, into=dh)


def _head_weight_grad(name, latent, dproj):
    T = dproj.shape[0]
    tk = _tile(T, 2048)
    return _matmul(name, latent, dproj, dims=TN, grid=(N_HEADS, 1, T // tk),
                   a_spec=pl.BlockSpec((tk, LORA), lambda i, j, k: (k, 0)),
                   b_spec=pl.BlockSpec((tk, D_HEAD_PAD), lambda i, j, k: (k, i)),
                   acc_shape=(LORA, D_HEAD_PAD),
                   outs=[(_sds((N_HEADS, LORA, D_HEAD_PAD), F32),
                          pl.BlockSpec((None, LORA, D_HEAD_PAD), lambda i, j, k: (i, 0, 0)))],
                   epilogue=_store())[0]


def _weight_grad(name, a, b, tm_pref=1024, tn_pref=1024, stacked_cols=None):
    T, M = a.shape
    N = b.shape[1]
    tk = _tile(T, 2048)
    tm = _tile(M, tm_pref)
    if stacked_cols is None:
        tn = _tile(N, tn_pref)
        out = (_sds((M, N), F32), pl.BlockSpec((tm, tn), lambda i, j, k: (i, j)))
    else:
        tn = _tile(stacked_cols, tn_pref)
        per = stacked_cols // tn
        out = (_sds((N // stacked_cols, M, stacked_cols), F32),
               pl.BlockSpec((None, tm, tn), lambda i, j, k: (j // per, i, j % per)))
    return _matmul(name, a, b, dims=TN, grid=(M // tm, N // tn, T // tk),
                   a_spec=pl.BlockSpec((tk, tm), lambda i, j, k: (k, i)),
                   b_spec=pl.BlockSpec((tk, tn), lambda i, j, k: (k, j)),
                   acc_shape=(tm, tn), outs=[out], epilogue=_store())[0]


def _small_names():
    return ["ln_in_g", "ln_in_b", "g_cq", "g_ckv", "conv_b", "g_conv_ln", "b_conv_ln", "g_ln1", "b_ln1", "g_ln2", "b_ln2"]


def _pack(vecs):
    flat = jnp.concatenate([v.reshape(-1) for v in vecs])
    assert flat.shape[0] % (8 * LANES) == 0
    return flat.reshape(-1, LANES)


def _unpack(packed, like):
    flat, out, off = packed.reshape(-1), [], 0
    for v in like:
        out.append(flat[off:off + v.size].reshape(v.shape))
        off += v.size
    return out


def kernel(x, positions, ln_in_g, ln_in_b, w_in, g_cq, w_uq, g_ckv, w_uk, w_uv, conv_w, conv_b, g_conv_ln, b_conv_ln, w_out, g_ln1, b_ln1, w_ff1, w_ff2, g_ln2, b_ln2, loss_target, m_ln_in_g, m_ln_in_b, m_w_in, m_g_cq, m_w_uq, m_g_ckv, m_w_uk, m_w_uv, m_conv_w, m_conv_b, m_g_conv_ln, m_b_conv_ln, m_w_out, m_g_ln1, m_b_ln1, m_w_ff1, m_w_ff2, m_g_ln2, m_b_ln2, v_ln_in_g, v_ln_in_b, v_w_in, v_g_cq, v_w_uq, v_g_ckv, v_w_uk, v_w_uv, v_conv_w, v_conv_b, v_g_conv_ln, v_b_conv_ln, v_w_out, v_g_ln1, v_b_ln1, v_w_ff1, v_w_ff2, v_g_ln2, v_b_ln2):
    args = dict(locals())
    T, D = x.shape[1], x.shape[2]
    C = D - N_HEADS * D_V
    Fs = w_ff1.shape[2]
    F = N_DEV * Fs
    n_in = N_DEV * w_in.shape[2]
    n_in_p = 2 * C + 2 * LORA + LANES
    assert n_in == 2 * LORA + D_ROPE + 2 * C and w_uq.shape[2] == D_QK and conv_w.shape[2] * N_DEV == C

    xs, tgt = x[0], loss_target[0]
    row = lambda v_: v_.reshape(1, -1)

    gather = lambda k_: [False] * k_
    ag_in = _exchange_start("ag_in_start", [w_in[0].astype(BF16).reshape(-1, LANES)], gather(1))
    ag_heads = _exchange_start("ag_heads_start", [w_uq[0].astype(BF16), w_uk[0].astype(BF16), w_uv[0].astype(BF16),
                                                  conv_w[0]], gather(4))
    ag_ff = _exchange_start("ag_ff_start", [w_out[0].astype(BF16), w_ff1[0].astype(BF16), w_ff2[0].astype(BF16)],
                            gather(3))
    started = ag_in[4] + ag_heads[4] + ag_ff[4]

    half = D_ROPE // 2
    inv_freq = ROPE_BASE ** (-jnp.arange(half, dtype=F32) * (2.0 / D_ROPE))
    inv_freq = jnp.tile(inv_freq, LANES // half).reshape(1, LANES)
    tabs = _rope_tables(positions.reshape(T, 1), inv_freq)

    x0, x0b = _ln_in(xs, row(ln_in_g) + started, row(ln_in_b))

    (g_w_in,) = _exchange_wait("ag_in_wait", ag_in, gather(1), after=x0b)
    w_in_f = jnp.transpose(g_w_in.reshape(N_DEV, D, n_in // N_DEV), (1, 0, 2)).reshape(D, n_in)
    s_cq, s_ckv, s_kr, s_a, s_g = 0, LORA, 2 * LORA, 2 * LORA + D_ROPE, 2 * LORA + D_ROPE + C
    w_in_p = jnp.concatenate([w_in_f[:, s_a:s_g], w_in_f[:, s_g:], w_in_f[:, s_cq:s_ckv], w_in_f[:, s_ckv:s_kr],
                              w_in_f[:, s_kr:s_a], jnp.zeros((D, LANES - D_ROPE), BF16)], axis=1)

    tm, tn = _tile(T, 1024), _tile(n_in_p, 640)
    h = _matmul("h_proj", x0b, w_in_p, dims=NN, grid=(T // tm, n_in_p // tn, 1),
                a_spec=pl.BlockSpec((tm, D), lambda i, j, k: (i, 0)),
                b_spec=pl.BlockSpec((D, tn), lambda i, j, k: (0, j)), acc_shape=(tm, tn),
                outs=[(_sds((T, n_in_p), F32), pl.BlockSpec((tm, tn), lambda i, j, k: (i, j)))],
                epilogue=_store())[0]

    u, cqn, ckvn, kr = _mid(h, g_cq, g_ckv, tabs, C)
    g_w_uq, g_w_uk, g_w_uv, g_conv_w = _exchange_wait("ag_heads_wait", ag_heads, gather(4), after=cqn)
    w_uq_p = jnp.pad(g_w_uq, ((0, 0), (0, 0), (0, D_HEAD_PAD - D_QK)))
    w_ukv = jnp.concatenate([g_w_uk, g_w_uv], axis=2)
    conv_w_f = jnp.pad(jnp.transpose(g_conv_w, (1, 0, 2)).reshape(CONV_W, C), ((0, CONV_W_PAD - CONV_W), (0, 0)))
    q = _q_proj(cqn, w_uq_p, tabs)
    kf, vf = _kv_proj(ckvn, w_ukv, kr)
    attn, lse = _flash_fwd(q, kf, vf, D)
    conv_c = _conv_fwd(u, conv_w_f, conv_b)
    cat = _conv_post(conv_c, g_conv_ln, b_conv_ln, attn)
    g_w_out, g_w_ff1, g_w_ff2 = _exchange_wait("ag_ff_wait", ag_ff, gather(3), after=cat)
    w_out_f = g_w_out.reshape(D, D)
    w_ff2_f = g_w_ff2.reshape(F, D)

    def ep_ln1(acc, ex, out, first):
        z1 = ALPHA * ex[0][...] + acc
        xhat, _ = _ln_stats(z1)
        x1 = xhat * ex[1][...] + ex[2][...]
        out[0][...] = z1
        out[1][...] = x1
        out[2][...] = x1.astype(BF16)

    tm = _tile(T, 256)
    rowblk = pl.BlockSpec((tm, D), lambda i, j, k: (i, 0))
    vecD = pl.BlockSpec((1, D), lambda i, j, k: (0, 0))
    z1, x1, x1b = _matmul("mix_ln1", cat, w_out_f, dims=NN, grid=(T // tm, 1, 1), a_spec=rowblk,
                          b_spec=pl.BlockSpec((D, D), lambda i, j, k: (0, 0)), acc_shape=(tm, D),
                          extras=[(x0, rowblk), (g_ln1, vecD), (b_ln1, vecD)],
                          outs=[(_sds((T, D), F32), rowblk), (_sds((T, D), F32), rowblk), (_sds((T, D), BF16), rowblk)],
                          epilogue=ep_ln1, ep_rows=EPILOGUE_ROWS)

    def ep_ff1(acc, ex, out, first):
        r = jnp.maximum(acc, 0.0)
        out[0][...] = (r * r).astype(BF16)
        out[1][...] = r.astype(BF16)

    tm, tn = _tile(T, 1024), _tile(Fs, 1024)
    per = Fs // tn
    fblk = pl.BlockSpec((tm, tn), lambda i, j, k: (i, j))
    f_act, r_act = _matmul("ff1_relu2", x1b, g_w_ff1, dims=NN, grid=(T // tm, F // tn, 1),
                           a_spec=pl.BlockSpec((tm, D), lambda i, j, k: (i, 0)),
                           b_spec=pl.BlockSpec((None, D, tn), lambda i, j, k: (j // per, 0, j % per)),
                           acc_shape=(tm, tn), outs=[(_sds((T, F), BF16), fblk), (_sds((T, F), BF16), fblk)],
                           epilogue=ep_ff1)

    def ep_ln2(acc, ex, out, first):
        g2 = ex[2][...]
        z2 = ALPHA * ex[0][...] + acc
        xhat, rstd = _ln_stats(z2)
        err = xhat * g2 + ex[3][...] - ex[1][...]
        part = 0.5 * jnp.sum(jnp.mean(err * err, axis=-1, keepdims=True))
        _acc_out(out[2], jnp.zeros((8, LANES), F32) + part, first)
        dy = err * (1.0 / D)
        _acc_out(out[3], _colsum(dy * xhat), first)
        _acc_out(out[4], _colsum(dy), first)
        dz2 = _ln_bwd(dy, xhat, rstd, g2)
        out[0][...] = dz2
        out[1][...] = dz2.astype(BF16)

    tm, tk = _tile(T, 512), _tile(F, 1024)
    rowblk = pl.BlockSpec((tm, D), lambda i, j, k: (i, 0))
    dz2, dz2b, loss_blk, dg_ln2, db_ln2 = _matmul(
        "ff2_ln2_loss", f_act, w_ff2_f, dims=NN, grid=(T // tm, 1, F // tk),
        a_spec=pl.BlockSpec((tm, tk), lambda i, j, k: (i, k)), b_spec=pl.BlockSpec((tk, D), lambda i, j, k: (k, 0)),
        acc_shape=(tm, D), extras=[(x1, rowblk), (tgt, rowblk), (g_ln2, vecD), (b_ln2, vecD)],
        outs=[(_sds((T, D), F32), rowblk), (_sds((T, D), BF16), rowblk),
              (_sds((8, LANES), F32), pl.BlockSpec((8, LANES), lambda i, j, k: (0, 0))),
              (_sds((1, D), F32), vecD), (_sds((1, D), F32), vecD)],
        epilogue=ep_ln2, ep_rows=EPILOGUE_ROWS)
    loss = lax.psum(loss_blk[0, 0], ("x", "y", "c"))

    def ep_dpre(acc, ex, out, first):
        out[0][...] = (acc * (2.0 * ex[0][...].astype(F32))).astype(BF16)

    tm, tn = _tile(T, 1024), _tile(F, 1024)
    fblk = pl.BlockSpec((tm, tn), lambda i, j, k: (i, j))
    dpre = _matmul("ff2_dgrad", dz2b, w_ff2_f, dims=NT, grid=(T // tm, F // tn, 1),
                   a_spec=pl.BlockSpec((tm, D), lambda i, j, k: (i, 0)), b_spec=pl.BlockSpec((tn, D), lambda i, j, k: (j, 0)),
                   acc_shape=(tm, tn), extras=[(r_act, fblk)], outs=[(_sds((T, F), BF16), fblk)], epilogue=ep_dpre)[0]

    dw_ff2 = _weight_grad("ff2_wgrad", f_act, dz2b).reshape(N_DEV, Fs, D)
    dw_ff1 = _weight_grad("ff1_wgrad", x1b, dpre, stacked_cols=Fs)
    scatter = lambda k_: [True] * k_
    rs_ff = _exchange_start("rs_ff_start", [dw_ff2, dw_ff1], scatter(2))

    def ep_ln1_bwd(acc, ex, out, first):
        dx1 = ALPHA * ex[0][...] + acc
        xhat, rstd = _ln_stats(ex[1][...])
        _acc_out(out[2], _colsum(dx1 * xhat), first)
        _acc_out(out[3], _colsum(dx1), first)
        dz1 = _ln_bwd(dx1, xhat, rstd, ex[2][...])
        out[0][...] = dz1
        out[1][...] = dz1.astype(BF16)

    tm, tk = _tile(T, 512), _tile(Fs, 1024)
    per = Fs // tk
    rowblk = pl.BlockSpec((tm, D), lambda i, j, k: (i, 0))
    dz1, dz1b, dg_ln1, db_ln1 = _matmul(
        "ff1_dgrad_ln1_bwd", dpre, g_w_ff1, dims=NT, grid=(T // tm, 1, F // tk),
        a_spec=pl.BlockSpec((tm, tk), lambda i, j, k: (i, k)),
        b_spec=pl.BlockSpec((None, D, tk), lambda i, j, k: (k // per, 0, k % per)),
        acc_shape=(tm, D), extras=[(dz2, rowblk), (z1, rowblk), (g_ln1 + rs_ff[4], vecD)],
        outs=[(_sds((T, D), F32), rowblk), (_sds((T, D), BF16), rowblk), (_sds((1, D), F32), vecD), (_sds((1, D), F32), vecD)],
        epilogue=ep_ln1_bwd, ep_rows=EPILOGUE_ROWS)

    dw_out = _weight_grad("out_wgrad", cat, dz1b).reshape(N_DEV, D // N_DEV, D)
    rs_out = _exchange_start("rs_out_start", [dw_out], scatter(1))
    tm, tn = _tile(T, 1024), _tile(D, 1024)
    dcat = _matmul("out_dgrad", dz1b, w_out_f, dims=NT, grid=(T // tm, D // tn, 1),
                   a_spec=pl.BlockSpec((tm, D), lambda i, j, k: (i, 0)), b_spec=pl.BlockSpec((tn, D), lambda i, j, k: (j, 0)),
                   acc_shape=(tm, tn), outs=[(_sds((T, D), F32), pl.BlockSpec((tm, tn), lambda i, j, k: (i, j)))],
                   epilogue=_store())[0]

    dc, dg_conv_ln, db_conv_ln = _conv_post_bwd(dcat, conv_c, g_conv_ln + rs_out[4], b_conv_ln)
    du, dconv_w_p, dconv_b = _conv_bwd(dc, u, conv_w_f)
    dh = _glu_bwd(du, h, C)

    delta, do_heads = _attn_delta(dcat, cat)
    dq, dk, dv = _flash_bwd(q, kf, vf, do_heads, lse, delta)
    dq_raw = _dq_post(dq, tabs)
    cq_blk = (2 * C) // LORA
    dkv, dh = _dk_post(dk, dv, tabs, dh, (2 * C + 2 * LORA) // LANES)
    by_rank = lambda w_: jnp.transpose(w_, (1, 0, 2)).reshape(LORA, N_HEADS * D_HEAD_PAD)
    dh, dg_cq = _latent_bwd("q_dgrad_rms_bwd", dq_raw, by_rank(w_uq_p), h, cq_blk, g_cq, dh)
    dh, dg_ckv = _latent_bwd("kv_dgrad_rms_bwd", dkv, by_rank(w_ukv), h, cq_blk + 1, g_ckv, dh)

    dw_in_p = _weight_grad("in_wgrad", x0b, dh, tn_pref=640)
    dw_in_f = jnp.concatenate([dw_in_p[:, 2 * C:2 * C + 2 * LORA + D_ROPE], dw_in_p[:, :2 * C]], axis=1)
    dw_in = jnp.transpose(dw_in_f.reshape(D, N_DEV, n_in // N_DEV), (1, 0, 2)).reshape(N_DEV, -1, LANES)
    rs_in =_exchange_start("rs_in_start", [dw_in], scatter(1))

    def ep_ln_in_bwd(acc, ex, out, first):
        dx0 = ALPHA * ex[0][...] + acc
        xhat, rstd = _ln_stats(ex[1][...])
        _acc_out(out[1], _colsum(dx0 * xhat), first)
        _acc_out(out[2], _colsum(dx0), first)
        out[0][...] = _ln_bwd(dx0, xhat, rstd, ex[2][...])

    tm, tk = _tile(T, 512), _tile(n_in_p, 640)
    rowblk = pl.BlockSpec((tm, D), lambda i, j, k: (i, 0))
    grad_x, dg_ln_in, db_ln_in = _matmul(
        "in_dgrad_ln_in_bwd", dh, w_in_p, dims=NT, grid=(T // tm, 1, n_in_p // tk),
        a_spec=pl.BlockSpec((tm, tk), lambda i, j, k: (i, k)), b_spec=pl.BlockSpec((D, tk), lambda i, j, k: (0, k)),
        acc_shape=(tm, D), extras=[(dz1, rowblk), (xs, rowblk), (row(ln_in_g) + rs_in[4], vecD)],
        outs=[(_sds((T, D), F32), rowblk), (_sds((1, D), F32), vecD), (_sds((1, D), F32), vecD)],
        epilogue=ep_ln_in_bwd, ep_rows=EPILOGUE_ROWS)

    dw_uq = _head_weight_grad("uq_wgrad", cqn, dq_raw)[:, :, :D_QK]
    dw_ukv = _head_weight_grad("ukv_wgrad", ckvn, dkv)
    dw_uk, dw_uv = dw_ukv[:, :, :D_NOPE], dw_ukv[:, :, D_NOPE:]
    dconv_w = jnp.transpose(dconv_w_p[:CONV_W].reshape(CONV_W, N_DEV, C // N_DEV), (1, 0, 2))
    rs_heads = _exchange_start("rs_heads_start", [dw_uq, dw_uk, dw_uv, dconv_w], scatter(4))

    small = dict(ln_in_g=dg_ln_in, ln_in_b=db_ln_in, g_cq=dg_cq, g_ckv=dg_ckv, conv_b=dconv_b, g_conv_ln=dg_conv_ln,
                 b_conv_ln=db_conv_ln, g_ln1=dg_ln1, b_ln1=db_ln1, g_ln2=dg_ln2, b_ln2=db_ln2)
    names = _small_names()
    rs_small = _exchange_start("rs_small_start", [_pack([small[n] for n in names])], gather(1))
    res = {}

    def update(group, parts, after):
        last = after
        for n, p in zip(group, parts):
            shard = lambda a: a[0].reshape(p.shape[1:])
            outs_n = _adamw("adamw_" + n, p, shard(args[n]), shard(args["m_" + n]), shard(args["v_" + n]))
            res[n] = [o.reshape(args[n].shape) for o in outs_n]
            last = outs_n[0]
        return last

    done = update(["w_ff2", "w_ff1"], _exchange_wait("rs_ff_wait", rs_ff, scatter(2), after=grad_x), grad_x)
    done = update(["w_out"], _exchange_wait("rs_out_wait", rs_out, scatter(1), after=done), done)
    done = update(["w_in"], _exchange_wait("rs_in_wait", rs_in, scatter(1), after=done), done)
    done = update(["w_uq", "w_uk", "w_uv", "conv_w"],
                  _exchange_wait("rs_heads_wait", rs_heads, scatter(4), after=done), done)
    (small_parts,) = _exchange_wait("rs_small_wait", rs_small, gather(1), after=done)
    packed = _adamw("adamw_small", small_parts, _pack([args[n] for n in names]), _pack([args["m_" + n] for n in names]),
                    _pack([args["v_" + n] for n in names]))
    like = [args[n] for n in names]
    unpacked = [_unpack(p, like) for p in packed]
    for i, n in enumerate(names):
        res[n] = [unpacked[kind][i] for kind in range(4)]

    order = ["ln_in_g", "ln_in_b", "w_in", "g_cq", "w_uq", "g_ckv", "w_uk", "w_uv", "conv_w", "conv_b", "g_conv_ln",
             "b_conv_ln", "w_out", "g_ln1", "b_ln1", "w_ff1", "w_ff2", "g_ln2", "b_ln2"]
    outs = [loss, grad_x.reshape(x.shape)]
    for kind in range(4):
        outs += [res[n][kind] for n in order]
    return tuple(outs)
```

```python
import jax
import jax.numpy as jnp
from jax import lax
from jax.experimental import pallas as pl
from jax.experimental.pallas import tpu as pltpu

F32 = jnp.float32
BF16 = jnp.bfloat16

N_HEADS = 8
D_NOPE = 128
D_ROPE = 64
D_V = 128
D_QK = D_NOPE + D_ROPE
D_HEAD_PAD = 256
LORA = 512
CONV_W = 31
CONV_HALF = CONV_W // 2
CONV_W_PAD = 32
HALO = 16
LN_EPS = 1e-5
RMS_EPS = 1e-6
ALPHA = 2.0 ** 0.25
SCALE = float(D_QK) ** -0.5
LOG2_E = 1.4426950408889634
LN_2 = 0.6931471805599453
Q_SCALE = SCALE * LOG2_E
ROPE_BASE = 10000.0
ADAM_LR, ADAM_B1, ADAM_B2, ADAM_EPS, ADAM_WD, ADAM_STEP = 0.001, 0.9, 0.999, 1e-08, 0.01, 10

N_DEV = 8
LANES = 128
VMEM_LIMIT_V7X = 56 * 1024 * 1024

NN = (((1,), (0,)), ((), ()))
NT = (((1,), (1,)), ((), ()))
TN = (((0,), (0,)), ((), ()))


def _call(body, *, name, grid, in_specs, out_specs, out_shape, scratch=(), aliases=None):
    params = pltpu.CompilerParams(dimension_semantics=("arbitrary",) * len(grid),
                                  vmem_limit_bytes=VMEM_LIMIT_V7X)
    return pl.pallas_call(body, name=name, grid=grid, in_specs=in_specs, out_specs=out_specs,
                          out_shape=out_shape, scratch_shapes=scratch, compiler_params=params,
                          input_output_aliases=aliases or {})


def _tile(n, pref):
    if n <= pref:
        return n
    t = (pref // LANES) * LANES
    while t > LANES and n % t:
        t -= LANES
    assert n % t == 0, (n, pref)
    return t


def _sds(shape, dtype):
    return jax.ShapeDtypeStruct(shape, dtype)


def _ln_stats(z):
    mu = jnp.mean(z, axis=-1, keepdims=True)
    zc = z - mu
    var = jnp.mean(zc * zc, axis=-1, keepdims=True)
    rstd = lax.rsqrt(var + LN_EPS)
    return zc * rstd, rstd


def _ln_bwd(dy, xhat, rstd, g):
    gd = dy * g
    m1 = jnp.mean(gd, axis=-1, keepdims=True)
    m2 = jnp.mean(gd * xhat, axis=-1, keepdims=True)
    return rstd * (gd - m1 - xhat * m2)


def _rms(x, g):
    return x * lax.rsqrt(jnp.mean(x * x, axis=-1, keepdims=True) + RMS_EPS) * g


def _rms_bwd(dy, x, g):
    r = lax.rsqrt(jnp.mean(x * x, axis=-1, keepdims=True) + RMS_EPS)
    dxn = dy * g
    dx = r * dxn - x * (r * r * r) * jnp.mean(dxn * x, axis=-1, keepdims=True)
    dg = jnp.sum(dy * x * r, axis=0, keepdims=True)
    return dx, dg


def _sigmoid(x):
    return 1.0 / (1.0 + jnp.exp(-x))


def _rope(x, cos_p, sin_a, sin_b):
    return x * cos_p + pltpu.roll(x, 96, 1) * sin_a + pltpu.roll(x, 32, 1) * sin_b


def _unrope(d, cos_p, sin_a, sin_b):
    return d * cos_p - pltpu.roll(d, 96, 1) * sin_a - pltpu.roll(d, 32, 1) * sin_b


def _colsum(v):
    return jnp.sum(v, axis=0, keepdims=True)


def _acc_out(ref, val, first):
    if first is False:
        ref[...] += val
        return

    @pl.when(first)
    def _():
        ref[...] = val

    @pl.when(jnp.logical_not(first))
    def _():
        ref[...] += val


class _Rows:
    def __init__(self, ref, sl):
        self.ref, self.sl = ref, sl

    def __getitem__(self, idx):
        assert idx is Ellipsis
        return self.ref[self.sl, :]

    def __setitem__(self, idx, val):
        assert idx is Ellipsis
        self.ref[self.sl, :] = val


def _matmul(name, a, b, *, dims, grid, a_spec, b_spec, acc_shape, outs, epilogue, extras=(), ep_rows=None, into=None):
    nk = grid[2]
    ne, no = len(extras), len(outs)
    tm = acc_shape[0]
    n_in = 2 + ne + (0 if into is None else 1)

    def finish(acc_rows, ex, out):
        first = pl.program_id(0) == 0
        if ep_rows is None or ep_rows >= tm:
            epilogue(acc_rows(slice(None)), ex, out, first)
            return
        for r0 in range(0, tm, ep_rows):
            sl = slice(r0, r0 + ep_rows)
            view = lambda r: _Rows(r, sl) if r.shape[0] == tm else r
            epilogue(acc_rows(sl), [view(r) for r in ex], [view(r) for r in out], first if r0 == 0 else False)

    def body(*refs):
        a_ref, b_ref = refs[0], refs[1]
        ex = refs[2:2 + ne]
        out = refs[n_in:n_in + no]
        if nk == 1:
            part = lax.dot_general(a_ref[...], b_ref[...], dims, preferred_element_type=F32)
            finish(lambda sl: part[sl, :], ex, out)
        else:
            acc = refs[n_in + no]
            k = pl.program_id(2)

            @pl.when(k == 0)
            def _():
                acc[...] = jnp.zeros_like(acc)

            acc[...] += lax.dot_general(a_ref[...], b_ref[...], dims, preferred_element_type=F32)

            @pl.when(k == nk - 1)
            def _():
                finish(lambda sl: acc[sl, :], ex, out)

    scratch = [] if nk == 1 else [pltpu.VMEM(acc_shape, F32)]
    ins = [a, b] + [e for e, _ in extras]
    in_specs = [a_spec, b_spec] + [s for _, s in extras]
    aliases = {}
    if into is not None:
        ins.append(into)
        in_specs.append(pl.BlockSpec(memory_space=pl.ANY))
        aliases = {n_in - 1: 0}
    return _call(body, name=name, grid=grid, in_specs=in_specs, out_specs=[s for _, s in outs],
                 out_shape=[o for o, _ in outs], scratch=scratch, aliases=aliases)(*ins)


def _store(dtype=F32):
    def ep(acc, ex, out, first):
        out[0][...] = acc.astype(dtype)
    return ep


def _mesh_pos():
    return lax.axis_index("x"), lax.axis_index("y"), lax.axis_index("c")


def _flip(v, bit):
    return 1 - v if bit else v


_HBM = pl.BlockSpec(memory_space=pltpu.HBM)
_SEM = pl.BlockSpec(memory_space=pltpu.SEMAPHORE)
_EFFECT = pltpu.SideEffectType.DATAFLOW_SIDE_EFFECTING


def _my_slot():
    x, y, c = _mesh_pos()
    return 4 * x + 2 * y + c


def _exchange_copies(srcs, lands, send_sems, recv_sems, stacked, receives=True):
    x, y, c = _mesh_pos()
    me = 4 * x + 2 * y + c
    pairs = []
    for w in range(len(srcs)):
        for k in range(1, N_DEV):
            peer = (_flip(x, k & 4), _flip(y, k & 2), _flip(c, k & 1))
            peer_slot = 4 * peer[0] + 2 * peer[1] + peer[2]
            to_peer = srcs[w].at[peer_slot] if stacked[w] else srcs[w]
            mine = srcs[w].at[me] if stacked[w] else srcs[w]
            s = w * (N_DEV - 1) + k - 1
            sems = dict(send_sem=send_sems.at[s], recv_sem=recv_sems.at[s],
                        device_id=peer, device_id_type=pl.DeviceIdType.MESH)
            send = pltpu.make_async_remote_copy(src_ref=to_peer, dst_ref=lands[w].at[me], **sems)
            recv = pltpu.make_async_remote_copy(src_ref=mine, dst_ref=lands[w].at[peer_slot], **sems) if receives else None
            pairs.append((send, recv))
    return pairs


def _exchange_start(name, srcs, stacked):
    n = len(srcs)
    land_shapes = [s.shape if st else (N_DEV,) + s.shape for s, st in zip(srcs, stacked)]

    def body(*refs):
        src, land = refs[:n], refs[n:2 * n]
        send_sems, recv_sems = refs[2 * n], refs[2 * n + 1]
        token = refs[-1]
        for send, _ in _exchange_copies(src, land, send_sems, recv_sems, stacked, receives=False):
            send.start()
        token[...] = jnp.zeros_like(token)

    hbm = lambda a: pltpu.with_memory_space_constraint(a, pltpu.HBM)
    outs = pl.pallas_call(
        body, name=name,
        out_shape=(pltpu.SemaphoreType.DMA((n * (N_DEV - 1),)), pltpu.SemaphoreType.DMA((n * (N_DEV - 1),)),
                   *[pltpu.HBM(s.shape, s.dtype) for s in srcs],
                   *[pltpu.HBM(ls, s.dtype) for ls, s in zip(land_shapes, srcs)],
                   _sds((8, LANES), F32)),
        in_specs=[_HBM] * (2 * n),
        out_specs=(_SEM, _SEM, *[_HBM] * (2 * n), pl.BlockSpec(memory_space=pltpu.VMEM)),
        input_output_aliases={i: 2 + i for i in range(2 * n)},
        compiler_params=pltpu.CompilerParams(has_side_effects=_EFFECT),
    )(*[hbm(s) for s in srcs], *[hbm(lax.empty(ls, s.dtype)) for ls, s in zip(land_shapes, srcs)])
    return outs[0], outs[1], list(outs[2:2 + n]), list(outs[2 + n:2 + 2 * n]), outs[-1][0, 0]


def _exchange_wait(name, started, stacked, after):
    srcs, lands = _wait_call(name, started, stacked, after)
    me = _my_slot()
    full = []
    for src, land, st in zip(srcs, lands, stacked):
        own = lax.dynamic_index_in_dim(src, me, 0, keepdims=True) if st else src[None]
        full.append(lax.dynamic_update_index_in_dim(land, own, me, 0))
    return full


def _wait_call(name, started, stacked, after):
    send_sems, recv_sems, srcs, lands, _ = started
    n = len(srcs)

    def body(*refs):
        src, land = refs[:n], refs[n:2 * n]
        s_sems, r_sems = refs[2 * n], refs[2 * n + 1]
        for send, recv in _exchange_copies(src, land, s_sems, r_sems, stacked):
            send.wait_send()
            recv.wait_recv()

    outs = pl.pallas_call(
        body, name=name,
        out_shape=tuple(pltpu.HBM(a.shape, a.dtype) for a in srcs + lands),
        in_specs=[_HBM] * (2 * n) + [_SEM, _SEM, pl.BlockSpec(memory_space=pl.ANY)],
        out_specs=[_HBM] * (2 * n),
        input_output_aliases={i: i for i in range(2 * n)},
        compiler_params=pltpu.CompilerParams(has_side_effects=_EFFECT),
    )(*srcs, *lands, send_sems, recv_sems, after)
    return outs[:n], outs[n:]


def _adamw(name, parts, w, m, v):
    rows, cols = w.shape
    cap = max(8, (LANES * 1024) // cols)
    tr = rows
    if rows > cap:
        tr = (cap // 8) * 8
        while rows % tr:
            tr -= 8
    c1 = 1.0 / (1.0 - ADAM_B1 ** ADAM_STEP)
    c2 = 1.0 / (1.0 - ADAM_B2 ** ADAM_STEP)

    def body(p_ref, w_ref, m_ref, v_ref, g_o, d_o, m_o, v_o):
        g = p_ref[0]
        for s in range(1, N_DEV):
            g = g + p_ref[s]
        mn = ADAM_B1 * m_ref[...] + (1.0 - ADAM_B1) * g
        vn = ADAM_B2 * v_ref[...] + (1.0 - ADAM_B2) * (g * g)
        g_o[...] = g
        m_o[...] = mn
        v_o[...] = vn
        d_o[...] = -ADAM_LR * ((mn * c1) / (jnp.sqrt(vn * c2) + ADAM_EPS) + ADAM_WD * w_ref[...])

    blk = pl.BlockSpec((tr, cols), lambda i: (i, 0))
    return _call(body, name=name, grid=(rows // tr,),
                 in_specs=[pl.BlockSpec((N_DEV, tr, cols), lambda i: (0, i, 0)), blk, blk, blk],
                 out_specs=[blk] * 4, out_shape=[_sds((rows, cols), F32)] * 4)(parts, w, m, v)


def _rope_tables(pos_col, inv_freq):
    T = pos_col.shape[0]
    tm = _tile(T, 1024)

    def body(p_ref, f_ref, c_o, sa_o, sb_o):
        ang = p_ref[...].astype(F32) * f_ref[...]
        lane = lax.broadcasted_iota(jnp.int32, ang.shape, 1)
        cs, sn = jnp.cos(ang), jnp.sin(ang)
        c_o[...] = jnp.where(lane < D_ROPE, cs, 0.0)
        sa_o[...] = jnp.where(lane < D_ROPE // 2, -sn, 0.0)
        sb_o[...] = jnp.where((lane >= D_ROPE // 2) & (lane < D_ROPE), sn, 0.0)

    blk = pl.BlockSpec((tm, LANES), lambda i: (i, 0))
    return _call(body, name="rope_tables", grid=(T // tm,),
                 in_specs=[pl.BlockSpec((tm, 1), lambda i: (i, 0)), pl.BlockSpec((1, LANES), lambda i: (0, 0))],
                 out_specs=[blk] * 3, out_shape=[_sds((T, LANES), F32)] * 3)(pos_col, inv_freq)


def _ln_in(x, g, b):
    T, D = x.shape
    tm = _tile(T, 512)

    def body(x_ref, g_ref, b_ref, o32, o16):
        xhat, _ = _ln_stats(x_ref[...])
        y = xhat * g_ref[...] + b_ref[...]
        o32[...] = y
        o16[...] = y.astype(BF16)

    blk = pl.BlockSpec((tm, D), lambda i: (i, 0))
    vec = pl.BlockSpec((1, D), lambda i: (0, 0))
    return _call(body, name="ln_in", grid=(T // tm,), in_specs=[blk, vec, vec], out_specs=[blk, blk],
                 out_shape=[_sds((T, D), F32), _sds((T, D), BF16)])(x, g, b)


def _mid(h, g_cq, g_ckv, tabs, C):
    T = h.shape[0]
    tm = _tile(T, 256)
    cq_blk, kr_blk = (2 * C) // LORA, (2 * C + 2 * LORA) // LANES

    def body(a_ref, gt_ref, cq_ref, ckv_ref, kr_ref, gq_ref, gkv_ref, cp, sa, sb, u_o, cqn_o, ckvn_o, kr_o):
        u_o[...] = a_ref[...] * _sigmoid(gt_ref[...])
        cqn_o[...] = _rms(cq_ref[...], gq_ref[...]).astype(BF16)
        ckvn_o[...] = _rms(ckv_ref[...], gkv_ref[...]).astype(BF16)
        kr_o[...] = _rope(kr_ref[...], cp[...], sa[...], sb[...]).astype(BF16)

    def col(w, j):
        return pl.BlockSpec((tm, w), lambda i: (i, j))

    vec = pl.BlockSpec((1, LORA), lambda i: (0, 0))
    return _call(body, name="mid_norm_glu", grid=(T // tm,),
                 in_specs=[col(C, 0), col(C, 1), col(LORA, cq_blk), col(LORA, cq_blk + 1), col(LANES, kr_blk),
                           vec, vec, col(LANES, 0), col(LANES, 0), col(LANES, 0)],
                 out_specs=[col(C, 0), col(LORA, 0), col(LORA, 0), col(LANES, 0)],
                 out_shape=[_sds((T, C), F32), _sds((T, LORA), BF16), _sds((T, LORA), BF16), _sds((T, LANES), BF16)],
                 )(h, h, h, h, h, g_cq, g_ckv, *tabs)


def _q_proj(cqn, w_uq_p, tabs):
    T = cqn.shape[0]
    tm = _tile(T, HEAD_ROWS)

    def body(c_ref, w_ref, cp, sa, sb, o_ref):
        q = jnp.dot(c_ref[...], w_ref[...], preferred_element_type=F32)
        o_ref[:, :D_NOPE] = (q[:, :D_NOPE] * Q_SCALE).astype(BF16)
        o_ref[:, D_NOPE:] = (_rope(q[:, D_NOPE:], cp[...], sa[...], sb[...]) * Q_SCALE).astype(BF16)

    tab = pl.BlockSpec((tm, LANES), lambda i, h: (i, 0))
    return _call(body, name="q_proj_rope", grid=(T // tm, N_HEADS),
                 in_specs=[pl.BlockSpec((tm, LORA), lambda i, h: (i, 0)),
                           pl.BlockSpec((None, LORA, D_HEAD_PAD), lambda i, h: (h, 0, 0)), tab, tab, tab],
                 out_specs=pl.BlockSpec((None, tm, D_HEAD_PAD), lambda i, h: (h, i, 0)),
                 out_shape=_sds((N_HEADS, T, D_HEAD_PAD), BF16))(cqn, w_uq_p, *tabs)


def _kv_proj(ckvn, w_ukv, kr):
    T = ckvn.shape[0]
    tm = _tile(T, HEAD_ROWS)

    def body(c_ref, w_ref, kr_ref, k_o, v_o):
        kv = jnp.dot(c_ref[...], w_ref[...], preferred_element_type=F32)
        k_o[:, :D_NOPE] = kv[:, :D_NOPE].astype(BF16)
        k_o[:, D_NOPE:] = kr_ref[...]
        v_o[:, :D_V] = kv[:, D_NOPE:].astype(BF16)
        v_o[:, D_V:] = jnp.ones((tm, D_V), BF16)

    return _call(body, name="kv_proj", grid=(T // tm, N_HEADS),
                 in_specs=[pl.BlockSpec((tm, LORA), lambda i, h: (i, 0)),
                           pl.BlockSpec((None, LORA, D_NOPE + D_V), lambda i, h: (h, 0, 0)),
                           pl.BlockSpec((tm, LANES), lambda i, h: (i, 0))],
                 out_specs=[pl.BlockSpec((None, tm, D_HEAD_PAD), lambda i, h: (h, i, 0)),
                            pl.BlockSpec((None, tm, 2 * D_V), lambda i, h: (h, i, 0))],
                 out_shape=[_sds((N_HEADS, T, D_HEAD_PAD), BF16), _sds((N_HEADS, T, 2 * D_V), BF16)])(ckvn, w_ukv, kr)


def _flash_fwd(q, k, v1, out_cols):
    _, T, _ = q.shape
    tq, tk = _tile(T, FLASH_TQ), _tile(T, FLASH_TK)
    nkv, reps = T // tk, tk // LANES

    def body(q_ref, k_ref, v_ref, o_ref, lse_ref, m_sc, acc_sc):
        m_sc[...] = jnp.full_like(m_sc, -jnp.inf)
        acc_sc[...] = jnp.zeros_like(acc_sc)
        qv = q_ref[...]

        def rows(j):
            return pl.ds(pl.multiple_of(j * tk, tk), tk)

        def scores(j):
            return lax.dot_general(qv, k_ref[rows(j), :], NT, preferred_element_type=F32)

        def update(s, j):
            m_prev = m_sc[...]
            m_new = jnp.maximum(m_prev, jnp.max(s, axis=1, keepdims=True))
            a = jnp.exp2(m_prev - m_new)
            p = jnp.exp2(s - jnp.tile(m_new, (1, reps)))
            pv = jnp.dot(p.astype(BF16), v_ref[rows(j), :], preferred_element_type=F32)
            acc_sc[...] = jnp.tile(a, (1, 2)) * acc_sc[...] + pv
            m_sc[...] = m_new

        def step(j, carry):
            update(scores(j), j)
            return carry

        lax.fori_loop(0, nkv, step, 0, unroll=FLASH_UNROLL_FWD if nkv % FLASH_UNROLL_FWD == 0 else 1)
        acc = acc_sc[...]
        l = acc[:, D_V:]
        o_ref[...] = (acc[:, :D_V] / l).astype(BF16)
        lse_ref[...] = m_sc[...] + jnp.log(l) * LOG2_E

    return _call(body, name="flash_fwd", grid=(N_HEADS, T // tq),
                 in_specs=[pl.BlockSpec((None, tq, D_HEAD_PAD), lambda h, i: (h, i, 0)),
                           pl.BlockSpec((None, T, D_HEAD_PAD), lambda h, i: (h, 0, 0)),
                           pl.BlockSpec((None, T, 2 * D_V), lambda h, i: (h, 0, 0))],
                 out_specs=[pl.BlockSpec((tq, D_V), lambda h, i: (i, h)),
                            pl.BlockSpec((None, tq, LANES), lambda h, i: (h, i, 0))],
                 out_shape=[_sds((T, out_cols), BF16), _sds((N_HEADS, T, LANES), F32)],
                 scratch=[pltpu.VMEM((tq, LANES), F32), pltpu.VMEM((tq, 2 * D_V), F32)])(q, k, v1)


def _halo_specs(tm, cb, n_t):
    r = tm // HALO
    return [pl.BlockSpec((HALO, cb), lambda jc, i: (jnp.maximum(i * r - 1, 0), jc)),
            pl.BlockSpec((tm, cb), lambda jc, i: (i, jc)),
            pl.BlockSpec((HALO, cb), lambda jc, i: (jnp.minimum((i + 1) * r, n_t * r - 1), jc))]


def _fill_ext(ext, prev_ref, cur_ref, next_ref, i, n_t, tm):
    ext[0:HALO, :] = jnp.where(i > 0, prev_ref[...], 0.0)
    ext[HALO:HALO + tm, :] = cur_ref[...]
    ext[HALO + tm:, :] = jnp.where(i < n_t - 1, next_ref[...], 0.0)


HEAD_ROWS = 2048
FLASH_TQ = 512
FLASH_TK = 512
FLASH_UNROLL_FWD = 16
FLASH_UNROLL_BWD = 8
EPILOGUE_ROWS = 128
CONV_ROWS = 64


def _fill_shifted(shifted, ext):
    rows = shifted.shape[1]
    for s in range(1, 8):
        shifted[s, :, :] = ext[s:s + rows, :]


def _window(ext, shifted, start, rows):
    s, base = start % 8, start - start % 8
    return ext[base:base + rows, :] if s == 0 else shifted[s, base:base + rows, :]


def _conv_fwd(u, w_pad, bias):
    T, C = u.shape
    tm, cb = _tile(T, 256), _tile(C, 256)
    n_t = T // tm
    rb = min(CONV_ROWS, tm)

    def body(up, uc, un, w_ref, b_ref, c_o, ext, shifted):
        i = pl.program_id(1)
        _fill_ext(ext, up, uc, un, i, n_t, tm)
        _fill_shifted(shifted, ext)
        for r0 in range(0, tm, rb):
            acc = jnp.zeros((rb, cb), F32) + b_ref[...]
            for k in range(CONV_W):
                acc = acc + w_ref[k:k + 1, :] * _window(ext, shifted, r0 + k + 1, rb)
            c_o[r0:r0 + rb, :] = acc

    return _call(body, name="conv_fwd", grid=(C // cb, n_t),
                 in_specs=_halo_specs(tm, cb, n_t) + [pl.BlockSpec((CONV_W_PAD, cb), lambda jc, i: (0, jc)),
                                                      pl.BlockSpec((1, cb), lambda jc, i: (0, jc))],
                 out_specs=pl.BlockSpec((tm, cb), lambda jc, i: (i, jc)),
                 out_shape=_sds((T, C), F32),
                 scratch=[pltpu.VMEM((tm + 2 * HALO, cb), F32),
                          pltpu.VMEM((8, tm + 2 * HALO - 8, cb), F32)])(u, u, u, w_pad, bias)


def _conv_post(c, g, b, cat):
    T, C = c.shape
    assert cat.shape == (T, 2 * C)
    tm = _tile(T, 512)

    def body(c_ref, g_ref, b_ref, cat_ref, o_ref):
        xhat, _ = _ln_stats(c_ref[...])
        y = xhat * g_ref[...] + b_ref[...]
        o_ref[...] = (y * _sigmoid(y)).astype(BF16)

    blk = pl.BlockSpec((tm, C), lambda i: (i, 0))
    vec = pl.BlockSpec((1, C), lambda i: (0, 0))
    return _call(body, name="conv_ln_silu", grid=(T // tm,),
                 in_specs=[blk, vec, vec, pl.BlockSpec(memory_space=pl.ANY)],
                 out_specs=pl.BlockSpec((tm, C), lambda i: (i, 1)),
                 out_shape=_sds(cat.shape, BF16), aliases={3: 0})(c, g, b, cat)


def _conv_post_bwd(dcat, c, g, b):
    T, C = c.shape
    tm = _tile(T, 512)

    def body(d_ref, c_ref, g_ref, b_ref, dc_o, dg_o, db_o):
        first = pl.program_id(0) == 0
        xhat, rstd = _ln_stats(c_ref[...])
        y = xhat * g_ref[...] + b_ref[...]
        sg = _sigmoid(y)
        dy = d_ref[...] * (sg * (1.0 + y * (1.0 - sg)))
        _acc_out(dg_o, _colsum(dy * xhat), first)
        _acc_out(db_o, _colsum(dy), first)
        dc_o[...] = _ln_bwd(dy, xhat, rstd, g_ref[...])

    blk = pl.BlockSpec((tm, C), lambda i: (i, 0))
    vec = pl.BlockSpec((1, C), lambda i: (0, 0))
    return _call(body, name="conv_ln_silu_bwd", grid=(T // tm,),
                 in_specs=[pl.BlockSpec((tm, C), lambda i: (i, 1)), blk, vec, vec],
                 out_specs=[blk, vec, vec],
                 out_shape=[_sds((T, C), F32), _sds((1, C), F32), _sds((1, C), F32)])(dcat, c, g, b)


def _conv_bwd(dc, u, w_pad):
    T, C = u.shape
    tm, cb = _tile(T, 256), _tile(C, 256)
    n_t = T // tm
    rb = min(CONV_ROWS, tm)

    def body(dp, dcur, dn, uc, w_ref, du_o, dw_o, db_o, dext, shifted, dw_sc):
        i = pl.program_id(1)
        _fill_ext(dext, dp, dcur, dn, i, n_t, tm)
        _fill_shifted(shifted, dext)

        @pl.when(i == 0)
        def _():
            dw_sc[...] = jnp.zeros_like(dw_sc)

        for r0 in range(0, tm, rb):
            acc = jnp.zeros((rb, cb), F32)
            u_here = uc[r0:r0 + rb, :]
            for k in range(CONV_W):
                win = _window(dext, shifted, r0 + 2 * HALO - 1 - k, rb)
                acc = acc + w_ref[k:k + 1, :] * win
                dw_sc[k] += jnp.sum((win * u_here).reshape(rb // 8, 8, cb), axis=0)
            du_o[r0:r0 + rb, :] = acc
        dw_sc[CONV_W] += jnp.sum(dcur[...].reshape(tm // 8, 8, cb), axis=0)

        @pl.when(i == n_t - 1)
        def _():
            red = jnp.sum(dw_sc[...], axis=1)
            row = lax.broadcasted_iota(jnp.int32, red.shape, 0)
            dw_o[...] = jnp.where(row < CONV_W, red, 0.0)
            db_o[...] = jnp.sum(jnp.where(row == CONV_W, red, 0.0), axis=0, keepdims=True)

    return _call(body, name="conv_bwd", grid=(C // cb, n_t),
                 in_specs=_halo_specs(tm, cb, n_t) + [pl.BlockSpec((tm, cb), lambda jc, i: (i, jc)),
                                                      pl.BlockSpec((CONV_W_PAD, cb), lambda jc, i: (0, jc))],
                 out_specs=[pl.BlockSpec((tm, cb), lambda jc, i: (i, jc)),
                            pl.BlockSpec((CONV_W_PAD, cb), lambda jc, i: (0, jc)),
                            pl.BlockSpec((1, cb), lambda jc, i: (0, jc))],
                 out_shape=[_sds((T, C), F32), _sds((CONV_W_PAD, C), F32), _sds((1, C), F32)],
                 scratch=[pltpu.VMEM((tm + 2 * HALO, cb), F32), pltpu.VMEM((8, tm + 2 * HALO - 8, cb), F32),
                          pltpu.VMEM((CONV_W_PAD, 8, cb), F32)])(dc, dc, dc, u, w_pad)


def _glu_bwd(du, h, C):
    T = du.shape[0]
    tm = _tile(T, 512)

    def body(du_ref, a_ref, gt_ref, o_ref):
        sg = _sigmoid(gt_ref[...])
        du_v = du_ref[...]
        o_ref[:, :C] = (du_v * sg).astype(BF16)
        o_ref[:, C:] = (du_v * a_ref[...] * sg * (1.0 - sg)).astype(BF16)

    return _call(body, name="glu_bwd", grid=(T // tm,),
                 in_specs=[pl.BlockSpec((tm, C), lambda i: (i, 0)), pl.BlockSpec((tm, C), lambda i: (i, 0)),
                           pl.BlockSpec((tm, C), lambda i: (i, 1))],
                 out_specs=pl.BlockSpec((tm, 2 * C), lambda i: (i, 0)),
                 out_shape=_sds(h.shape, BF16))(du, h, h)


def _attn_delta(dcat, attn):
    T = attn.shape[0]
    tm = _tile(T, HEAD_ROWS)

    def body(d_ref, o_ref, dl_o, dob_o):
        d = d_ref[...]
        dl = jnp.sum(d * o_ref[...].astype(F32), axis=1, keepdims=True)
        dl_o[...] = jnp.broadcast_to(dl, (tm, LANES))
        dob_o[...] = d.astype(BF16)

    blk = pl.BlockSpec((tm, D_V), lambda i, h: (i, h))
    hblk = pl.BlockSpec((None, tm, D_V), lambda i, h: (h, i, 0))
    return _call(body, name="attn_delta", grid=(T // tm, N_HEADS), in_specs=[blk, blk], out_specs=[hblk, hblk],
                 out_shape=[_sds((N_HEADS, T, LANES), F32), _sds((N_HEADS, T, D_V), BF16)])(dcat, attn)


def _flash_bwd(q, k, v1, do, lse, delta):
    _, T, _ = q.shape
    tq, tk = _tile(T, FLASH_TQ), _tile(T, FLASH_TK)
    nkv, reps = T // tk, tk // LANES

    def body(q_ref, do_ref, lse_ref, dl_ref, k_ref, v_ref, dq_o, dk_o, dv_o, dq_sc):
        @pl.when(pl.program_id(1) == 0)
        def _():
            dk_o[...] = jnp.zeros_like(dk_o)
            dv_o[...] = jnp.zeros_like(dv_o)

        qv, dov = q_ref[...], do_ref[...]
        lse_t = jnp.tile(lse_ref[...], (1, reps))
        dl_t = jnp.tile(dl_ref[...], (1, reps))
        dq_sc[...] = jnp.zeros_like(dq_sc)

        def rows(j):
            return pl.ds(pl.multiple_of(j * tk, tk), tk)

        def scores(j):
            s = lax.dot_general(qv, k_ref[rows(j), :], NT, preferred_element_type=F32)
            dp = lax.dot_general(dov, v_ref[rows(j), :D_V], NT, preferred_element_type=F32)
            return s, dp

        def update(s_dp, j):
            s, dp = s_dp
            p = jnp.exp2(s - lse_t)
            ds = (p * (dp - dl_t)).astype(BF16)
            dv_o[rows(j), :] += lax.dot_general(p.astype(BF16), dov, TN, preferred_element_type=F32)
            dk_o[rows(j), :] += lax.dot_general(ds, qv, TN, preferred_element_type=F32)
            dq_sc[...] += jnp.dot(ds, k_ref[rows(j), :], preferred_element_type=F32)

        def step(j, carry):
            update(scores(j), j)
            return carry

        lax.fori_loop(0, nkv, step, 0, unroll=FLASH_UNROLL_BWD if nkv % FLASH_UNROLL_BWD == 0 else 1)
        dq_o[...] = dq_sc[...]

    def tile(w):
        return pl.BlockSpec((None, tq, w), lambda h, i: (h, i, 0))

    def whole(w):
        return pl.BlockSpec((None, T, w), lambda h, i: (h, 0, 0))

    return _call(body, name="flash_bwd", grid=(N_HEADS, T // tq),
                 in_specs=[tile(D_HEAD_PAD), tile(D_V), tile(LANES), tile(LANES), whole(D_HEAD_PAD), whole(2 * D_V)],
                 out_specs=[tile(D_HEAD_PAD), whole(D_HEAD_PAD), whole(D_V)],
                 out_shape=[_sds((N_HEADS, T, D_HEAD_PAD), F32), _sds((N_HEADS, T, D_HEAD_PAD), F32),
                            _sds((N_HEADS, T, D_V), F32)],
                 scratch=[pltpu.VMEM((tq, D_HEAD_PAD), F32)])(q, do, lse, delta, k, v1)


def _dq_post(dq, tabs):
    _, T, _ = dq.shape
    tm = _tile(T, HEAD_ROWS)

    def body(d_ref, cp, sa, sb, o_ref):
        d = d_ref[...] * SCALE
        o_ref[:, :D_NOPE] = d[:, :D_NOPE].astype(BF16)
        o_ref[:, D_NOPE:] = _unrope(d[:, D_NOPE:], cp[...], sa[...], sb[...]).astype(BF16)

    blk = pl.BlockSpec((None, tm, D_HEAD_PAD), lambda i, h: (h, i, 0))
    tab = pl.BlockSpec((tm, LANES), lambda i, h: (i, 0))
    return _call(body, name="dq_unrope", grid=(T // tm, N_HEADS), in_specs=[blk, tab, tab, tab],
                 out_specs=pl.BlockSpec((tm, D_HEAD_PAD), lambda i, h: (i, h)),
                 out_shape=_sds((T, N_HEADS * D_HEAD_PAD), BF16))(dq, *tabs)


def _dk_post(dk, dv, tabs, dh, kr_blk):
    _, T, _ = dk.shape
    tm = _tile(T, 1024)

    def body(dk_ref, dv_ref, cp, sa, sb, dh_ref, dkv_o, dkr_o, sc):
        h = pl.program_id(1)
        d = dk_ref[...] * LN_2
        dkv_o[:, :D_NOPE] = d[:, :D_NOPE].astype(BF16)
        dkv_o[:, D_NOPE:] = dv_ref[...].astype(BF16)

        @pl.when(h == 0)
        def _():
            sc[...] = d[:, D_NOPE:]

        @pl.when(h > 0)
        def _():
            sc[...] += d[:, D_NOPE:]

        @pl.when(h == N_HEADS - 1)
        def _():
            dkr_o[...] = _unrope(sc[...], cp[...], sa[...], sb[...]).astype(BF16)

    tab = pl.BlockSpec((tm, LANES), lambda i, h: (i, 0))
    return _call(body, name="dk_unrope", grid=(T // tm, N_HEADS),
                 in_specs=[pl.BlockSpec((None, tm, D_HEAD_PAD), lambda i, h: (h, i, 0)),
                           pl.BlockSpec((None, tm, D_V), lambda i, h: (h, i, 0)), tab, tab, tab,
                           pl.BlockSpec(memory_space=pl.ANY)],
                 out_specs=[pl.BlockSpec((tm, D_HEAD_PAD), lambda i, h: (i, h)),
                            pl.BlockSpec((tm, LANES), lambda i, h: (i, kr_blk))],
                 out_shape=[_sds((T, N_HEADS * D_HEAD_PAD), BF16), _sds(dh.shape, BF16)],
                 scratch=[pltpu.VMEM((tm, LANES), F32)], aliases={5: 1})(dk, dv, *tabs, dh)


def _latent_bwd(name, dproj, w_heads, h, col_blk, g, dh):
    T, HD = dproj.shape
    tm = _tile(T, 1024)

    def ep(acc, ex, out, first):
        dx, dg = _rms_bwd(acc, ex[0][...], ex[1][...])
        out[0][...] = dx.astype(BF16)
        _acc_out(out[1], dg, first)

    return _matmul(name, dproj, w_heads, dims=NT, grid=(T // tm, 1, 1),
                   a_spec=pl.BlockSpec((tm, HD), lambda i, j, k: (i, 0)),
                   b_spec=pl.BlockSpec((LORA, HD), lambda i, j, k: (0, 0)),
                   acc_shape=(tm, LORA),
                   extras=[(h, pl.BlockSpec((tm, LORA), lambda i, j, k: (i, col_blk))),
                           (g, pl.BlockSpec((1, LORA), lambda i, j, k: (0, 0)))],
                   outs=[(_sds(dh.shape, BF16), pl.BlockSpec((tm, LORA), lambda i, j, k: (i, col_blk))),
                         (_sds((1, LORA), F32), pl.BlockSpec((1, LORA), lambda i, j, k: (0, 0)))],
                   epilogue=ep, into=dh)


def _head_weight_grad(name, latent, dproj):
    T = dproj.shape[0]
    tk = _tile(T, 2048)
    return _matmul(name, latent, dproj, dims=TN, grid=(N_HEADS, 1, T // tk),
                   a_spec=pl.BlockSpec((tk, LORA), lambda i, j, k: (k, 0)),
                   b_spec=pl.BlockSpec((tk, D_HEAD_PAD), lambda i, j, k: (k, i)),
                   acc_shape=(LORA, D_HEAD_PAD),
                   outs=[(_sds((N_HEADS, LORA, D_HEAD_PAD), F32),
                          pl.BlockSpec((None, LORA, D_HEAD_PAD), lambda i, j, k: (i, 0, 0)))],
                   epilogue=_store())[0]


def _weight_grad(name, a, b, tm_pref=1024, tn_pref=1024, stacked_cols=None):
    T, M = a.shape
    N = b.shape[1]
    tk = _tile(T, 2048)
    tm = _tile(M, tm_pref)
    if stacked_cols is None:
        tn = _tile(N, tn_pref)
        out = (_sds((M, N), F32), pl.BlockSpec((tm, tn), lambda i, j, k: (i, j)))
    else:
        tn = _tile(stacked_cols, tn_pref)
        per = stacked_cols // tn
        out = (_sds((N // stacked_cols, M, stacked_cols), F32),
               pl.BlockSpec((None, tm, tn), lambda i, j, k: (j // per, i, j % per)))
    return _matmul(name, a, b, dims=TN, grid=(M // tm, N // tn, T // tk),
                   a_spec=pl.BlockSpec((tk, tm), lambda i, j, k: (k, i)),
                   b_spec=pl.BlockSpec((tk, tn), lambda i, j, k: (k, j)),
                   acc_shape=(tm, tn), outs=[out], epilogue=_store())[0]


def _small_names():
    return ["ln_in_g", "ln_in_b", "g_cq", "g_ckv", "conv_b", "g_conv_ln", "b_conv_ln", "g_ln1", "b_ln1", "g_ln2", "b_ln2"]


def _pack(vecs):
    flat = jnp.concatenate([v.reshape(-1) for v in vecs])
    assert flat.shape[0] % (8 * LANES) == 0
    return flat.reshape(-1, LANES)


def _unpack(packed, like):
    flat, out, off = packed.reshape(-1), [], 0
    for v in like:
        out.append(flat[off:off + v.size].reshape(v.shape))
        off += v.size
    return out


def kernel(x, positions, ln_in_g, ln_in_b, w_in, g_cq, w_uq, g_ckv, w_uk, w_uv, conv_w, conv_b, g_conv_ln, b_conv_ln, w_out, g_ln1, b_ln1, w_ff1, w_ff2, g_ln2, b_ln2, loss_target, m_ln_in_g, m_ln_in_b, m_w_in, m_g_cq, m_w_uq, m_g_ckv, m_w_uk, m_w_uv, m_conv_w, m_conv_b, m_g_conv_ln, m_b_conv_ln, m_w_out, m_g_ln1, m_b_ln1, m_w_ff1, m_w_ff2, m_g_ln2, m_b_ln2, v_ln_in_g, v_ln_in_b, v_w_in, v_g_cq, v_w_uq, v_g_ckv, v_w_uk, v_w_uv, v_conv_w, v_conv_b, v_g_conv_ln, v_b_conv_ln, v_w_out, v_g_ln1, v_b_ln1, v_w_ff1, v_w_ff2, v_g_ln2, v_b_ln2):
    args = dict(locals())
    T, D = x.shape[1], x.shape[2]
    C = D - N_HEADS * D_V
    Fs = w_ff1.shape[2]
    F = N_DEV * Fs
    n_in = N_DEV * w_in.shape[2]
    n_in_p = 2 * C + 2 * LORA + LANES
    assert n_in == 2 * LORA + D_ROPE + 2 * C and w_uq.shape[2] == D_QK and conv_w.shape[2] * N_DEV == C

    xs, tgt = x[0], loss_target[0]
    row = lambda v_: v_.reshape(1, -1)

    gather = lambda k_: [False] * k_
    ag_in = _exchange_start("ag_in_start", [w_in[0].astype(BF16).reshape(-1, LANES)], gather(1))
    ag_heads = _exchange_start("ag_heads_start", [w_uq[0].astype(BF16), w_uk[0].astype(BF16), w_uv[0].astype(BF16),
                                                  conv_w[0]], gather(4))
    ag_ff = _exchange_start("ag_ff_start", [w_out[0].astype(BF16), w_ff1[0].astype(BF16), w_ff2[0].astype(BF16)],
                            gather(3))
    started = ag_in[4] + ag_heads[4] + ag_ff[4]

    half = D_ROPE // 2
    inv_freq = ROPE_BASE ** (-jnp.arange(half, dtype=F32) * (2.0 / D_ROPE))
    inv_freq = jnp.tile(inv_freq, LANES // half).reshape(1, LANES)
    tabs = _rope_tables(positions.reshape(T, 1), inv_freq)

    x0, x0b = _ln_in(xs, row(ln_in_g) + started, row(ln_in_b))

    (g_w_in,) = _exchange_wait("ag_in_wait", ag_in, gather(1), after=x0b)
    w_in_f = jnp.transpose(g_w_in.reshape(N_DEV, D, n_in // N_DEV), (1, 0, 2)).reshape(D, n_in)
    s_cq, s_ckv, s_kr, s_a, s_g = 0, LORA, 2 * LORA, 2 * LORA + D_ROPE, 2 * LORA + D_ROPE + C
    w_in_p = jnp.concatenate([w_in_f[:, s_a:s_g], w_in_f[:, s_g:], w_in_f[:, s_cq:s_ckv], w_in_f[:, s_ckv:s_kr],
                              w_in_f[:, s_kr:s_a], jnp.zeros((D, LANES - D_ROPE), BF16)], axis=1)

    tm, tn = _tile(T, 1024), _tile(n_in_p, 640)
    h = _matmul("h_proj", x0b, w_in_p, dims=NN, grid=(T // tm, n_in_p // tn, 1),
                a_spec=pl.BlockSpec((tm, D), lambda i, j, k: (i, 0)),
                b_spec=pl.BlockSpec((D, tn), lambda i, j, k: (0, j)), acc_shape=(tm, tn),
                outs=[(_sds((T, n_in_p), F32), pl.BlockSpec((tm, tn), lambda i, j, k: (i, j)))],
                epilogue=_store())[0]

    u, cqn, ckvn, kr = _mid(h, g_cq, g_ckv, tabs, C)
    g_w_uq, g_w_uk, g_w_uv, g_conv_w = _exchange_wait("ag_heads_wait", ag_heads, gather(4), after=cqn)
    w_uq_p = jnp.pad(g_w_uq, ((0, 0), (0, 0), (0, D_HEAD_PAD - D_QK)))
    w_ukv = jnp.concatenate([g_w_uk, g_w_uv], axis=2)
    conv_w_f = jnp.pad(jnp.transpose(g_conv_w, (1, 0, 2)).reshape(CONV_W, C), ((0, CONV_W_PAD - CONV_W), (0, 0)))
    q = _q_proj(cqn, w_uq_p, tabs)
    kf, vf = _kv_proj(ckvn, w_ukv, kr)
    attn, lse = _flash_fwd(q, kf, vf, D)
    conv_c = _conv_fwd(u, conv_w_f, conv_b)
    cat = _conv_post(conv_c, g_conv_ln, b_conv_ln, attn)
    g_w_out, g_w_ff1, g_w_ff2 = _exchange_wait("ag_ff_wait", ag_ff, gather(3), after=cat)
    w_out_f = g_w_out.reshape(D, D)
    w_ff2_f = g_w_ff2.reshape(F, D)

    def ep_ln1(acc, ex, out, first):
        z1 = ALPHA * ex[0][...] + acc
        xhat, _ = _ln_stats(z1)
        x1 = xhat * ex[1][...] + ex[2][...]
        out[0][...] = z1
        out[1][...] = x1
        out[2][...] = x1.astype(BF16)

    tm = _tile(T, 256)
    rowblk = pl.BlockSpec((tm, D), lambda i, j, k: (i, 0))
    vecD = pl.BlockSpec((1, D), lambda i, j, k: (0, 0))
    z1, x1, x1b = _matmul("mix_ln1", cat, w_out_f, dims=NN, grid=(T // tm, 1, 1), a_spec=rowblk,
                          b_spec=pl.BlockSpec((D, D), lambda i, j, k: (0, 0)), acc_shape=(tm, D),
                          extras=[(x0, rowblk), (g_ln1, vecD), (b_ln1, vecD)],
                          outs=[(_sds((T, D), F32), rowblk), (_sds((T, D), F32), rowblk), (_sds((T, D), BF16), rowblk)],
                          epilogue=ep_ln1, ep_rows=EPILOGUE_ROWS)

    def ep_ff1(acc, ex, out, first):
        r = jnp.maximum(acc, 0.0)
        out[0][...] = (r * r).astype(BF16)
        out[1][...] = r.astype(BF16)

    tm, tn = _tile(T, 1024), _tile(Fs, 1024)
    per = Fs // tn
    fblk = pl.BlockSpec((tm, tn), lambda i, j, k: (i, j))
    f_act, r_act = _matmul("ff1_relu2", x1b, g_w_ff1, dims=NN, grid=(T // tm, F // tn, 1),
                           a_spec=pl.BlockSpec((tm, D), lambda i, j, k: (i, 0)),
                           b_spec=pl.BlockSpec((None, D, tn), lambda i, j, k: (j // per, 0, j % per)),
                           acc_shape=(tm, tn), outs=[(_sds((T, F), BF16), fblk), (_sds((T, F), BF16), fblk)],
                           epilogue=ep_ff1)

    def ep_ln2(acc, ex, out, first):
        g2 = ex[2][...]
        z2 = ALPHA * ex[0][...] + acc
        xhat, rstd = _ln_stats(z2)
        err = xhat * g2 + ex[3][...] - ex[1][...]
        part = 0.5 * jnp.sum(jnp.mean(err * err, axis=-1, keepdims=True))
        _acc_out(out[2], jnp.zeros((8, LANES), F32) + part, first)
        dy = err * (1.0 / D)
        _acc_out(out[3], _colsum(dy * xhat), first)
        _acc_out(out[4], _colsum(dy), first)
        dz2 = _ln_bwd(dy, xhat, rstd, g2)
        out[0][...] = dz2
        out[1][...] = dz2.astype(BF16)

    tm, tk = _tile(T, 512), _tile(F, 1024)
    rowblk = pl.BlockSpec((tm, D), lambda i, j, k: (i, 0))
    dz2, dz2b, loss_blk, dg_ln2, db_ln2 = _matmul(
        "ff2_ln2_loss", f_act, w_ff2_f, dims=NN, grid=(T // tm, 1, F // tk),
        a_spec=pl.BlockSpec((tm, tk), lambda i, j, k: (i, k)), b_spec=pl.BlockSpec((tk, D), lambda i, j, k: (k, 0)),
        acc_shape=(tm, D), extras=[(x1, rowblk), (tgt, rowblk), (g_ln2, vecD), (b_ln2, vecD)],
        outs=[(_sds((T, D), F32), rowblk), (_sds((T, D), BF16), rowblk),
              (_sds((8, LANES), F32), pl.BlockSpec((8, LANES), lambda i, j, k: (0, 0))),
              (_sds((1, D), F32), vecD), (_sds((1, D), F32), vecD)],
        epilogue=ep_ln2, ep_rows=EPILOGUE_ROWS)
    loss = lax.psum(loss_blk[0, 0], ("x", "y", "c"))

    def ep_dpre(acc, ex, out, first):
        out[0][...] = (acc * (2.0 * ex[0][...].astype(F32))).astype(BF16)

    tm, tn = _tile(T, 1024), _tile(F, 1024)
    fblk = pl.BlockSpec((tm, tn), lambda i, j, k: (i, j))
    dpre = _matmul("ff2_dgrad", dz2b, w_ff2_f, dims=NT, grid=(T // tm, F // tn, 1),
                   a_spec=pl.BlockSpec((tm, D), lambda i, j, k: (i, 0)), b_spec=pl.BlockSpec((tn, D), lambda i, j, k: (j, 0)),
                   acc_shape=(tm, tn), extras=[(r_act, fblk)], outs=[(_sds((T, F), BF16), fblk)], epilogue=ep_dpre)[0]

    dw_ff2 = _weight_grad("ff2_wgrad", f_act, dz2b).reshape(N_DEV, Fs, D)
    dw_ff1 = _weight_grad("ff1_wgrad", x1b, dpre, stacked_cols=Fs)
    scatter = lambda k_: [True] * k_
    rs_ff = _exchange_start("rs_ff_start", [dw_ff2, dw_ff1], scatter(2))

    def ep_ln1_bwd(acc, ex, out, first):
        dx1 = ALPHA * ex[0][...] + acc
        xhat, rstd = _ln_stats(ex[1][...])
        _acc_out(out[2], _colsum(dx1 * xhat), first)
        _acc_out(out[3], _colsum(dx1), first)
        dz1 = _ln_bwd(dx1, xhat, rstd, ex[2][...])
        out[0][...] = dz1
        out[1][...] = dz1.astype(BF16)

    tm, tk = _tile(T, 512), _tile(Fs, 1024)
    per = Fs // tk
    rowblk = pl.BlockSpec((tm, D), lambda i, j, k: (i, 0))
    dz1, dz1b, dg_ln1, db_ln1 = _matmul(
        "ff1_dgrad_ln1_bwd", dpre, g_w_ff1, dims=NT, grid=(T // tm, 1, F // tk),
        a_spec=pl.BlockSpec((tm, tk), lambda i, j, k: (i, k)),
        b_spec=pl.BlockSpec((None, D, tk), lambda i, j, k: (k // per, 0, k % per)),
        acc_shape=(tm, D), extras=[(dz2, rowblk), (z1, rowblk), (g_ln1 + rs_ff[4], vecD)],
        outs=[(_sds((T, D), F32), rowblk), (_sds((T, D), BF16), rowblk), (_sds((1, D), F32), vecD), (_sds((1, D), F32), vecD)],
        epilogue=ep_ln1_bwd, ep_rows=EPILOGUE_ROWS)

    dw_out = _weight_grad("out_wgrad", cat, dz1b).reshape(N_DEV, D // N_DEV, D)
    rs_out = _exchange_start("rs_out_start", [dw_out], scatter(1))
    tm, tn = _tile(T, 1024), _tile(D, 1024)
    dcat = _matmul("out_dgrad", dz1b, w_out_f, dims=NT, grid=(T // tm, D // tn, 1),
                   a_spec=pl.BlockSpec((tm, D), lambda i, j, k: (i, 0)), b_spec=pl.BlockSpec((tn, D), lambda i, j, k: (j, 0)),
                   acc_shape=(tm, tn), outs=[(_sds((T, D), F32), pl.BlockSpec((tm, tn), lambda i, j, k: (i, j)))],
                   epilogue=_store())[0]

    dc, dg_conv_ln, db_conv_ln = _conv_post_bwd(dcat, conv_c, g_conv_ln + rs_out[4], b_conv_ln)
    du, dconv_w_p, dconv_b = _conv_bwd(dc, u, conv_w_f)
    dh = _glu_bwd(du, h, C)

    delta, do_heads = _attn_delta(dcat, cat)
    dq, dk, dv = _flash_bwd(q, kf, vf, do_heads, lse, delta)
    dq_raw = _dq_post(dq, tabs)
    cq_blk = (2 * C) // LORA
    dkv, dh = _dk_post(dk, dv, tabs, dh, (2 * C + 2 * LORA) // LANES)
    by_rank = lambda w_: jnp.transpose(w_, (1, 0, 2)).reshape(LORA, N_HEADS * D_HEAD_PAD)
    dh, dg_cq = _latent_bwd("q_dgrad_rms_bwd", dq_raw, by_rank(w_uq_p), h, cq_blk, g_cq, dh)
    dh, dg_ckv = _latent_bwd("kv_dgrad_rms_bwd", dkv, by_rank(w_ukv), h, cq_blk + 1, g_ckv, dh)

    dw_in_p = _weight_grad("in_wgrad", x0b, dh, tn_pref=640)
    dw_in_f = jnp.concatenate([dw_in_p[:, 2 * C:2 * C + 2 * LORA + D_ROPE], dw_in_p[:, :2 * C]], axis=1)
    dw_in = jnp.transpose(dw_in_f.reshape(D, N_DEV, n_in // N_DEV), (1, 0, 2)).reshape(N_DEV, -1, LANES)
    rs_in =_exchange_start("rs_in_start", [dw_in], scatter(1))

    def ep_ln_in_bwd(acc, ex, out, first):
        dx0 = ALPHA * ex[0][...] + acc
        xhat, rstd = _ln_stats(ex[1][...])
        _acc_out(out[1], _colsum(dx0 * xhat), first)
        _acc_out(out[2], _colsum(dx0), first)
        out[0][...] = _ln_bwd(dx0, xhat, rstd, ex[2][...])

    tm, tk = _tile(T, 512), _tile(n_in_p, 640)
    rowblk = pl.BlockSpec((tm, D), lambda i, j, k: (i, 0))
    grad_x, dg_ln_in, db_ln_in = _matmul(
        "in_dgrad_ln_in_bwd", dh, w_in_p, dims=NT, grid=(T // tm, 1, n_in_p // tk),
        a_spec=pl.BlockSpec((tm, tk), lambda i, j, k: (i, k)), b_spec=pl.BlockSpec((D, tk), lambda i, j, k: (0, k)),
        acc_shape=(tm, D), extras=[(dz1, rowblk), (xs, rowblk), (row(ln_in_g) + rs_in[4], vecD)],
        outs=[(_sds((T, D), F32), rowblk), (_sds((1, D), F32), vecD), (_sds((1, D), F32), vecD)],
        epilogue=ep_ln_in_bwd, ep_rows=EPILOGUE_ROWS)

    dw_uq = _head_weight_grad("uq_wgrad", cqn, dq_raw)[:, :, :D_QK]
    dw_ukv = _head_weight_grad("ukv_wgrad", ckvn, dkv)
    dw_uk, dw_uv = dw_ukv[:, :, :D_NOPE], dw_ukv[:, :, D_NOPE:]
    dconv_w = jnp.transpose(dconv_w_p[:CONV_W].reshape(CONV_W, N_DEV, C // N_DEV), (1, 0, 2))
    rs_heads = _exchange_start("rs_heads_start", [dw_uq, dw_uk, dw_uv, dconv_w], scatter(4))

    small = dict(ln_in_g=dg_ln_in, ln_in_b=db_ln_in, g_cq=dg_cq, g_ckv=dg_ckv, conv_b=dconv_b, g_conv_ln=dg_conv_ln,
                 b_conv_ln=db_conv_ln, g_ln1=dg_ln1, b_ln1=db_ln1, g_ln2=dg_ln2, b_ln2=db_ln2)
    names = _small_names()
    rs_small = _exchange_start("rs_small_start", [_pack([small[n] for n in names])], gather(1))
    res = {}

    def update(group, parts, after):
        last = after
        for n, p in zip(group, parts):
            shard = lambda a: a[0].reshape(p.shape[1:])
            outs_n = _adamw("adamw_" + n, p, shard(args[n]), shard(args["m_" + n]), shard(args["v_" + n]))
            res[n] = [o.reshape(args[n].shape) for o in outs_n]
            last = outs_n[0]
        return last

    done = update(["w_ff2", "w_ff1"], _exchange_wait("rs_ff_wait", rs_ff, scatter(2), after=grad_x), grad_x)
    done = update(["w_out"], _exchange_wait("rs_out_wait", rs_out, scatter(1), after=done), done)
    done = update(["w_in"], _exchange_wait("rs_in_wait", rs_in, scatter(1), after=done), done)
    done = update(["w_uq", "w_uk", "w_uv", "conv_w"],
                  _exchange_wait("rs_heads_wait", rs_heads, scatter(4), after=done), done)
    (small_parts,) = _exchange_wait("rs_small_wait", rs_small, gather(1), after=done)
    packed = _adamw("adamw_small", small_parts, _pack([args[n] for n in names]), _pack([args["m_" + n] for n in names]),
                    _pack([args["v_" + n] for n in names]))
    like = [args[n] for n in names]
    unpacked = [_unpack(p, like) for p in packed]
    for i, n in enumerate(names):
        res[n] = [unpacked[kind][i] for kind in range(4)]

    order = ["ln_in_g", "ln_in_b", "w_in", "g_cq", "w_uq", "g_ckv", "w_uk", "w_uv", "conv_w", "conv_b", "g_conv_ln",
             "b_conv_ln", "w_out", "g_ln1", "b_ln1", "w_ff1", "w_ff2", "g_ln2", "b_ln2"]
    outs = [loss, grad_x.reshape(x.shape)]
    for kind in range(4):
        outs += [res[n][kind] for n in order]
    return tuple(outs)
```

```python
import jax
import jax.numpy as jnp
from jax import lax
from jax.experimental import pallas as pl
from jax.experimental.pallas import tpu as pltpu

F32 = jnp.float32
BF16 = jnp.bfloat16

N_HEADS = 8
D_NOPE = 128
D_ROPE = 64
D_V = 128
D_QK = D_NOPE + D_ROPE
D_HEAD_PAD = 256
LORA = 512
CONV_W = 31
CONV_HALF = CONV_W // 2
CONV_W_PAD = 32
HALO = 16
LN_EPS = 1e-5
RMS_EPS = 1e-6
ALPHA = 2.0 ** 0.25
SCALE = float(D_QK) ** -0.5
LOG2_E = 1.4426950408889634
LN_2 = 0.6931471805599453
Q_SCALE = SCALE * LOG2_E
ROPE_BASE = 10000.0
ADAM_LR, ADAM_B1, ADAM_B2, ADAM_EPS, ADAM_WD, ADAM_STEP = 0.001, 0.9, 0.999, 1e-08, 0.01, 10

N_DEV = 8
LANES = 128
VMEM_LIMIT_V7X = 56 * 1024 * 1024

NN = (((1,), (0,)), ((), ()))
NT = (((1,), (1,)), ((), ()))
TN = (((0,), (0,)), ((), ()))


def _call(body, *, name, grid, in_specs, out_specs, out_shape, scratch=(), aliases=None):
    params = pltpu.CompilerParams(dimension_semantics=("arbitrary",) * len(grid),
                                  vmem_limit_bytes=VMEM_LIMIT_V7X)
    return pl.pallas_call(body, name=name, grid=grid, in_specs=in_specs, out_specs=out_specs,
                          out_shape=out_shape, scratch_shapes=scratch, compiler_params=params,
                          input_output_aliases=aliases or {})


def _tile(n, pref):
    if n <= pref:
        return n
    t = (pref // LANES) * LANES
    while t > LANES and n % t:
        t -= LANES
    assert n % t == 0, (n, pref)
    return t


def _sds(shape, dtype):
    return jax.ShapeDtypeStruct(shape, dtype)


def _ln_stats(z):
    mu = jnp.mean(z, axis=-1, keepdims=True)
    zc = z - mu
    var = jnp.mean(zc * zc, axis=-1, keepdims=True)
    rstd = lax.rsqrt(var + LN_EPS)
    return zc * rstd, rstd


def _ln_bwd(dy, xhat, rstd, g):
    gd = dy * g
    m1 = jnp.mean(gd, axis=-1, keepdims=True)
    m2 = jnp.mean(gd * xhat, axis=-1, keepdims=True)
    return rstd * (gd - m1 - xhat * m2)


def _rms(x, g):
    return x * lax.rsqrt(jnp.mean(x * x, axis=-1, keepdims=True) + RMS_EPS) * g


def _rms_bwd(dy, x, g):
    r = lax.rsqrt(jnp.mean(x * x, axis=-1, keepdims=True) + RMS_EPS)
    dxn = dy * g
    dx = r * dxn - x * (r * r * r) * jnp.mean(dxn * x, axis=-1, keepdims=True)
    dg = jnp.sum(dy * x * r, axis=0, keepdims=True)
    return dx, dg


def _sigmoid(x):
    return 1.0 / (1.0 + jnp.exp(-x))


def _rope(x, cos_p, sin_a, sin_b):
    return x * cos_p + pltpu.roll(x, 96, 1) * sin_a + pltpu.roll(x, 32, 1) * sin_b


def _unrope(d, cos_p, sin_a, sin_b):
    return d * cos_p - pltpu.roll(d, 96, 1) * sin_a - pltpu.roll(d, 32, 1) * sin_b


def _colsum(v):
    return jnp.sum(v, axis=0, keepdims=True)


def _acc_out(ref, val, first):
    if first is False:
        ref[...] += val
        return

    @pl.when(first)
    def _():
        ref[...] = val

    @pl.when(jnp.logical_not(first))
    def _():
        ref[...] += val


class _Rows:
    def __init__(self, ref, sl):
        self.ref, self.sl = ref, sl

    def __getitem__(self, idx):
        assert idx is Ellipsis
        return self.ref[self.sl, :]

    def __setitem__(self, idx, val):
        assert idx is Ellipsis
        self.ref[self.sl, :] = val


def _matmul(name, a, b, *, dims, grid, a_spec, b_spec, acc_shape, outs, epilogue, extras=(), ep_rows=None, into=None):
    nk = grid[2]
    ne, no = len(extras), len(outs)
    tm = acc_shape[0]
    n_in = 2 + ne + (0 if into is None else 1)

    def finish(acc_rows, ex, out):
        first = pl.program_id(0) == 0
        if ep_rows is None or ep_rows >= tm:
            epilogue(acc_rows(slice(None)), ex, out, first)
            return
        for r0 in range(0, tm, ep_rows):
            sl = slice(r0, r0 + ep_rows)
            view = lambda r: _Rows(r, sl) if r.shape[0] == tm else r
            epilogue(acc_rows(sl), [view(r) for r in ex], [view(r) for r in out], first if r0 == 0 else False)

    def body(*refs):
        a_ref, b_ref = refs[0], refs[1]
        ex = refs[2:2 + ne]
        out = refs[n_in:n_in + no]
        if nk == 1:
            part = lax.dot_general(a_ref[...], b_ref[...], dims, preferred_element_type=F32)
            finish(lambda sl: part[sl, :], ex, out)
        else:
            acc = refs[n_in + no]
            k = pl.program_id(2)

            @pl.when(k == 0)
            def _():
                acc[...] = jnp.zeros_like(acc)

            acc[...] += lax.dot_general(a_ref[...], b_ref[...], dims, preferred_element_type=F32)

            @pl.when(k == nk - 1)
            def _():
                finish(lambda sl: acc[sl, :], ex, out)

    scratch = [] if nk == 1 else [pltpu.VMEM(acc_shape, F32)]
    ins = [a, b] + [e for e, _ in extras]
    in_specs = [a_spec, b_spec] + [s for _, s in extras]
    aliases = {}
    if into is not None:
        ins.append(into)
        in_specs.append(pl.BlockSpec(memory_space=pl.ANY))
        aliases = {n_in - 1: 0}
    return _call(body, name=name, grid=grid, in_specs=in_specs, out_specs=[s for _, s in outs],
                 out_shape=[o for o, _ in outs], scratch=scratch, aliases=aliases)(*ins)


def _store(dtype=F32):
    def ep(acc, ex, out, first):
        out[0][...] = acc.astype(dtype)
    return ep


def _mesh_pos():
    return lax.axis_index("x"), lax.axis_index("y"), lax.axis_index("c")


def _flip(v, bit):
    return 1 - v if bit else v


_HBM = pl.BlockSpec(memory_space=pltpu.HBM)
_SEM = pl.BlockSpec(memory_space=pltpu.SEMAPHORE)
_EFFECT = pltpu.SideEffectType.DATAFLOW_SIDE_EFFECTING


def _my_slot():
    x, y, c = _mesh_pos()
    return 4 * x + 2 * y + c


def _exchange_copies(srcs, lands, send_sems, recv_sems, stacked, receives=True):
    x, y, c = _mesh_pos()
    me = 4 * x + 2 * y + c
    pairs = []
    for w in range(len(srcs)):
        for k in range(1, N_DEV):
            peer = (_flip(x, k & 4), _flip(y, k & 2), _flip(c, k & 1))
            peer_slot = 4 * peer[0] + 2 * peer[1] + peer[2]
            to_peer = srcs[w].at[peer_slot] if stacked[w] else srcs[w]
            mine = srcs[w].at[me] if stacked[w] else srcs[w]
            s = w * (N_DEV - 1) + k - 1
            sems = dict(send_sem=send_sems.at[s], recv_sem=recv_sems.at[s],
                        device_id=peer, device_id_type=pl.DeviceIdType.MESH)
            send = pltpu.make_async_remote_copy(src_ref=to_peer, dst_ref=lands[w].at[me], **sems)
            recv = pltpu.make_async_remote_copy(src_ref=mine, dst_ref=lands[w].at[peer_slot], **sems) if receives else None
            pairs.append((send, recv))
    return pairs


def _exchange_start(name, srcs, stacked):
    n = len(srcs)
    land_shapes = [s.shape if st else (N_DEV,) + s.shape for s, st in zip(srcs, stacked)]

    def body(*refs):
        src, land = refs[:n], refs[n:2 * n]
        send_sems, recv_sems = refs[2 * n], refs[2 * n + 1]
        token = refs[-1]
        for send, _ in _exchange_copies(src, land, send_sems, recv_sems, stacked, receives=False):
            send.start()
        token[...] = jnp.zeros_like(token)

    hbm = lambda a: pltpu.with_memory_space_constraint(a, pltpu.HBM)
    outs = pl.pallas_call(
        body, name=name,
        out_shape=(pltpu.SemaphoreType.DMA((n * (N_DEV - 1),)), pltpu.SemaphoreType.DMA((n * (N_DEV - 1),)),
                   *[pltpu.HBM(s.shape, s.dtype) for s in srcs],
                   *[pltpu.HBM(ls, s.dtype) for ls, s in zip(land_shapes, srcs)],
                   _sds((8, LANES), F32)),
        in_specs=[_HBM] * (2 * n),
        out_specs=(_SEM, _SEM, *[_HBM] * (2 * n), pl.BlockSpec(memory_space=pltpu.VMEM)),
        input_output_aliases={i: 2 + i for i in range(2 * n)},
        compiler_params=pltpu.CompilerParams(has_side_effects=_EFFECT),
    )(*[hbm(s) for s in srcs], *[hbm(lax.empty(ls, s.dtype)) for ls, s in zip(land_shapes, srcs)])
    return outs[0], outs[1], list(outs[2:2 + n]), list(outs[2 + n:2 + 2 * n]), outs[-1][0, 0]


def _exchange_wait(name, started, stacked, after):
    srcs, lands = _wait_call(name, started, stacked, after)
    me = _my_slot()
    full = []
    for src, land, st in zip(srcs, lands, stacked):
        own = lax.dynamic_index_in_dim(src, me, 0, keepdims=True) if st else src[None]
        full.append(lax.dynamic_update_index_in_dim(land, own, me, 0))
    return full


def _wait_call(name, started, stacked, after):
    send_sems, recv_sems, srcs, lands, _ = started
    n = len(srcs)

    def body(*refs):
        src, land = refs[:n], refs[n:2 * n]
        s_sems, r_sems = refs[2 * n], refs[2 * n + 1]
        for send, recv in _exchange_copies(src, land, s_sems, r_sems, stacked):
            send.wait_send()
            recv.wait_recv()

    outs = pl.pallas_call(
        body, name=name,
        out_shape=tuple(pltpu.HBM(a.shape, a.dtype) for a in srcs + lands),
        in_specs=[_HBM] * (2 * n) + [_SEM, _SEM, pl.BlockSpec(memory_space=pl.ANY)],
        out_specs=[_HBM] * (2 * n),
        input_output_aliases={i: i for i in range(2 * n)},
        compiler_params=pltpu.CompilerParams(has_side_effects=_EFFECT),
    )(*srcs, *lands, send_sems, recv_sems, after)
    return outs[:n], outs[n:]


def _adamw(name, parts, w, m, v):
    rows, cols = w.shape
    cap = max(8, (LANES * 1024) // cols)
    tr = rows
    if rows > cap:
        tr = (cap // 8) * 8
        while rows % tr:
            tr -= 8
    c1 = 1.0 / (1.0 - ADAM_B1 ** ADAM_STEP)
    c2 = 1.0 / (1.0 - ADAM_B2 ** ADAM_STEP)

    def body(p_ref, w_ref, m_ref, v_ref, g_o, d_o, m_o, v_o):
        g = p_ref[0]
        for s in range(1, N_DEV):
            g = g + p_ref[s]
        mn = ADAM_B1 * m_ref[...] + (1.0 - ADAM_B1) * g
        vn = ADAM_B2 * v_ref[...] + (1.0 - ADAM_B2) * (g * g)
        g_o[...] = g
        m_o[...] = mn
        v_o[...] = vn
        d_o[...] = -ADAM_LR * ((mn * c1) / (jnp.sqrt(vn * c2) + ADAM_EPS) + ADAM_WD * w_ref[...])

    blk = pl.BlockSpec((tr, cols), lambda i: (i, 0))
    return _call(body, name=name, grid=(rows // tr,),
                 in_specs=[pl.BlockSpec((N_DEV, tr, cols), lambda i: (0, i, 0)), blk, blk, blk],
                 out_specs=[blk] * 4, out_shape=[_sds((rows, cols), F32)] * 4)(parts, w, m, v)


def _rope_tables(pos_col, inv_freq):
    T = pos_col.shape[0]
    tm = _tile(T, 1024)

    def body(p_ref, f_ref, c_o, sa_o, sb_o):
        ang = p_ref[...].astype(F32) * f_ref[...]
        lane = lax.broadcasted_iota(jnp.int32, ang.shape, 1)
        cs, sn = jnp.cos(ang), jnp.sin(ang)
        c_o[...] = jnp.where(lane < D_ROPE, cs, 0.0)
        sa_o[...] = jnp.where(lane < D_ROPE // 2, -sn, 0.0)
        sb_o[...] = jnp.where((lane >= D_ROPE // 2) & (lane < D_ROPE), sn, 0.0)

    blk = pl.BlockSpec((tm, LANES), lambda i: (i, 0))
    return _call(body, name="rope_tables", grid=(T // tm,),
                 in_specs=[pl.BlockSpec((tm, 1), lambda i: (i, 0)), pl.BlockSpec((1, LANES), lambda i: (0, 0))],
                 out_specs=[blk] * 3, out_shape=[_sds((T, LANES), F32)] * 3)(pos_col, inv_freq)


def _ln_in(x, g, b):
    T, D = x.shape
    tm = _tile(T, 512)

    def body(x_ref, g_ref, b_ref, o32, o16):
        xhat, _ = _ln_stats(x_ref[...])
        y = xhat * g_ref[...] + b_ref[...]
        o32[...] = y
        o16[...] = y.astype(BF16)

    blk = pl.BlockSpec((tm, D), lambda i: (i, 0))
    vec = pl.BlockSpec((1, D), lambda i: (0, 0))
    return _call(body, name="ln_in", grid=(T // tm,), in_specs=[blk, vec, vec], out_specs=[blk, blk],
                 out_shape=[_sds((T, D), F32), _sds((T, D), BF16)])(x, g, b)


def _mid(h, g_cq, g_ckv, tabs, C):
    T = h.shape[0]
    tm = _tile(T, 256)
    cq_blk, kr_blk = (2 * C) // LORA, (2 * C + 2 * LORA) // LANES

    def body(a_ref, gt_ref, cq_ref, ckv_ref, kr_ref, gq_ref, gkv_ref, cp, sa, sb, u_o, cqn_o, ckvn_o, kr_o):
        u_o[...] = a_ref[...] * _sigmoid(gt_ref[...])
        cqn_o[...] = _rms(cq_ref[...], gq_ref[...]).astype(BF16)
        ckvn_o[...] = _rms(ckv_ref[...], gkv_ref[...]).astype(BF16)
        kr_o[...] = _rope(kr_ref[...], cp[...], sa[...], sb[...]).astype(BF16)

    def col(w, j):
        return pl.BlockSpec((tm, w), lambda i: (i, j))

    vec = pl.BlockSpec((1, LORA), lambda i: (0, 0))
    return _call(body, name="mid_norm_glu", grid=(T // tm,),
                 in_specs=[col(C, 0), col(C, 1), col(LORA, cq_blk), col(LORA, cq_blk + 1), col(LANES, kr_blk),
                           vec, vec, col(LANES, 0), col(LANES, 0), col(LANES, 0)],
                 out_specs=[col(C, 0), col(LORA, 0), col(LORA, 0), col(LANES, 0)],
                 out_shape=[_sds((T, C), F32), _sds((T, LORA), BF16), _sds((T, LORA), BF16), _sds((T, LANES), BF16)],
                 )(h, h, h, h, h, g_cq, g_ckv, *tabs)


def _q_proj(cqn, w_uq_p, tabs):
    T = cqn.shape[0]
    tm = _tile(T, HEAD_ROWS)

    def body(c_ref, w_ref, cp, sa, sb, o_ref):
        q = jnp.dot(c_ref[...], w_ref[...], preferred_element_type=F32)
        o_ref[:, :D_NOPE] = (q[:, :D_NOPE] * Q_SCALE).astype(BF16)
        o_ref[:, D_NOPE:] = (_rope(q[:, D_NOPE:], cp[...], sa[...], sb[...]) * Q_SCALE).astype(BF16)

    tab = pl.BlockSpec((tm, LANES), lambda i, h: (i, 0))
    return _call(body, name="q_proj_rope", grid=(T // tm, N_HEADS),
                 in_specs=[pl.BlockSpec((tm, LORA), lambda i, h: (i, 0)),
                           pl.BlockSpec((None, LORA, D_HEAD_PAD), lambda i, h: (h, 0, 0)), tab, tab, tab],
                 out_specs=pl.BlockSpec((None, tm, D_HEAD_PAD), lambda i, h: (h, i, 0)),
                 out_shape=_sds((N_HEADS, T, D_HEAD_PAD), BF16))(cqn, w_uq_p, *tabs)


def _kv_proj(ckvn, w_ukv, kr):
    T = ckvn.shape[0]
    tm = _tile(T, HEAD_ROWS)

    def body(c_ref, w_ref, kr_ref, k_o, v_o):
        kv = jnp.dot(c_ref[...], w_ref[...], preferred_element_type=F32)
        k_o[:, :D_NOPE] = kv[:, :D_NOPE].astype(BF16)
        k_o[:, D_NOPE:] = kr_ref[...]
        v_o[:, :D_V] = kv[:, D_NOPE:].astype(BF16)
        v_o[:, D_V:] = jnp.ones((tm, D_V), BF16)

    return _call(body, name="kv_proj", grid=(T // tm, N_HEADS),
                 in_specs=[pl.BlockSpec((tm, LORA), lambda i, h: (i, 0)),
                           pl.BlockSpec((None, LORA, D_NOPE + D_V), lambda i, h: (h, 0, 0)),
                           pl.BlockSpec((tm, LANES), lambda i, h: (i, 0))],
                 out_specs=[pl.BlockSpec((None, tm, D_HEAD_PAD), lambda i, h: (h, i, 0)),
                            pl.BlockSpec((None, tm, 2 * D_V), lambda i, h: (h, i, 0))],
                 out_shape=[_sds((N_HEADS, T, D_HEAD_PAD), BF16), _sds((N_HEADS, T, 2 * D_V), BF16)])(ckvn, w_ukv, kr)


def _flash_fwd(q, k, v1, out_cols):
    _, T, _ = q.shape
    tq, tk = _tile(T, FLASH_TQ), _tile(T, FLASH_TK)
    nkv, reps = T // tk, tk // LANES

    def body(q_ref, k_ref, v_ref, o_ref, lse_ref, m_sc, acc_sc):
        m_sc[...] = jnp.full_like(m_sc, -jnp.inf)
        acc_sc[...] = jnp.zeros_like(acc_sc)
        qv = q_ref[...]

        def rows(j):
            return pl.ds(pl.multiple_of(j * tk, tk), tk)

        def scores(j):
            return lax.dot_general(qv, k_ref[rows(j), :], NT, preferred_element_type=F32)

        def update(s, j):
            m_prev = m_sc[...]
            m_new = jnp.maximum(m_prev, jnp.max(s, axis=1, keepdims=True))
            a = jnp.exp2(m_prev - m_new)
            p = jnp.exp2(s - jnp.tile(m_new, (1, reps)))
            pv = jnp.dot(p.astype(BF16), v_ref[rows(j), :], preferred_element_type=F32)
            acc_sc[...] = jnp.tile(a, (1, 2)) * acc_sc[...] + pv
            m_sc[...] = m_new

        def step(j, carry):
            update(scores(j), j)
            return carry

        lax.fori_loop(0, nkv, step, 0, unroll=FLASH_UNROLL_FWD if nkv % FLASH_UNROLL_FWD == 0 else 1)
        acc = acc_sc[...]
        l = acc[:, D_V:]
        o_ref[...] = (acc[:, :D_V] / l).astype(BF16)
        lse_ref[...] = m_sc[...] + jnp.log(l) * LOG2_E

    return _call(body, name="flash_fwd", grid=(N_HEADS, T // tq),
                 in_specs=[pl.BlockSpec((None, tq, D_HEAD_PAD), lambda h, i: (h, i, 0)),
                           pl.BlockSpec((None, T, D_HEAD_PAD), lambda h, i: (h, 0, 0)),
                           pl.BlockSpec((None, T, 2 * D_V), lambda h, i: (h, 0, 0))],
                 out_specs=[pl.BlockSpec((tq, D_V), lambda h, i: (i, h)),
                            pl.BlockSpec((None, tq, LANES), lambda h, i: (h, i, 0))],
                 out_shape=[_sds((T, out_cols), BF16), _sds((N_HEADS, T, LANES), F32)],
                 scratch=[pltpu.VMEM((tq, LANES), F32), pltpu.VMEM((tq, 2 * D_V), F32)])(q, k, v1)


def _halo_specs(tm, cb, n_t):
    r = tm // HALO
    return [pl.BlockSpec((HALO, cb), lambda jc, i: (jnp.maximum(i * r - 1, 0), jc)),
            pl.BlockSpec((tm, cb), lambda jc, i: (i, jc)),
            pl.BlockSpec((HALO, cb), lambda jc, i: (jnp.minimum((i + 1) * r, n_t * r - 1), jc))]


def _fill_ext(ext, prev_ref, cur_ref, next_ref, i, n_t, tm):
    ext[0:HALO, :] = jnp.where(i > 0, prev_ref[...], 0.0)
    ext[HALO:HALO + tm, :] = cur_ref[...]
    ext[HALO + tm:, :] = jnp.where(i < n_t - 1, next_ref[...], 0.0)


HEAD_ROWS = 2048
FLASH_TQ = 512
FLASH_TK = 512
FLASH_UNROLL_FWD = 16
FLASH_UNROLL_BWD = 16
EPILOGUE_ROWS = 128
CONV_ROWS = 64


def _fill_shifted(shifted, ext):
    rows = shifted.shape[1]
    for s in range(1, 8):
        shifted[s, :, :] = ext[s:s + rows, :]


def _window(ext, shifted, start, rows):
    s, base = start % 8, start - start % 8
    return ext[base:base + rows, :] if s == 0 else shifted[s, base:base + rows, :]


def _conv_fwd(u, w_pad, bias):
    T, C = u.shape
    tm, cb = _tile(T, 256), _tile(C, 256)
    n_t = T // tm
    rb = min(CONV_ROWS, tm)

    def body(up, uc, un, w_ref, b_ref, c_o, ext, shifted):
        i = pl.program_id(1)
        _fill_ext(ext, up, uc, un, i, n_t, tm)
        _fill_shifted(shifted, ext)
        for r0 in range(0, tm, rb):
            acc = jnp.zeros((rb, cb), F32) + b_ref[...]
            for k in range(CONV_W):
                acc = acc + w_ref[k:k + 1, :] * _window(ext, shifted, r0 + k + 1, rb)
            c_o[r0:r0 + rb, :] = acc

    return _call(body, name="conv_fwd", grid=(C // cb, n_t),
                 in_specs=_halo_specs(tm, cb, n_t) + [pl.BlockSpec((CONV_W_PAD, cb), lambda jc, i: (0, jc)),
                                                      pl.BlockSpec((1, cb), lambda jc, i: (0, jc))],
                 out_specs=pl.BlockSpec((tm, cb), lambda jc, i: (i, jc)),
                 out_shape=_sds((T, C), F32),
                 scratch=[pltpu.VMEM((tm + 2 * HALO, cb), F32),
                          pltpu.VMEM((8, tm + 2 * HALO - 8, cb), F32)])(u, u, u, w_pad, bias)


def _conv_post(c, g, b, cat):
    T, C = c.shape
    assert cat.shape == (T, 2 * C)
    tm = _tile(T, 512)

    def body(c_ref, g_ref, b_ref, cat_ref, o_ref):
        xhat, _ = _ln_stats(c_ref[...])
        y = xhat * g_ref[...] + b_ref[...]
        o_ref[...] = (y * _sigmoid(y)).astype(BF16)

    blk = pl.BlockSpec((tm, C), lambda i: (i, 0))
    vec = pl.BlockSpec((1, C), lambda i: (0, 0))
    return _call(body, name="conv_ln_silu", grid=(T // tm,),
                 in_specs=[blk, vec, vec, pl.BlockSpec(memory_space=pl.ANY)],
                 out_specs=pl.BlockSpec((tm, C), lambda i: (i, 1)),
                 out_shape=_sds(cat.shape, BF16), aliases={3: 0})(c, g, b, cat)


def _conv_post_bwd(dcat, c, g, b):
    T, C = c.shape
    tm = _tile(T, 512)

    def body(d_ref, c_ref, g_ref, b_ref, dc_o, dg_o, db_o):
        first = pl.program_id(0) == 0
        xhat, rstd = _ln_stats(c_ref[...])
        y = xhat * g_ref[...] + b_ref[...]
        sg = _sigmoid(y)
        dy = d_ref[...] * (sg * (1.0 + y * (1.0 - sg)))
        _acc_out(dg_o, _colsum(dy * xhat), first)
        _acc_out(db_o, _colsum(dy), first)
        dc_o[...] = _ln_bwd(dy, xhat, rstd, g_ref[...])

    blk = pl.BlockSpec((tm, C), lambda i: (i, 0))
    vec = pl.BlockSpec((1, C), lambda i: (0, 0))
    return _call(body, name="conv_ln_silu_bwd", grid=(T // tm,),
                 in_specs=[pl.BlockSpec((tm, C), lambda i: (i, 1)), blk, vec, vec],
                 out_specs=[blk, vec, vec],
                 out_shape=[_sds((T, C), F32), _sds((1, C), F32), _sds((1, C), F32)])(dcat, c, g, b)


def _conv_bwd(dc, u, w_pad):
    T, C = u.shape
    tm, cb = _tile(T, 256), _tile(C, 256)
    n_t = T // tm
    rb = min(CONV_ROWS, tm)

    def body(dp, dcur, dn, uc, w_ref, du_o, dw_o, db_o, dext, shifted, dw_sc):
        i = pl.program_id(1)
        _fill_ext(dext, dp, dcur, dn, i, n_t, tm)
        _fill_shifted(shifted, dext)

        @pl.when(i == 0)
        def _():
            dw_sc[...] = jnp.zeros_like(dw_sc)

        for r0 in range(0, tm, rb):
            acc = jnp.zeros((rb, cb), F32)
            u_here = uc[r0:r0 + rb, :]
            for k in range(CONV_W):
                win = _window(dext, shifted, r0 + 2 * HALO - 1 - k, rb)
                acc = acc + w_ref[k:k + 1, :] * win
                dw_sc[k] += jnp.sum((win * u_here).reshape(rb // 8, 8, cb), axis=0)
            du_o[r0:r0 + rb, :] = acc
        dw_sc[CONV_W] += jnp.sum(dcur[...].reshape(tm // 8, 8, cb), axis=0)

        @pl.when(i == n_t - 1)
        def _():
            red = jnp.sum(dw_sc[...], axis=1)
            row = lax.broadcasted_iota(jnp.int32, red.shape, 0)
            dw_o[...] = jnp.where(row < CONV_W, red, 0.0)
            db_o[...] = jnp.sum(jnp.where(row == CONV_W, red, 0.0), axis=0, keepdims=True)

    return _call(body, name="conv_bwd", grid=(C // cb, n_t),
                 in_specs=_halo_specs(tm, cb, n_t) + [pl.BlockSpec((tm, cb), lambda jc, i: (i, jc)),
                                                      pl.BlockSpec((CONV_W_PAD, cb), lambda jc, i: (0, jc))],
                 out_specs=[pl.BlockSpec((tm, cb), lambda jc, i: (i, jc)),
                            pl.BlockSpec((CONV_W_PAD, cb), lambda jc, i: (0, jc)),
                            pl.BlockSpec((1, cb), lambda jc, i: (0, jc))],
                 out_shape=[_sds((T, C), F32), _sds((CONV_W_PAD, C), F32), _sds((1, C), F32)],
                 scratch=[pltpu.VMEM((tm + 2 * HALO, cb), F32), pltpu.VMEM((8, tm + 2 * HALO - 8, cb), F32),
                          pltpu.VMEM((CONV_W_PAD, 8, cb), F32)])(dc, dc, dc, u, w_pad)


def _glu_bwd(du, h, C):
    T = du.shape[0]
    tm = _tile(T, 512)

    def body(du_ref, a_ref, gt_ref, o_ref):
        sg = _sigmoid(gt_ref[...])
        du_v = du_ref[...]
        o_ref[:, :C] = (du_v * sg).astype(BF16)
        o_ref[:, C:] = (du_v * a_ref[...] * sg * (1.0 - sg)).astype(BF16)

    return _call(body, name="glu_bwd", grid=(T // tm,),
                 in_specs=[pl.BlockSpec((tm, C), lambda i: (i, 0)), pl.BlockSpec((tm, C), lambda i: (i, 0)),
                           pl.BlockSpec((tm, C), lambda i: (i, 1))],
                 out_specs=pl.BlockSpec((tm, 2 * C), lambda i: (i, 0)),
                 out_shape=_sds(h.shape, BF16))(du, h, h)


def _attn_delta(dcat, attn):
    T = attn.shape[0]
    tm = _tile(T, HEAD_ROWS)

    def body(d_ref, o_ref, dl_o, dob_o):
        d = d_ref[...]
        dl = jnp.sum(d * o_ref[...].astype(F32), axis=1, keepdims=True)
        dl_o[...] = jnp.broadcast_to(dl, (tm, LANES))
        dob_o[...] = d.astype(BF16)

    blk = pl.BlockSpec((tm, D_V), lambda i, h: (i, h))
    hblk = pl.BlockSpec((None, tm, D_V), lambda i, h: (h, i, 0))
    return _call(body, name="attn_delta", grid=(T // tm, N_HEADS), in_specs=[blk, blk], out_specs=[hblk, hblk],
                 out_shape=[_sds((N_HEADS, T, LANES), F32), _sds((N_HEADS, T, D_V), BF16)])(dcat, attn)


def _flash_bwd(q, k, v1, do, lse, delta):
    _, T, _ = q.shape
    tq, tk = _tile(T, FLASH_TQ), _tile(T, FLASH_TK)
    nkv, reps = T // tk, tk // LANES

    def body(q_ref, do_ref, lse_ref, dl_ref, k_ref, v_ref, dq_o, dk_o, dv_o, dq_sc):
        @pl.when(pl.program_id(1) == 0)
        def _():
            dk_o[...] = jnp.zeros_like(dk_o)
            dv_o[...] = jnp.zeros_like(dv_o)

        qv, dov = q_ref[...], do_ref[...]
        lse_t = jnp.tile(lse_ref[...], (1, reps))
        dl_t = jnp.tile(dl_ref[...], (1, reps))
        dq_sc[...] = jnp.zeros_like(dq_sc)

        def rows(j):
            return pl.ds(pl.multiple_of(j * tk, tk), tk)

        def scores(j):
            s = lax.dot_general(qv, k_ref[rows(j), :], NT, preferred_element_type=F32)
            dp = lax.dot_general(dov, v_ref[rows(j), :D_V], NT, preferred_element_type=F32)
            return s, dp

        def update(s_dp, j):
            s, dp = s_dp
            p = jnp.exp2(s - lse_t)
            ds = (p * (dp - dl_t)).astype(BF16)
            dv_o[rows(j), :] += lax.dot_general(p.astype(BF16), dov, TN, preferred_element_type=F32)
            dk_o[rows(j), :] += lax.dot_general(ds, qv, TN, preferred_element_type=F32)
            dq_sc[...] += jnp.dot(ds, k_ref[rows(j), :], preferred_element_type=F32)

        def step(j, carry):
            update(scores(j), j)
            return carry

        lax.fori_loop(0, nkv, step, 0, unroll=FLASH_UNROLL_BWD if nkv % FLASH_UNROLL_BWD == 0 else 1)
        dq_o[...] = dq_sc[...]

    def tile(w):
        return pl.BlockSpec((None, tq, w), lambda h, i: (h, i, 0))

    def whole(w):
        return pl.BlockSpec((None, T, w), lambda h, i: (h, 0, 0))

    return _call(body, name="flash_bwd", grid=(N_HEADS, T // tq),
                 in_specs=[tile(D_HEAD_PAD), tile(D_V), tile(LANES), tile(LANES), whole(D_HEAD_PAD), whole(2 * D_V)],
                 out_specs=[tile(D_HEAD_PAD), whole(D_HEAD_PAD), whole(D_V)],
                 out_shape=[_sds((N_HEADS, T, D_HEAD_PAD), F32), _sds((N_HEADS, T, D_HEAD_PAD), F32),
                            _sds((N_HEADS, T, D_V), F32)],
                 scratch=[pltpu.VMEM((tq, D_HEAD_PAD), F32)])(q, do, lse, delta, k, v1)


def _dq_post(dq, tabs):
    _, T, _ = dq.shape
    tm = _tile(T, HEAD_ROWS)

    def body(d_ref, cp, sa, sb, o_ref):
        d = d_ref[...] * SCALE
        o_ref[:, :D_NOPE] = d[:, :D_NOPE].astype(BF16)
        o_ref[:, D_NOPE:] = _unrope(d[:, D_NOPE:], cp[...], sa[...], sb[...]).astype(BF16)

    blk = pl.BlockSpec((None, tm, D_HEAD_PAD), lambda i, h: (h, i, 0))
    tab = pl.BlockSpec((tm, LANES), lambda i, h: (i, 0))
    return _call(body, name="dq_unrope", grid=(T // tm, N_HEADS), in_specs=[blk, tab, tab, tab],
                 out_specs=pl.BlockSpec((tm, D_HEAD_PAD), lambda i, h: (i, h)),
                 out_shape=_sds((T, N_HEADS * D_HEAD_PAD), BF16))(dq, *tabs)


def _dk_post(dk, dv, tabs, dh, kr_blk):
    _, T, _ = dk.shape
    tm = _tile(T, 1024)

    def body(dk_ref, dv_ref, cp, sa, sb, dh_ref, dkv_o, dkr_o, sc):
        h = pl.program_id(1)
        d = dk_ref[...] * LN_2
        dkv_o[:, :D_NOPE] = d[:, :D_NOPE].astype(BF16)
        dkv_o[:, D_NOPE:] = dv_ref[...].astype(BF16)

        @pl.when(h == 0)
        def _():
            sc[...] = d[:, D_NOPE:]

        @pl.when(h > 0)
        def _():
            sc[...] += d[:, D_NOPE:]

        @pl.when(h == N_HEADS - 1)
        def _():
            dkr_o[...] = _unrope(sc[...], cp[...], sa[...], sb[...]).astype(BF16)

    tab = pl.BlockSpec((tm, LANES), lambda i, h: (i, 0))
    return _call(body, name="dk_unrope", grid=(T // tm, N_HEADS),
                 in_specs=[pl.BlockSpec((None, tm, D_HEAD_PAD), lambda i, h: (h, i, 0)),
                           pl.BlockSpec((None, tm, D_V), lambda i, h: (h, i, 0)), tab, tab, tab,
                           pl.BlockSpec(memory_space=pl.ANY)],
                 out_specs=[pl.BlockSpec((tm, D_HEAD_PAD), lambda i, h: (i, h)),
                            pl.BlockSpec((tm, LANES), lambda i, h: (i, kr_blk))],
                 out_shape=[_sds((T, N_HEADS * D_HEAD_PAD), BF16), _sds(dh.shape, BF16)],
                 scratch=[pltpu.VMEM((tm, LANES), F32)], aliases={5: 1})(dk, dv, *tabs, dh)


def _latent_bwd(name, dproj, w_heads, h, col_blk, g, dh):
    T, HD = dproj.shape
    tm = _tile(T, 1024)

    def ep(acc, ex, out, first):
        dx, dg = _rms_bwd(acc, ex[0][...], ex[1][...])
        out[0][...] = dx.astype(BF16)
        _acc_out(out[1], dg, first)

    return _matmul(name, dproj, w_heads, dims=NT, grid=(T // tm, 1, 1),
                   a_spec=pl.BlockSpec((tm, HD), lambda i, j, k: (i, 0)),
                   b_spec=pl.BlockSpec((LORA, HD), lambda i, j, k: (0, 0)),
                   acc_shape=(tm, LORA),
                   extras=[(h, pl.BlockSpec((tm, LORA), lambda i, j, k: (i, col_blk))),
                           (g, pl.BlockSpec((1, LORA), lambda i, j, k: (0, 0)))],
                   outs=[(_sds(dh.shape, BF16), pl.BlockSpec((tm, LORA), lambda i, j, k: (i, col_blk))),
                         (_sds((1, LORA), F32), pl.BlockSpec((1, LORA), lambda i, j, k: (0, 0)))],
                   epilogue=ep, into=dh)


def _head_weight_grad(name, latent, dproj):
    T = dproj.shape[0]
    tk = _tile(T, 2048)
    return _matmul(name, latent, dproj, dims=TN, grid=(N_HEADS, 1, T // tk),
                   a_spec=pl.BlockSpec((tk, LORA), lambda i, j, k: (k, 0)),
                   b_spec=pl.BlockSpec((tk, D_HEAD_PAD), lambda i, j, k: (k, i)),
                   acc_shape=(LORA, D_HEAD_PAD),
                   outs=[(_sds((N_HEADS, LORA, D_HEAD_PAD), F32),
                          pl.BlockSpec((None, LORA, D_HEAD_PAD), lambda i, j, k: (i, 0, 0)))],
                   epilogue=_store())[0]


def _weight_grad(name, a, b, tm_pref=1024, tn_pref=1024, stacked_cols=None):
    T, M = a.shape
    N = b.shape[1]
    tk = _tile(T, 2048)
    tm = _tile(M, tm_pref)
    if stacked_cols is None:
        tn = _tile(N, tn_pref)
        out = (_sds((M, N), F32), pl.BlockSpec((tm, tn), lambda i, j, k: (i, j)))
    else:
        tn = _tile(stacked_cols, tn_pref)
        per = stacked_cols // tn
        out = (_sds((N // stacked_cols, M, stacked_cols), F32),
               pl.BlockSpec((None, tm, tn), lambda i, j, k: (j // per, i, j % per)))
    return _matmul(name, a, b, dims=TN, grid=(M // tm, N // tn, T // tk),
                   a_spec=pl.BlockSpec((tk, tm), lambda i, j, k: (k, i)),
                   b_spec=pl.BlockSpec((tk, tn), lambda i, j, k: (k, j)),
                   acc_shape=(tm, tn), outs=[out], epilogue=_store())[0]


def _small_names():
    return ["ln_in_g", "ln_in_b", "g_cq", "g_ckv", "conv_b", "g_conv_ln", "b_conv_ln", "g_ln1", "b_ln1", "g_ln2", "b_ln2"]


def _pack(vecs):
    flat = jnp.concatenate([v.reshape(-1) for v in vecs])
    assert flat.shape[0] % (8 * LANES) == 0
    return flat.reshape(-1, LANES)


def _unpack(packed, like):
    flat, out, off = packed.reshape(-1), [], 0
    for v in like:
        out.append(flat[off:off + v.size].reshape(v.shape))
        off += v.size
    return out


def kernel(x, positions, ln_in_g, ln_in_b, w_in, g_cq, w_uq, g_ckv, w_uk, w_uv, conv_w, conv_b, g_conv_ln, b_conv_ln, w_out, g_ln1, b_ln1, w_ff1, w_ff2, g_ln2, b_ln2, loss_target, m_ln_in_g, m_ln_in_b, m_w_in, m_g_cq, m_w_uq, m_g_ckv, m_w_uk, m_w_uv, m_conv_w, m_conv_b, m_g_conv_ln, m_b_conv_ln, m_w_out, m_g_ln1, m_b_ln1, m_w_ff1, m_w_ff2, m_g_ln2, m_b_ln2, v_ln_in_g, v_ln_in_b, v_w_in, v_g_cq, v_w_uq, v_g_ckv, v_w_uk, v_w_uv, v_conv_w, v_conv_b, v_g_conv_ln, v_b_conv_ln, v_w_out, v_g_ln1, v_b_ln1, v_w_ff1, v_w_ff2, v_g_ln2, v_b_ln2):
    args = dict(locals())
    T, D = x.shape[1], x.shape[2]
    C = D - N_HEADS * D_V
    Fs = w_ff1.shape[2]
    F = N_DEV * Fs
    n_in = N_DEV * w_in.shape[2]
    n_in_p = 2 * C + 2 * LORA + LANES
    assert n_in == 2 * LORA + D_ROPE + 2 * C and w_uq.shape[2] == D_QK and conv_w.shape[2] * N_DEV == C

    xs, tgt = x[0], loss_target[0]
    row = lambda v_: v_.reshape(1, -1)

    gather = lambda k_: [False] * k_
    ag_in = _exchange_start("ag_in_start", [w_in[0].astype(BF16).reshape(-1, LANES)], gather(1))
    ag_heads = _exchange_start("ag_heads_start", [w_uq[0].astype(BF16), w_uk[0].astype(BF16), w_uv[0].astype(BF16),
                                                  conv_w[0]], gather(4))
    ag_ff = _exchange_start("ag_ff_start", [w_out[0].astype(BF16), w_ff1[0].astype(BF16), w_ff2[0].astype(BF16)],
                            gather(3))
    started = ag_in[4] + ag_heads[4] + ag_ff[4]

    half = D_ROPE // 2
    inv_freq = ROPE_BASE ** (-jnp.arange(half, dtype=F32) * (2.0 / D_ROPE))
    inv_freq = jnp.tile(inv_freq, LANES // half).reshape(1, LANES)
    tabs = _rope_tables(positions.reshape(T, 1), inv_freq)

    x0, x0b = _ln_in(xs, row(ln_in_g) + started, row(ln_in_b))

    (g_w_in,) = _exchange_wait("ag_in_wait", ag_in, gather(1), after=x0b)
    w_in_f = jnp.transpose(g_w_in.reshape(N_DEV, D, n_in // N_DEV), (1, 0, 2)).reshape(D, n_in)
    s_cq, s_ckv, s_kr, s_a, s_g = 0, LORA, 2 * LORA, 2 * LORA + D_ROPE, 2 * LORA + D_ROPE + C
    w_in_p = jnp.concatenate([w_in_f[:, s_a:s_g], w_in_f[:, s_g:], w_in_f[:, s_cq:s_ckv], w_in_f[:, s_ckv:s_kr],
                              w_in_f[:, s_kr:s_a], jnp.zeros((D, LANES - D_ROPE), BF16)], axis=1)

    tm, tn = _tile(T, 1024), _tile(n_in_p, 640)
    h = _matmul("h_proj", x0b, w_in_p, dims=NN, grid=(T // tm, n_in_p // tn, 1),
                a_spec=pl.BlockSpec((tm, D), lambda i, j, k: (i, 0)),
                b_spec=pl.BlockSpec((D, tn), lambda i, j, k: (0, j)), acc_shape=(tm, tn),
                outs=[(_sds((T, n_in_p), F32), pl.BlockSpec((tm, tn), lambda i, j, k: (i, j)))],
                epilogue=_store())[0]

    u, cqn, ckvn, kr = _mid(h, g_cq, g_ckv, tabs, C)
    g_w_uq, g_w_uk, g_w_uv, g_conv_w = _exchange_wait("ag_heads_wait", ag_heads, gather(4), after=cqn)
    w_uq_p = jnp.pad(g_w_uq, ((0, 0), (0, 0), (0, D_HEAD_PAD - D_QK)))
    w_ukv = jnp.concatenate([g_w_uk, g_w_uv], axis=2)
    conv_w_f = jnp.pad(jnp.transpose(g_conv_w, (1, 0, 2)).reshape(CONV_W, C), ((0, CONV_W_PAD - CONV_W), (0, 0)))
    q = _q_proj(cqn, w_uq_p, tabs)
    kf, vf = _kv_proj(ckvn, w_ukv, kr)
    attn, lse = _flash_fwd(q, kf, vf, D)
    conv_c = _conv_fwd(u, conv_w_f, conv_b)
    cat = _conv_post(conv_c, g_conv_ln, b_conv_ln, attn)
    g_w_out, g_w_ff1, g_w_ff2 = _exchange_wait("ag_ff_wait", ag_ff, gather(3), after=cat)
    w_out_f = g_w_out.reshape(D, D)
    w_ff2_f = g_w_ff2.reshape(F, D)

    def ep_ln1(acc, ex, out, first):
        z1 = ALPHA * ex[0][...] + acc
        xhat, _ = _ln_stats(z1)
        x1 = xhat * ex[1][...] + ex[2][...]
        out[0][...] = z1
        out[1][...] = x1
        out[2][...] = x1.astype(BF16)

    tm = _tile(T, 256)
    rowblk = pl.BlockSpec((tm, D), lambda i, j, k: (i, 0))
    vecD = pl.BlockSpec((1, D), lambda i, j, k: (0, 0))
    z1, x1, x1b = _matmul("mix_ln1", cat, w_out_f, dims=NN, grid=(T // tm, 1, 1), a_spec=rowblk,
                          b_spec=pl.BlockSpec((D, D), lambda i, j, k: (0, 0)), acc_shape=(tm, D),
                          extras=[(x0, rowblk), (g_ln1, vecD), (b_ln1, vecD)],
                          outs=[(_sds((T, D), F32), rowblk), (_sds((T, D), F32), rowblk), (_sds((T, D), BF16), rowblk)],
                          epilogue=ep_ln1, ep_rows=EPILOGUE_ROWS)

    def ep_ff1(acc, ex, out, first):
        r = jnp.maximum(acc, 0.0)
        out[0][...] = (r * r).astype(BF16)
        out[1][...] = r.astype(BF16)

    tm, tn = _tile(T, 1024), _tile(Fs, 1024)
    per = Fs // tn
    fblk = pl.BlockSpec((tm, tn), lambda i, j, k: (i, j))
    f_act, r_act = _matmul("ff1_relu2", x1b, g_w_ff1, dims=NN, grid=(T // tm, F // tn, 1),
                           a_spec=pl.BlockSpec((tm, D), lambda i, j, k: (i, 0)),
                           b_spec=pl.BlockSpec((None, D, tn), lambda i, j, k: (j // per, 0, j % per)),
                           acc_shape=(tm, tn), outs=[(_sds((T, F), BF16), fblk), (_sds((T, F), BF16), fblk)],
                           epilogue=ep_ff1)

    def ep_ln2(acc, ex, out, first):
        g2 = ex[2][...]
        z2 = ALPHA * ex[0][...] + acc
        xhat, rstd = _ln_stats(z2)
        err = xhat * g2 + ex[3][...] - ex[1][...]
        part = 0.5 * jnp.sum(jnp.mean(err * err, axis=-1, keepdims=True))
        _acc_out(out[2], jnp.zeros((8, LANES), F32) + part, first)
        dy = err * (1.0 / D)
        _acc_out(out[3], _colsum(dy * xhat), first)
        _acc_out(out[4], _colsum(dy), first)
        dz2 = _ln_bwd(dy, xhat, rstd, g2)
        out[0][...] = dz2
        out[1][...] = dz2.astype(BF16)

    tm, tk = _tile(T, 512), _tile(F, 1024)
    rowblk = pl.BlockSpec((tm, D), lambda i, j, k: (i, 0))
    dz2, dz2b, loss_blk, dg_ln2, db_ln2 = _matmul(
        "ff2_ln2_loss", f_act, w_ff2_f, dims=NN, grid=(T // tm, 1, F // tk),
        a_spec=pl.BlockSpec((tm, tk), lambda i, j, k: (i, k)), b_spec=pl.BlockSpec((tk, D), lambda i, j, k: (k, 0)),
        acc_shape=(tm, D), extras=[(x1, rowblk), (tgt, rowblk), (g_ln2, vecD), (b_ln2, vecD)],
        outs=[(_sds((T, D), F32), rowblk), (_sds((T, D), BF16), rowblk),
              (_sds((8, LANES), F32), pl.BlockSpec((8, LANES), lambda i, j, k: (0, 0))),
              (_sds((1, D), F32), vecD), (_sds((1, D), F32), vecD)],
        epilogue=ep_ln2, ep_rows=EPILOGUE_ROWS)
    loss = lax.psum(loss_blk[0, 0], ("x", "y", "c"))

    def ep_dpre(acc, ex, out, first):
        out[0][...] = (acc * (2.0 * ex[0][...].astype(F32))).astype(BF16)

    tm, tn = _tile(T, 1024), _tile(F, 1024)
    fblk = pl.BlockSpec((tm, tn), lambda i, j, k: (i, j))
    dpre = _matmul("ff2_dgrad", dz2b, w_ff2_f, dims=NT, grid=(T // tm, F // tn, 1),
                   a_spec=pl.BlockSpec((tm, D), lambda i, j, k: (i, 0)), b_spec=pl.BlockSpec((tn, D), lambda i, j, k: (j, 0)),
                   acc_shape=(tm, tn), extras=[(r_act, fblk)], outs=[(_sds((T, F), BF16), fblk)], epilogue=ep_dpre)[0]

    dw_ff2 = _weight_grad("ff2_wgrad", f_act, dz2b).reshape(N_DEV, Fs, D)
    dw_ff1 = _weight_grad("ff1_wgrad", x1b, dpre, stacked_cols=Fs)
    scatter = lambda k_: [True] * k_
    rs_ff = _exchange_start("rs_ff_start", [dw_ff2, dw_ff1], scatter(2))

    def ep_ln1_bwd(acc, ex, out, first):
        dx1 = ALPHA * ex[0][...] + acc
        xhat, rstd = _ln_stats(ex[1][...])
        _acc_out(out[2], _colsum(dx1 * xhat), first)
        _acc_out(out[3], _colsum(dx1), first)
        dz1 = _ln_bwd(dx1, xhat, rstd, ex[2][...])
        out[0][...] = dz1
        out[1][...] = dz1.astype(BF16)

    tm, tk = _tile(T, 512), _tile(Fs, 1024)
    per = Fs // tk
    rowblk = pl.BlockSpec((tm, D), lambda i, j, k: (i, 0))
    dz1, dz1b, dg_ln1, db_ln1 = _matmul(
        "ff1_dgrad_ln1_bwd", dpre, g_w_ff1, dims=NT, grid=(T // tm, 1, F // tk),
        a_spec=pl.BlockSpec((tm, tk), lambda i, j, k: (i, k)),
        b_spec=pl.BlockSpec((None, D, tk), lambda i, j, k: (k // per, 0, k % per)),
        acc_shape=(tm, D), extras=[(dz2, rowblk), (z1, rowblk), (g_ln1 + rs_ff[4], vecD)],
        outs=[(_sds((T, D), F32), rowblk), (_sds((T, D), BF16), rowblk), (_sds((1, D), F32), vecD), (_sds((1, D), F32), vecD)],
        epilogue=ep_ln1_bwd, ep_rows=EPILOGUE_ROWS)

    dw_out = _weight_grad("out_wgrad", cat, dz1b).reshape(N_DEV, D // N_DEV, D)
    rs_out = _exchange_start("rs_out_start", [dw_out], scatter(1))
    tm, tn = _tile(T, 1024), _tile(D, 1024)
    dcat = _matmul("out_dgrad", dz1b, w_out_f, dims=NT, grid=(T // tm, D // tn, 1),
                   a_spec=pl.BlockSpec((tm, D), lambda i, j, k: (i, 0)), b_spec=pl.BlockSpec((tn, D), lambda i, j, k: (j, 0)),
                   acc_shape=(tm, tn), outs=[(_sds((T, D), F32), pl.BlockSpec((tm, tn), lambda i, j, k: (i, j)))],
                   epilogue=_store())[0]

    dc, dg_conv_ln, db_conv_ln = _conv_post_bwd(dcat, conv_c, g_conv_ln + rs_out[4], b_conv_ln)
    du, dconv_w_p, dconv_b = _conv_bwd(dc, u, conv_w_f)
    dh = _glu_bwd(du, h, C)

    delta, do_heads = _attn_delta(dcat, cat)
    dq, dk, dv = _flash_bwd(q, kf, vf, do_heads, lse, delta)
    dq_raw = _dq_post(dq, tabs)
    cq_blk = (2 * C) // LORA
    dkv, dh = _dk_post(dk, dv, tabs, dh, (2 * C + 2 * LORA) // LANES)
    by_rank = lambda w_: jnp.transpose(w_, (1, 0, 2)).reshape(LORA, N_HEADS * D_HEAD_PAD)
    dh, dg_cq = _latent_bwd("q_dgrad_rms_bwd", dq_raw, by_rank(w_uq_p), h, cq_blk, g_cq, dh)
    dh, dg_ckv = _latent_bwd("kv_dgrad_rms_bwd", dkv, by_rank(w_ukv), h, cq_blk + 1, g_ckv, dh)

    dw_in_p = _weight_grad("in_wgrad", x0b, dh, tn_pref=640)
    dw_in_f = jnp.concatenate([dw_in_p[:, 2 * C:2 * C + 2 * LORA + D_ROPE], dw_in_p[:, :2 * C]], axis=1)
    dw_in = jnp.transpose(dw_in_f.reshape(D, N_DEV, n_in // N_DEV), (1, 0, 2)).reshape(N_DEV, -1, LANES)
    rs_in =_exchange_start("rs_in_start", [dw_in], scatter(1))

    def ep_ln_in_bwd(acc, ex, out, first):
        dx0 = ALPHA * ex[0][...] + acc
        xhat, rstd = _ln_stats(ex[1][...])
        _acc_out(out[1], _colsum(dx0 * xhat), first)
        _acc_out(out[2], _colsum(dx0), first)
        out[0][...] = _ln_bwd(dx0, xhat, rstd, ex[2][...])

    tm, tk = _tile(T, 512), _tile(n_in_p, 640)
    rowblk = pl.BlockSpec((tm, D), lambda i, j, k: (i, 0))
    grad_x, dg_ln_in, db_ln_in = _matmul(
        "in_dgrad_ln_in_bwd", dh, w_in_p, dims=NT, grid=(T // tm, 1, n_in_p // tk),
        a_spec=pl.BlockSpec((tm, tk), lambda i, j, k: (i, k)), b_spec=pl.BlockSpec((D, tk), lambda i, j, k: (0, k)),
        acc_shape=(tm, D), extras=[(dz1, rowblk), (xs, rowblk), (row(ln_in_g) + rs_in[4], vecD)],
        outs=[(_sds((T, D), F32), rowblk), (_sds((1, D), F32), vecD), (_sds((1, D), F32), vecD)],
        epilogue=ep_ln_in_bwd, ep_rows=EPILOGUE_ROWS)

    dw_uq = _head_weight_grad("uq_wgrad", cqn, dq_raw)[:, :, :D_QK]
    dw_ukv = _head_weight_grad("ukv_wgrad", ckvn, dkv)
    dw_uk, dw_uv = dw_ukv[:, :, :D_NOPE], dw_ukv[:, :, D_NOPE:]
    dconv_w = jnp.transpose(dconv_w_p[:CONV_W].reshape(CONV_W, N_DEV, C // N_DEV), (1, 0, 2))
    rs_heads = _exchange_start("rs_heads_start", [dw_uq, dw_uk, dw_uv, dconv_w], scatter(4))

    small = dict(ln_in_g=dg_ln_in, ln_in_b=db_ln_in, g_cq=dg_cq, g_ckv=dg_ckv, conv_b=dconv_b, g_conv_ln=dg_conv_ln,
                 b_conv_ln=db_conv_ln, g_ln1=dg_ln1, b_ln1=db_ln1, g_ln2=dg_ln2, b_ln2=db_ln2)
    names = _small_names()
    rs_small = _exchange_start("rs_small_start", [_pack([small[n] for n in names])], gather(1))
    res = {}

    def update(group, parts, after):
        last = after
        for n, p in zip(group, parts):
            shard = lambda a: a[0].reshape(p.shape[1:])
            outs_n = _adamw("adamw_" + n, p, shard(args[n]), shard(args["m_" + n]), shard(args["v_" + n]))
            res[n] = [o.reshape(args[n].shape) for o in outs_n]
            last = outs_n[0]
        return last

    done = update(["w_ff2", "w_ff1"], _exchange_wait("rs_ff_wait", rs_ff, scatter(2), after=grad_x), grad_x)
    done = update(["w_out"], _exchange_wait("rs_out_wait", rs_out, scatter(1), after=done), done)
    done = update(["w_in"], _exchange_wait("rs_in_wait", rs_in, scatter(1), after=done), done)
    done = update(["w_uq", "w_uk", "w_uv", "conv_w"],
                  _exchange_wait("rs_heads_wait", rs_heads, scatter(4), after=done), done)
    (small_parts,) = _exchange_wait("rs_small_wait", rs_small, gather(1), after=done)
    packed = _adamw("adamw_small", small_parts, _pack([args[n] for n in names]), _pack([args["m_" + n] for n in names]),
                    _pack([args["v_" + n] for n in names]))
    like = [args[n] for n in names]
    unpacked = [_unpack(p, like) for p in packed]
    for i, n in enumerate(names):
        res[n] = [unpacked[kind][i] for kind in range(4)]

    order = ["ln_in_g", "ln_in_b", "w_in", "g_cq", "w_uq", "g_ckv", "w_uk", "w_uv", "conv_w", "conv_b", "g_conv_ln",
             "b_conv_ln", "w_out", "g_ln1", "b_ln1", "w_ff1", "w_ff2", "g_ln2", "b_ln2"]
    outs = [loss, grad_x.reshape(x.shape)]
    for kind in range(4):
        outs += [res[n][kind] for n in order]
    return tuple(outs)
```

```python
import jax
import jax.numpy as jnp
from jax import lax
from jax.experimental import pallas as pl
from jax.experimental.pallas import tpu as pltpu

F32 = jnp.float32
BF16 = jnp.bfloat16

N_HEADS = 8
D_NOPE = 128
D_ROPE = 64
D_V = 128
D_QK = D_NOPE + D_ROPE
D_HEAD_PAD = 256
LORA = 512
CONV_W = 31
CONV_HALF = CONV_W // 2
CONV_W_PAD = 32
HALO = 16
LN_EPS = 1e-5
RMS_EPS = 1e-6
ALPHA = 2.0 ** 0.25
SCALE = float(D_QK) ** -0.5
LOG2_E = 1.4426950408889634
LN_2 = 0.6931471805599453
Q_SCALE = SCALE * LOG2_E
ROPE_BASE = 10000.0
ADAM_LR, ADAM_B1, ADAM_B2, ADAM_EPS, ADAM_WD, ADAM_STEP = 0.001, 0.9, 0.999, 1e-08, 0.01, 10

N_DEV = 8
LANES = 128
VMEM_LIMIT_V7X = 56 * 1024 * 1024

NN = (((1,), (0,)), ((), ()))
NT = (((1,), (1,)), ((), ()))
TN = (((0,), (0,)), ((), ()))


def _call(body, *, name, grid, in_specs, out_specs, out_shape, scratch=(), aliases=None):
    params = pltpu.CompilerParams(dimension_semantics=("arbitrary",) * len(grid),
                                  vmem_limit_bytes=VMEM_LIMIT_V7X)
    return pl.pallas_call(body, name=name, grid=grid, in_specs=in_specs, out_specs=out_specs,
                          out_shape=out_shape, scratch_shapes=scratch, compiler_params=params,
                          input_output_aliases=aliases or {})


def _tile(n, pref):
    if n <= pref:
        return n
    t = (pref // LANES) * LANES
    while t > LANES and n % t:
        t -= LANES
    assert n % t == 0, (n, pref)
    return t


def _sds(shape, dtype):
    return jax.ShapeDtypeStruct(shape, dtype)


def _ln_stats(z):
    mu = jnp.mean(z, axis=-1, keepdims=True)
    zc = z - mu
    var = jnp.mean(zc * zc, axis=-1, keepdims=True)
    rstd = lax.rsqrt(var + LN_EPS)
    return zc * rstd, rstd


def _ln_bwd(dy, xhat, rstd, g):
    gd = dy * g
    m1 = jnp.mean(gd, axis=-1, keepdims=True)
    m2 = jnp.mean(gd * xhat, axis=-1, keepdims=True)
    return rstd * (gd - m1 - xhat * m2)


def _rms(x, g):
    return x * lax.rsqrt(jnp.mean(x * x, axis=-1, keepdims=True) + RMS_EPS) * g


def _rms_bwd(dy, x, g):
    r = lax.rsqrt(jnp.mean(x * x, axis=-1, keepdims=True) + RMS_EPS)
    dxn = dy * g
    dx = r * dxn - x * (r * r * r) * jnp.mean(dxn * x, axis=-1, keepdims=True)
    dg = jnp.sum(dy * x * r, axis=0, keepdims=True)
    return dx, dg


def _sigmoid(x):
    return 1.0 / (1.0 + jnp.exp(-x))


def _rope(x, cos_p, sin_a, sin_b):
    return x * cos_p + pltpu.roll(x, 96, 1) * sin_a + pltpu.roll(x, 32, 1) * sin_b


def _unrope(d, cos_p, sin_a, sin_b):
    return d * cos_p - pltpu.roll(d, 96, 1) * sin_a - pltpu.roll(d, 32, 1) * sin_b


def _colsum(v):
    return jnp.sum(v, axis=0, keepdims=True)


def _acc_out(ref, val, first):
    if first is False:
        ref[...] += val
        return

    @pl.when(first)
    def _():
        ref[...] = val

    @pl.when(jnp.logical_not(first))
    def _():
        ref[...] += val


class _Rows:
    def __init__(self, ref, sl):
        self.ref, self.sl = ref, sl

    def __getitem__(self, idx):
        assert idx is Ellipsis
        return self.ref[self.sl, :]

    def __setitem__(self, idx, val):
        assert idx is Ellipsis
        self.ref[self.sl, :] = val


def _matmul(name, a, b, *, dims, grid, a_spec, b_spec, acc_shape, outs, epilogue, extras=(), ep_rows=None, into=None):
    nk = grid[2]
    ne, no = len(extras), len(outs)
    tm = acc_shape[0]
    n_in = 2 + ne + (0 if into is None else 1)

    def finish(acc_rows, ex, out):
        first = pl.program_id(0) == 0
        if ep_rows is None or ep_rows >= tm:
            epilogue(acc_rows(slice(None)), ex, out, first)
            return
        for r0 in range(0, tm, ep_rows):
            sl = slice(r0, r0 + ep_rows)
            view = lambda r: _Rows(r, sl) if r.shape[0] == tm else r
            epilogue(acc_rows(sl), [view(r) for r in ex], [view(r) for r in out], first if r0 == 0 else False)

    def body(*refs):
        a_ref, b_ref = refs[0], refs[1]
        ex = refs[2:2 + ne]
        out = refs[n_in:n_in + no]
        if nk == 1:
            part = lax.dot_general(a_ref[...], b_ref[...], dims, preferred_element_type=F32)
            finish(lambda sl: part[sl, :], ex, out)
        else:
            acc = refs[n_in + no]
            k = pl.program_id(2)

            @pl.when(k == 0)
            def _():
                acc[...] = jnp.zeros_like(acc)

            acc[...] += lax.dot_general(a_ref[...], b_ref[...], dims, preferred_element_type=F32)

            @pl.when(k == nk - 1)
            def _():
                finish(lambda sl: acc[sl, :], ex, out)

    scratch = [] if nk == 1 else [pltpu.VMEM(acc_shape, F32)]
    ins = [a, b] + [e for e, _ in extras]
    in_specs = [a_spec, b_spec] + [s for _, s in extras]
    aliases = {}
    if into is not None:
        ins.append(into)
        in_specs.append(pl.BlockSpec(memory_space=pl.ANY))
        aliases = {n_in - 1: 0}
    return _call(body, name=name, grid=grid, in_specs=in_specs, out_specs=[s for _, s in outs],
                 out_shape=[o for o, _ in outs], scratch=scratch, aliases=aliases)(*ins)


def _store(dtype=F32):
    def ep(acc, ex, out, first):
        out[0][...] = acc.astype(dtype)
    return ep


def _mesh_pos():
    return lax.axis_index("x"), lax.axis_index("y"), lax.axis_index("c")


def _flip(v, bit):
    return 1 - v if bit else v


_HBM = pl.BlockSpec(memory_space=pltpu.HBM)
_SEM = pl.BlockSpec(memory_space=pltpu.SEMAPHORE)
_EFFECT = pltpu.SideEffectType.DATAFLOW_SIDE_EFFECTING


def _my_slot():
    x, y, c = _mesh_pos()
    return 4 * x + 2 * y + c


def _exchange_copies(srcs, lands, send_sems, recv_sems, stacked, receives=True):
    x, y, c = _mesh_pos()
    me = 4 * x + 2 * y + c
    pairs = []
    for w in range(len(srcs)):
        for k in range(1, N_DEV):
            peer = (_flip(x, k & 4), _flip(y, k & 2), _flip(c, k & 1))
            peer_slot = 4 * peer[0] + 2 * peer[1] + peer[2]
            to_peer = srcs[w].at[peer_slot] if stacked[w] else srcs[w]
            mine = srcs[w].at[me] if stacked[w] else srcs[w]
            s = w * (N_DEV - 1) + k - 1
            sems = dict(send_sem=send_sems.at[s], recv_sem=recv_sems.at[s],
                        device_id=peer, device_id_type=pl.DeviceIdType.MESH)
            send = pltpu.make_async_remote_copy(src_ref=to_peer, dst_ref=lands[w].at[me], **sems)
            recv = pltpu.make_async_remote_copy(src_ref=mine, dst_ref=lands[w].at[peer_slot], **sems) if receives else None
            pairs.append((send, recv))
    return pairs


def _exchange_start(name, srcs, stacked):
    n = len(srcs)
    land_shapes = [s.shape if st else (N_DEV,) + s.shape for s, st in zip(srcs, stacked)]

    def body(*refs):
        src, land = refs[:n], refs[n:2 * n]
        send_sems, recv_sems = refs[2 * n], refs[2 * n + 1]
        token = refs[-1]
        for send, _ in _exchange_copies(src, land, send_sems, recv_sems, stacked, receives=False):
            send.start()
        token[...] = jnp.zeros_like(token)

    hbm = lambda a: pltpu.with_memory_space_constraint(a, pltpu.HBM)
    outs = pl.pallas_call(
        body, name=name,
        out_shape=(pltpu.SemaphoreType.DMA((n * (N_DEV - 1),)), pltpu.SemaphoreType.DMA((n * (N_DEV - 1),)),
                   *[pltpu.HBM(s.shape, s.dtype) for s in srcs],
                   *[pltpu.HBM(ls, s.dtype) for ls, s in zip(land_shapes, srcs)],
                   _sds((8, LANES), F32)),
        in_specs=[_HBM] * (2 * n),
        out_specs=(_SEM, _SEM, *[_HBM] * (2 * n), pl.BlockSpec(memory_space=pltpu.VMEM)),
        input_output_aliases={i: 2 + i for i in range(2 * n)},
        compiler_params=pltpu.CompilerParams(has_side_effects=_EFFECT),
    )(*[hbm(s) for s in srcs], *[hbm(lax.empty(ls, s.dtype)) for ls, s in zip(land_shapes, srcs)])
    return outs[0], outs[1], list(outs[2:2 + n]), list(outs[2 + n:2 + 2 * n]), outs[-1][0, 0]


def _exchange_wait(name, started, stacked, after):
    srcs, lands = _wait_call(name, started, stacked, after)
    me = _my_slot()
    full = []
    for src, land, st in zip(srcs, lands, stacked):
        own = lax.dynamic_index_in_dim(src, me, 0, keepdims=True) if st else src[None]
        full.append(lax.dynamic_update_index_in_dim(land, own, me, 0))
    return full


def _wait_call(name, started, stacked, after):
    send_sems, recv_sems, srcs, lands, _ = started
    n = len(srcs)

    def body(*refs):
        src, land = refs[:n], refs[n:2 * n]
        s_sems, r_sems = refs[2 * n], refs[2 * n + 1]
        for send, recv in _exchange_copies(src, land, s_sems, r_sems, stacked):
            send.wait_send()
            recv.wait_recv()

    outs = pl.pallas_call(
        body, name=name,
        out_shape=tuple(pltpu.HBM(a.shape, a.dtype) for a in srcs + lands),
        in_specs=[_HBM] * (2 * n) + [_SEM, _SEM, pl.BlockSpec(memory_space=pl.ANY)],
        out_specs=[_HBM] * (2 * n),
        input_output_aliases={i: i for i in range(2 * n)},
        compiler_params=pltpu.CompilerParams(has_side_effects=_EFFECT),
    )(*srcs, *lands, send_sems, recv_sems, after)
    return outs[:n], outs[n:]


def _adamw(name, parts, w, m, v):
    rows, cols = w.shape
    cap = max(8, (LANES * 1024) // cols)
    tr = rows
    if rows > cap:
        tr = (cap // 8) * 8
        while rows % tr:
            tr -= 8
    c1 = 1.0 / (1.0 - ADAM_B1 ** ADAM_STEP)
    c2 = 1.0 / (1.0 - ADAM_B2 ** ADAM_STEP)

    def body(p_ref, w_ref, m_ref, v_ref, g_o, d_o, m_o, v_o):
        g = p_ref[0]
        for s in range(1, N_DEV):
            g = g + p_ref[s]
        mn = ADAM_B1 * m_ref[...] + (1.0 - ADAM_B1) * g
        vn = ADAM_B2 * v_ref[...] + (1.0 - ADAM_B2) * (g * g)
        g_o[...] = g
        m_o[...] = mn
        v_o[...] = vn
        d_o[...] = -ADAM_LR * ((mn * c1) / (jnp.sqrt(vn * c2) + ADAM_EPS) + ADAM_WD * w_ref[...])

    blk = pl.BlockSpec((tr, cols), lambda i: (i, 0))
    return _call(body, name=name, grid=(rows // tr,),
                 in_specs=[pl.BlockSpec((N_DEV, tr, cols), lambda i: (0, i, 0)), blk, blk, blk],
                 out_specs=[blk] * 4, out_shape=[_sds((rows, cols), F32)] * 4)(parts, w, m, v)


def _rope_tables(pos_col, inv_freq):
    T = pos_col.shape[0]
    tm = _tile(T, 1024)

    def body(p_ref, f_ref, c_o, sa_o, sb_o):
        ang = p_ref[...].astype(F32) * f_ref[...]
        lane = lax.broadcasted_iota(jnp.int32, ang.shape, 1)
        cs, sn = jnp.cos(ang), jnp.sin(ang)
        c_o[...] = jnp.where(lane < D_ROPE, cs, 0.0)
        sa_o[...] = jnp.where(lane < D_ROPE // 2, -sn, 0.0)
        sb_o[...] = jnp.where((lane >= D_ROPE // 2) & (lane < D_ROPE), sn, 0.0)

    blk = pl.BlockSpec((tm, LANES), lambda i: (i, 0))
    return _call(body, name="rope_tables", grid=(T // tm,),
                 in_specs=[pl.BlockSpec((tm, 1), lambda i: (i, 0)), pl.BlockSpec((1, LANES), lambda i: (0, 0))],
                 out_specs=[blk] * 3, out_shape=[_sds((T, LANES), F32)] * 3)(pos_col, inv_freq)


def _ln_in(x, g, b):
    T, D = x.shape
    tm = _tile(T, 512)

    def body(x_ref, g_ref, b_ref, o32, o16):
        xhat, _ = _ln_stats(x_ref[...])
        y = xhat * g_ref[...] + b_ref[...]
        o32[...] = y
        o16[...] = y.astype(BF16)

    blk = pl.BlockSpec((tm, D), lambda i: (i, 0))
    vec = pl.BlockSpec((1, D), lambda i: (0, 0))
    return _call(body, name="ln_in", grid=(T // tm,), in_specs=[blk, vec, vec], out_specs=[blk, blk],
                 out_shape=[_sds((T, D), F32), _sds((T, D), BF16)])(x, g, b)


def _mid(h, g_cq, g_ckv, tabs, C):
    T = h.shape[0]
    tm = _tile(T, 256)
    cq_blk, kr_blk = (2 * C) // LORA, (2 * C + 2 * LORA) // LANES

    def body(a_ref, gt_ref, cq_ref, ckv_ref, kr_ref, gq_ref, gkv_ref, cp, sa, sb, u_o, cqn_o, ckvn_o, kr_o):
        u_o[...] = a_ref[...] * _sigmoid(gt_ref[...])
        cqn_o[...] = _rms(cq_ref[...], gq_ref[...]).astype(BF16)
        ckvn_o[...] = _rms(ckv_ref[...], gkv_ref[...]).astype(BF16)
        kr_o[...] = _rope(kr_ref[...], cp[...], sa[...], sb[...]).astype(BF16)

    def col(w, j):
        return pl.BlockSpec((tm, w), lambda i: (i, j))

    vec = pl.BlockSpec((1, LORA), lambda i: (0, 0))
    return _call(body, name="mid_norm_glu", grid=(T // tm,),
                 in_specs=[col(C, 0), col(C, 1), col(LORA, cq_blk), col(LORA, cq_blk + 1), col(LANES, kr_blk),
                           vec, vec, col(LANES, 0), col(LANES, 0), col(LANES, 0)],
                 out_specs=[col(C, 0), col(LORA, 0), col(LORA, 0), col(LANES, 0)],
                 out_shape=[_sds((T, C), F32), _sds((T, LORA), BF16), _sds((T, LORA), BF16), _sds((T, LANES), BF16)],
                 )(h, h, h, h, h, g_cq, g_ckv, *tabs)


def _q_proj(cqn, w_uq_p, tabs):
    T = cqn.shape[0]
    tm = _tile(T, HEAD_ROWS)

    def body(c_ref, w_ref, cp, sa, sb, o_ref):
        q = jnp.dot(c_ref[...], w_ref[...], preferred_element_type=F32)
        o_ref[:, :D_NOPE] = (q[:, :D_NOPE] * Q_SCALE).astype(BF16)
        o_ref[:, D_NOPE:] = (_rope(q[:, D_NOPE:], cp[...], sa[...], sb[...]) * Q_SCALE).astype(BF16)

    tab = pl.BlockSpec((tm, LANES), lambda i, h: (i, 0))
    return _call(body, name="q_proj_rope", grid=(T // tm, N_HEADS),
                 in_specs=[pl.BlockSpec((tm, LORA), lambda i, h: (i, 0)),
                           pl.BlockSpec((None, LORA, D_HEAD_PAD), lambda i, h: (h, 0, 0)), tab, tab, tab],
                 out_specs=pl.BlockSpec((None, tm, D_HEAD_PAD), lambda i, h: (h, i, 0)),
                 out_shape=_sds((N_HEADS, T, D_HEAD_PAD), BF16))(cqn, w_uq_p, *tabs)


def _kv_proj(ckvn, w_ukv, kr):
    T = ckvn.shape[0]
    tm = _tile(T, HEAD_ROWS)

    def body(c_ref, w_ref, kr_ref, k_o, v_o):
        kv = jnp.dot(c_ref[...], w_ref[...], preferred_element_type=F32)
        k_o[:, :D_NOPE] = kv[:, :D_NOPE].astype(BF16)
        k_o[:, D_NOPE:] = kr_ref[...]
        v_o[:, :D_V] = kv[:, D_NOPE:].astype(BF16)
        v_o[:, D_V:] = jnp.ones((tm, D_V), BF16)

    return _call(body, name="kv_proj", grid=(T // tm, N_HEADS),
                 in_specs=[pl.BlockSpec((tm, LORA), lambda i, h: (i, 0)),
                           pl.BlockSpec((None, LORA, D_NOPE + D_V), lambda i, h: (h, 0, 0)),
                           pl.BlockSpec((tm, LANES), lambda i, h: (i, 0))],
                 out_specs=[pl.BlockSpec((None, tm, D_HEAD_PAD), lambda i, h: (h, i, 0)),
                            pl.BlockSpec((None, tm, 2 * D_V), lambda i, h: (h, i, 0))],
                 out_shape=[_sds((N_HEADS, T, D_HEAD_PAD), BF16), _sds((N_HEADS, T, 2 * D_V), BF16)])(ckvn, w_ukv, kr)


def _flash_fwd(q, k, v1, out_cols):
    _, T, _ = q.shape
    tq, tk = _tile(T, FLASH_TQ), _tile(T, FLASH_TK)
    nkv, reps = T // tk, tk // LANES

    def body(q_ref, k_ref, v_ref, o_ref, lse_ref, m_sc, acc_sc):
        m_sc[...] = jnp.full_like(m_sc, -jnp.inf)
        acc_sc[...] = jnp.zeros_like(acc_sc)
        qv = q_ref[...]

        def rows(j):
            return pl.ds(pl.multiple_of(j * tk, tk), tk)

        def scores(j):
            return lax.dot_general(qv, k_ref[rows(j), :], NT, preferred_element_type=F32)

        def update(s, j):
            m_prev = m_sc[...]
            m_new = jnp.maximum(m_prev, jnp.max(s, axis=1, keepdims=True))
            a = jnp.exp2(m_prev - m_new)
            p = jnp.exp2(s - jnp.tile(m_new, (1, reps)))
            pv = jnp.dot(p.astype(BF16), v_ref[rows(j), :], preferred_element_type=F32)
            acc_sc[...] = jnp.tile(a, (1, 2)) * acc_sc[...] + pv
            m_sc[...] = m_new

        def step(j, carry):
            update(scores(j), j)
            return carry

        lax.fori_loop(0, nkv, step, 0, unroll=FLASH_UNROLL_FWD if nkv % FLASH_UNROLL_FWD == 0 else 1)
        acc = acc_sc[...]
        l = acc[:, D_V:]
        o_ref[...] = (acc[:, :D_V] / l).astype(BF16)
        lse_ref[...] = m_sc[...] + jnp.log(l) * LOG2_E

    return _call(body, name="flash_fwd", grid=(N_HEADS, T // tq),
                 in_specs=[pl.BlockSpec((None, tq, D_HEAD_PAD), lambda h, i: (h, i, 0)),
                           pl.BlockSpec((None, T, D_HEAD_PAD), lambda h, i: (h, 0, 0)),
                           pl.BlockSpec((None, T, 2 * D_V), lambda h, i: (h, 0, 0))],
                 out_specs=[pl.BlockSpec((tq, D_V), lambda h, i: (i, h)),
                            pl.BlockSpec((None, tq, LANES), lambda h, i: (h, i, 0))],
                 out_shape=[_sds((T, out_cols), BF16), _sds((N_HEADS, T, LANES), F32)],
                 scratch=[pltpu.VMEM((tq, LANES), F32), pltpu.VMEM((tq, 2 * D_V), F32)])(q, k, v1)


def _halo_specs(tm, cb, n_t):
    r = tm // HALO
    return [pl.BlockSpec((HALO, cb), lambda jc, i: (jnp.maximum(i * r - 1, 0), jc)),
            pl.BlockSpec((tm, cb), lambda jc, i: (i, jc)),
            pl.BlockSpec((HALO, cb), lambda jc, i: (jnp.minimum((i + 1) * r, n_t * r - 1), jc))]


def _fill_ext(ext, prev_ref, cur_ref, next_ref, i, n_t, tm):
    ext[0:HALO, :] = jnp.where(i > 0, prev_ref[...], 0.0)
    ext[HALO:HALO + tm, :] = cur_ref[...]
    ext[HALO + tm:, :] = jnp.where(i < n_t - 1, next_ref[...], 0.0)


HEAD_ROWS = 2048
FLASH_TQ = 1024
FLASH_TK = 512
FLASH_UNROLL_FWD = 16
FLASH_UNROLL_BWD = 16
EPILOGUE_ROWS = 128
CONV_ROWS = 64


def _fill_shifted(shifted, ext):
    rows = shifted.shape[1]
    for s in range(1, 8):
        shifted[s, :, :] = ext[s:s + rows, :]


def _window(ext, shifted, start, rows):
    s, base = start % 8, start - start % 8
    return ext[base:base + rows, :] if s == 0 else shifted[s, base:base + rows, :]


def _conv_fwd(u, w_pad, bias):
    T, C = u.shape
    tm, cb = _tile(T, 256), _tile(C, 256)
    n_t = T // tm
    rb = min(CONV_ROWS, tm)

    def body(up, uc, un, w_ref, b_ref, c_o, ext, shifted):
        i = pl.program_id(1)
        _fill_ext(ext, up, uc, un, i, n_t, tm)
        _fill_shifted(shifted, ext)
        for r0 in range(0, tm, rb):
            acc = jnp.zeros((rb, cb), F32) + b_ref[...]
            for k in range(CONV_W):
                acc = acc + w_ref[k:k + 1, :] * _window(ext, shifted, r0 + k + 1, rb)
            c_o[r0:r0 + rb, :] = acc

    return _call(body, name="conv_fwd", grid=(C // cb, n_t),
                 in_specs=_halo_specs(tm, cb, n_t) + [pl.BlockSpec((CONV_W_PAD, cb), lambda jc, i: (0, jc)),
                                                      pl.BlockSpec((1, cb), lambda jc, i: (0, jc))],
                 out_specs=pl.BlockSpec((tm, cb), lambda jc, i: (i, jc)),
                 out_shape=_sds((T, C), F32),
                 scratch=[pltpu.VMEM((tm + 2 * HALO, cb), F32),
                          pltpu.VMEM((8, tm + 2 * HALO - 8, cb), F32)])(u, u, u, w_pad, bias)


def _conv_post(c, g, b, cat):
    T, C = c.shape
    assert cat.shape == (T, 2 * C)
    tm = _tile(T, 512)

    def body(c_ref, g_ref, b_ref, cat_ref, o_ref):
        xhat, _ = _ln_stats(c_ref[...])
        y = xhat * g_ref[...] + b_ref[...]
        o_ref[...] = (y * _sigmoid(y)).astype(BF16)

    blk = pl.BlockSpec((tm, C), lambda i: (i, 0))
    vec = pl.BlockSpec((1, C), lambda i: (0, 0))
    return _call(body, name="conv_ln_silu", grid=(T // tm,),
                 in_specs=[blk, vec, vec, pl.BlockSpec(memory_space=pl.ANY)],
                 out_specs=pl.BlockSpec((tm, C), lambda i: (i, 1)),
                 out_shape=_sds(cat.shape, BF16), aliases={3: 0})(c, g, b, cat)


def _conv_post_bwd(dcat, c, g, b):
    T, C = c.shape
    tm = _tile(T, 512)

    def body(d_ref, c_ref, g_ref, b_ref, dc_o, dg_o, db_o):
        first = pl.program_id(0) == 0
        xhat, rstd = _ln_stats(c_ref[...])
        y = xhat * g_ref[...] + b_ref[...]
        sg = _sigmoid(y)
        dy = d_ref[...] * (sg * (1.0 + y * (1.0 - sg)))
        _acc_out(dg_o, _colsum(dy * xhat), first)
        _acc_out(db_o, _colsum(dy), first)
        dc_o[...] = _ln_bwd(dy, xhat, rstd, g_ref[...])

    blk = pl.BlockSpec((tm, C), lambda i: (i, 0))
    vec = pl.BlockSpec((1, C), lambda i: (0, 0))
    return _call(body, name="conv_ln_silu_bwd", grid=(T // tm,),
                 in_specs=[pl.BlockSpec((tm, C), lambda i: (i, 1)), blk, vec, vec],
                 out_specs=[blk, vec, vec],
                 out_shape=[_sds((T, C), F32), _sds((1, C), F32), _sds((1, C), F32)])(dcat, c, g, b)


def _conv_bwd(dc, u, w_pad):
    T, C = u.shape
    tm, cb = _tile(T, 256), _tile(C, 256)
    n_t = T // tm
    rb = min(CONV_ROWS, tm)

    def body(dp, dcur, dn, uc, w_ref, du_o, dw_o, db_o, dext, shifted, dw_sc):
        i = pl.program_id(1)
        _fill_ext(dext, dp, dcur, dn, i, n_t, tm)
        _fill_shifted(shifted, dext)

        @pl.when(i == 0)
        def _():
            dw_sc[...] = jnp.zeros_like(dw_sc)

        for r0 in range(0, tm, rb):
            acc = jnp.zeros((rb, cb), F32)
            u_here = uc[r0:r0 + rb, :]
            for k in range(CONV_W):
                win = _window(dext, shifted, r0 + 2 * HALO - 1 - k, rb)
                acc = acc + w_ref[k:k + 1, :] * win
                dw_sc[k] += jnp.sum((win * u_here).reshape(rb // 8, 8, cb), axis=0)
            du_o[r0:r0 + rb, :] = acc
        dw_sc[CONV_W] += jnp.sum(dcur[...].reshape(tm // 8, 8, cb), axis=0)

        @pl.when(i == n_t - 1)
        def _():
            red = jnp.sum(dw_sc[...], axis=1)
            row = lax.broadcasted_iota(jnp.int32, red.shape, 0)
            dw_o[...] = jnp.where(row < CONV_W, red, 0.0)
            db_o[...] = jnp.sum(jnp.where(row == CONV_W, red, 0.0), axis=0, keepdims=True)

    return _call(body, name="conv_bwd", grid=(C // cb, n_t),
                 in_specs=_halo_specs(tm, cb, n_t) + [pl.BlockSpec((tm, cb), lambda jc, i: (i, jc)),
                                                      pl.BlockSpec((CONV_W_PAD, cb), lambda jc, i: (0, jc))],
                 out_specs=[pl.BlockSpec((tm, cb), lambda jc, i: (i, jc)),
                            pl.BlockSpec((CONV_W_PAD, cb), lambda jc, i: (0, jc)),
                            pl.BlockSpec((1, cb), lambda jc, i: (0, jc))],
                 out_shape=[_sds((T, C), F32), _sds((CONV_W_PAD, C), F32), _sds((1, C), F32)],
                 scratch=[pltpu.VMEM((tm + 2 * HALO, cb), F32), pltpu.VMEM((8, tm + 2 * HALO - 8, cb), F32),
                          pltpu.VMEM((CONV_W_PAD, 8, cb), F32)])(dc, dc, dc, u, w_pad)


def _glu_bwd(du, h, C):
    T = du.shape[0]
    tm = _tile(T, 512)

    def body(du_ref, a_ref, gt_ref, o_ref):
        sg = _sigmoid(gt_ref[...])
        du_v = du_ref[...]
        o_ref[:, :C] = (du_v * sg).astype(BF16)
        o_ref[:, C:] = (du_v * a_ref[...] * sg * (1.0 - sg)).astype(BF16)

    return _call(body, name="glu_bwd", grid=(T // tm,),
                 in_specs=[pl.BlockSpec((tm, C), lambda i: (i, 0)), pl.BlockSpec((tm, C), lambda i: (i, 0)),
                           pl.BlockSpec((tm, C), lambda i: (i, 1))],
                 out_specs=pl.BlockSpec((tm, 2 * C), lambda i: (i, 0)),
                 out_shape=_sds(h.shape, BF16))(du, h, h)


def _attn_delta(dcat, attn):
    T = attn.shape[0]
    tm = _tile(T, HEAD_ROWS)

    def body(d_ref, o_ref, dl_o, dob_o):
        d = d_ref[...]
        dl = jnp.sum(d * o_ref[...].astype(F32), axis=1, keepdims=True)
        dl_o[...] = jnp.broadcast_to(dl, (tm, LANES))
        dob_o[...] = d.astype(BF16)

    blk = pl.BlockSpec((tm, D_V), lambda i, h: (i, h))
    hblk = pl.BlockSpec((None, tm, D_V), lambda i, h: (h, i, 0))
    return _call(body, name="attn_delta", grid=(T // tm, N_HEADS), in_specs=[blk, blk], out_specs=[hblk, hblk],
                 out_shape=[_sds((N_HEADS, T, LANES), F32), _sds((N_HEADS, T, D_V), BF16)])(dcat, attn)


def _flash_bwd(q, k, v1, do, lse, delta):
    _, T, _ = q.shape
    tq, tk = _tile(T, FLASH_TQ), _tile(T, FLASH_TK)
    nkv, reps = T // tk, tk // LANES

    def body(q_ref, do_ref, lse_ref, dl_ref, k_ref, v_ref, dq_o, dk_o, dv_o, dq_sc):
        @pl.when(pl.program_id(1) == 0)
        def _():
            dk_o[...] = jnp.zeros_like(dk_o)
            dv_o[...] = jnp.zeros_like(dv_o)

        qv, dov = q_ref[...], do_ref[...]
        lse_t = jnp.tile(lse_ref[...], (1, reps))
        dl_t = jnp.tile(dl_ref[...], (1, reps))
        dq_sc[...] = jnp.zeros_like(dq_sc)

        def rows(j):
            return pl.ds(pl.multiple_of(j * tk, tk), tk)

        def scores(j):
            s = lax.dot_general(qv, k_ref[rows(j), :], NT, preferred_element_type=F32)
            dp = lax.dot_general(dov, v_ref[rows(j), :D_V], NT, preferred_element_type=F32)
            return s, dp

        def update(s_dp, j):
            s, dp = s_dp
            p = jnp.exp2(s - lse_t)
            ds = (p * (dp - dl_t)).astype(BF16)
            dv_o[rows(j), :] += lax.dot_general(p.astype(BF16), dov, TN, preferred_element_type=F32)
            dk_o[rows(j), :] += lax.dot_general(ds, qv, TN, preferred_element_type=F32)
            dq_sc[...] += jnp.dot(ds, k_ref[rows(j), :], preferred_element_type=F32)

        def step(j, carry):
            update(scores(j), j)
            return carry

        lax.fori_loop(0, nkv, step, 0, unroll=FLASH_UNROLL_BWD if nkv % FLASH_UNROLL_BWD == 0 else 1)
        dq_o[...] = dq_sc[...]

    def tile(w):
        return pl.BlockSpec((None, tq, w), lambda h, i: (h, i, 0))

    def whole(w):
        return pl.BlockSpec((None, T, w), lambda h, i: (h, 0, 0))

    return _call(body, name="flash_bwd", grid=(N_HEADS, T // tq),
                 in_specs=[tile(D_HEAD_PAD), tile(D_V), tile(LANES), tile(LANES), whole(D_HEAD_PAD), whole(2 * D_V)],
                 out_specs=[tile(D_HEAD_PAD), whole(D_HEAD_PAD), whole(D_V)],
                 out_shape=[_sds((N_HEADS, T, D_HEAD_PAD), F32), _sds((N_HEADS, T, D_HEAD_PAD), F32),
                            _sds((N_HEADS, T, D_V), F32)],
                 scratch=[pltpu.VMEM((tq, D_HEAD_PAD), F32)])(q, do, lse, delta, k, v1)


def _dq_post(dq, tabs):
    _, T, _ = dq.shape
    tm = _tile(T, HEAD_ROWS)

    def body(d_ref, cp, sa, sb, o_ref):
        d = d_ref[...] * SCALE
        o_ref[:, :D_NOPE] = d[:, :D_NOPE].astype(BF16)
        o_ref[:, D_NOPE:] = _unrope(d[:, D_NOPE:], cp[...], sa[...], sb[...]).astype(BF16)

    blk = pl.BlockSpec((None, tm, D_HEAD_PAD), lambda i, h: (h, i, 0))
    tab = pl.BlockSpec((tm, LANES), lambda i, h: (i, 0))
    return _call(body, name="dq_unrope", grid=(T // tm, N_HEADS), in_specs=[blk, tab, tab, tab],
                 out_specs=pl.BlockSpec((tm, D_HEAD_PAD), lambda i, h: (i, h)),
                 out_shape=_sds((T, N_HEADS * D_HEAD_PAD), BF16))(dq, *tabs)


def _dk_post(dk, dv, tabs, dh, kr_blk):
    _, T, _ = dk.shape
    tm = _tile(T, 1024)

    def body(dk_ref, dv_ref, cp, sa, sb, dh_ref, dkv_o, dkr_o, sc):
        h = pl.program_id(1)
        d = dk_ref[...] * LN_2
        dkv_o[:, :D_NOPE] = d[:, :D_NOPE].astype(BF16)
        dkv_o[:, D_NOPE:] = dv_ref[...].astype(BF16)

        @pl.when(h == 0)
        def _():
            sc[...] = d[:, D_NOPE:]

        @pl.when(h > 0)
        def _():
            sc[...] += d[:, D_NOPE:]

        @pl.when(h == N_HEADS - 1)
        def _():
            dkr_o[...] = _unrope(sc[...], cp[...], sa[...], sb[...]).astype(BF16)

    tab = pl.BlockSpec((tm, LANES), lambda i, h: (i, 0))
    return _call(body, name="dk_unrope", grid=(T // tm, N_HEADS),
                 in_specs=[pl.BlockSpec((None, tm, D_HEAD_PAD), lambda i, h: (h, i, 0)),
                           pl.BlockSpec((None, tm, D_V), lambda i, h: (h, i, 0)), tab, tab, tab,
                           pl.BlockSpec(memory_space=pl.ANY)],
                 out_specs=[pl.BlockSpec((tm, D_HEAD_PAD), lambda i, h: (i, h)),
                            pl.BlockSpec((tm, LANES), lambda i, h: (i, kr_blk))],
                 out_shape=[_sds((T, N_HEADS * D_HEAD_PAD), BF16), _sds(dh.shape, BF16)],
                 scratch=[pltpu.VMEM((tm, LANES), F32)], aliases={5: 1})(dk, dv, *tabs, dh)


def _latent_bwd(name, dproj, w_heads, h, col_blk, g, dh):
    T, HD = dproj.shape
    tm = _tile(T, 1024)

    def ep(acc, ex, out, first):
        dx, dg = _rms_bwd(acc, ex[0][...], ex[1][...])
        out[0][...] = dx.astype(BF16)
        _acc_out(out[1], dg, first)

    return _matmul(name, dproj, w_heads, dims=NT, grid=(T // tm, 1, 1),
                   a_spec=pl.BlockSpec((tm, HD), lambda i, j, k: (i, 0)),
                   b_spec=pl.BlockSpec((LORA, HD), lambda i, j, k: (0, 0)),
                   acc_shape=(tm, LORA),
                   extras=[(h, pl.BlockSpec((tm, LORA), lambda i, j, k: (i, col_blk))),
                           (g, pl.BlockSpec((1, LORA), lambda i, j, k: (0, 0)))],
                   outs=[(_sds(dh.shape, BF16), pl.BlockSpec((tm, LORA), lambda i, j, k: (i, col_blk))),
                         (_sds((1, LORA), F32), pl.BlockSpec((1, LORA), lambda i, j, k: (0, 0)))],
                   epilogue=ep, into=dh)


def _head_weight_grad(name, latent, dproj):
    T = dproj.shape[0]
    tk = _tile(T, 2048)
    return _matmul(name, latent, dproj, dims=TN, grid=(N_HEADS, 1, T // tk),
                   a_spec=pl.BlockSpec((tk, LORA), lambda i, j, k: (k, 0)),
                   b_spec=pl.BlockSpec((tk, D_HEAD_PAD), lambda i, j, k: (k, i)),
                   acc_shape=(LORA, D_HEAD_PAD),
                   outs=[(_sds((N_HEADS, LORA, D_HEAD_PAD), F32),
                          pl.BlockSpec((None, LORA, D_HEAD_PAD), lambda i, j, k: (i, 0, 0)))],
                   epilogue=_store())[0]


def _weight_grad(name, a, b, tm_pref=1024, tn_pref=1024, stacked_cols=None):
    T, M = a.shape
    N = b.shape[1]
    tk = _tile(T, 2048)
    tm = _tile(M, tm_pref)
    if stacked_cols is None:
        tn = _tile(N, tn_pref)
        out = (_sds((M, N), F32), pl.BlockSpec((tm, tn), lambda i, j, k: (i, j)))
    else:
        tn = _tile(stacked_cols, tn_pref)
        per = stacked_cols // tn
        out = (_sds((N // stacked_cols, M, stacked_cols), F32),
               pl.BlockSpec((None, tm, tn), lambda i, j, k: (j // per, i, j % per)))
    return _matmul(name, a, b, dims=TN, grid=(M // tm, N // tn, T // tk),
                   a_spec=pl.BlockSpec((tk, tm), lambda i, j, k: (k, i)),
                   b_spec=pl.BlockSpec((tk, tn), lambda i, j, k: (k, j)),
                   acc_shape=(tm, tn), outs=[out], epilogue=_store())[0]


def _small_names():
    return ["ln_in_g", "ln_in_b", "g_cq", "g_ckv", "conv_b", "g_conv_ln", "b_conv_ln", "g_ln1", "b_ln1", "g_ln2", "b_ln2"]


def _pack(vecs):
    flat = jnp.concatenate([v.reshape(-1) for v in vecs])
    assert flat.shape[0] % (8 * LANES) == 0
    return flat.reshape(-1, LANES)


def _unpack(packed, like):
    flat, out, off = packed.reshape(-1), [], 0
    for v in like:
        out.append(flat[off:off + v.size].reshape(v.shape))
        off += v.size
    return out


def kernel(x, positions, ln_in_g, ln_in_b, w_in, g_cq, w_uq, g_ckv, w_uk, w_uv, conv_w, conv_b, g_conv_ln, b_conv_ln, w_out, g_ln1, b_ln1, w_ff1, w_ff2, g_ln2, b_ln2, loss_target, m_ln_in_g, m_ln_in_b, m_w_in, m_g_cq, m_w_uq, m_g_ckv, m_w_uk, m_w_uv, m_conv_w, m_conv_b, m_g_conv_ln, m_b_conv_ln, m_w_out, m_g_ln1, m_b_ln1, m_w_ff1, m_w_ff2, m_g_ln2, m_b_ln2, v_ln_in_g, v_ln_in_b, v_w_in, v_g_cq, v_w_uq, v_g_ckv, v_w_uk, v_w_uv, v_conv_w, v_conv_b, v_g_conv_ln, v_b_conv_ln, v_w_out, v_g_ln1, v_b_ln1, v_w_ff1, v_w_ff2, v_g_ln2, v_b_ln2):
    args = dict(locals())
    T, D = x.shape[1], x.shape[2]
    C = D - N_HEADS * D_V
    Fs = w_ff1.shape[2]
    F = N_DEV * Fs
    n_in = N_DEV * w_in.shape[2]
    n_in_p = 2 * C + 2 * LORA + LANES
    assert n_in == 2 * LORA + D_ROPE + 2 * C and w_uq.shape[2] == D_QK and conv_w.shape[2] * N_DEV == C

    xs, tgt = x[0], loss_target[0]
    row = lambda v_: v_.reshape(1, -1)

    gather = lambda k_: [False] * k_
    ag_in = _exchange_start("ag_in_start", [w_in[0].astype(BF16).reshape(-1, LANES)], gather(1))
    ag_heads = _exchange_start("ag_heads_start", [w_uq[0].astype(BF16), w_uk[0].astype(BF16), w_uv[0].astype(BF16),
                                                  conv_w[0]], gather(4))
    ag_ff = _exchange_start("ag_ff_start", [w_out[0].astype(BF16), w_ff1[0].astype(BF16), w_ff2[0].astype(BF16)],
                            gather(3))
    started = ag_in[4] + ag_heads[4] + ag_ff[4]

    half = D_ROPE // 2
    inv_freq = ROPE_BASE ** (-jnp.arange(half, dtype=F32) * (2.0 / D_ROPE))
    inv_freq = jnp.tile(inv_freq, LANES // half).reshape(1, LANES)
    tabs = _rope_tables(positions.reshape(T, 1), inv_freq)

    x0, x0b = _ln_in(xs, row(ln_in_g) + started, row(ln_in_b))

    (g_w_in,) = _exchange_wait("ag_in_wait", ag_in, gather(1), after=x0b)
    w_in_f = jnp.transpose(g_w_in.reshape(N_DEV, D, n_in // N_DEV), (1, 0, 2)).reshape(D, n_in)
    s_cq, s_ckv, s_kr, s_a, s_g = 0, LORA, 2 * LORA, 2 * LORA + D_ROPE, 2 * LORA + D_ROPE + C
    w_in_p = jnp.concatenate([w_in_f[:, s_a:s_g], w_in_f[:, s_g:], w_in_f[:, s_cq:s_ckv], w_in_f[:, s_ckv:s_kr],
                              w_in_f[:, s_kr:s_a], jnp.zeros((D, LANES - D_ROPE), BF16)], axis=1)

    tm, tn = _tile(T, 1024), _tile(n_in_p, 640)
    h = _matmul("h_proj", x0b, w_in_p, dims=NN, grid=(T // tm, n_in_p // tn, 1),
                a_spec=pl.BlockSpec((tm, D), lambda i, j, k: (i, 0)),
                b_spec=pl.BlockSpec((D, tn), lambda i, j, k: (0, j)), acc_shape=(tm, tn),
                outs=[(_sds((T, n_in_p), F32), pl.BlockSpec((tm, tn), lambda i, j, k: (i, j)))],
                epilogue=_store())[0]

    u, cqn, ckvn, kr = _mid(h, g_cq, g_ckv, tabs, C)
    g_w_uq, g_w_uk, g_w_uv, g_conv_w = _exchange_wait("ag_heads_wait", ag_heads, gather(4), after=cqn)
    w_uq_p = jnp.pad(g_w_uq, ((0, 0), (0, 0), (0, D_HEAD_PAD - D_QK)))
    w_ukv = jnp.concatenate([g_w_uk, g_w_uv], axis=2)
    conv_w_f = jnp.pad(jnp.transpose(g_conv_w, (1, 0, 2)).reshape(CONV_W, C), ((0, CONV_W_PAD - CONV_W), (0, 0)))
    q = _q_proj(cqn, w_uq_p, tabs)
    kf, vf = _kv_proj(ckvn, w_ukv, kr)
    attn, lse = _flash_fwd(q, kf, vf, D)
    conv_c = _conv_fwd(u, conv_w_f, conv_b)
    cat = _conv_post(conv_c, g_conv_ln, b_conv_ln, attn)
    g_w_out, g_w_ff1, g_w_ff2 = _exchange_wait("ag_ff_wait", ag_ff, gather(3), after=cat)
    w_out_f = g_w_out.reshape(D, D)
    w_ff2_f = g_w_ff2.reshape(F, D)

    def ep_ln1(acc, ex, out, first):
        z1 = ALPHA * ex[0][...] + acc
        xhat, _ = _ln_stats(z1)
        x1 = xhat * ex[1][...] + ex[2][...]
        out[0][...] = z1
        out[1][...] = x1
        out[2][...] = x1.astype(BF16)

    tm = _tile(T, 256)
    rowblk = pl.BlockSpec((tm, D), lambda i, j, k: (i, 0))
    vecD = pl.BlockSpec((1, D), lambda i, j, k: (0, 0))
    z1, x1, x1b = _matmul("mix_ln1", cat, w_out_f, dims=NN, grid=(T // tm, 1, 1), a_spec=rowblk,
                          b_spec=pl.BlockSpec((D, D), lambda i, j, k: (0, 0)), acc_shape=(tm, D),
                          extras=[(x0, rowblk), (g_ln1, vecD), (b_ln1, vecD)],
                          outs=[(_sds((T, D), F32), rowblk), (_sds((T, D), F32), rowblk), (_sds((T, D), BF16), rowblk)],
                          epilogue=ep_ln1, ep_rows=EPILOGUE_ROWS)

    def ep_ff1(acc, ex, out, first):
        r = jnp.maximum(acc, 0.0)
        out[0][...] = (r * r).astype(BF16)
        out[1][...] = r.astype(BF16)

    tm, tn = _tile(T, 1024), _tile(Fs, 1024)
    per = Fs // tn
    fblk = pl.BlockSpec((tm, tn), lambda i, j, k: (i, j))
    f_act, r_act = _matmul("ff1_relu2", x1b, g_w_ff1, dims=NN, grid=(T // tm, F // tn, 1),
                           a_spec=pl.BlockSpec((tm, D), lambda i, j, k: (i, 0)),
                           b_spec=pl.BlockSpec((None, D, tn), lambda i, j, k: (j // per, 0, j % per)),
                           acc_shape=(tm, tn), outs=[(_sds((T, F), BF16), fblk), (_sds((T, F), BF16), fblk)],
                           epilogue=ep_ff1)

    def ep_ln2(acc, ex, out, first):
        g2 = ex[2][...]
        z2 = ALPHA * ex[0][...] + acc
        xhat, rstd = _ln_stats(z2)
        err = xhat * g2 + ex[3][...] - ex[1][...]
        part = 0.5 * jnp.sum(jnp.mean(err * err, axis=-1, keepdims=True))
        _acc_out(out[2], jnp.zeros((8, LANES), F32) + part, first)
        dy = err * (1.0 / D)
        _acc_out(out[3], _colsum(dy * xhat), first)
        _acc_out(out[4], _colsum(dy), first)
        dz2 = _ln_bwd(dy, xhat, rstd, g2)
        out[0][...] = dz2
        out[1][...] = dz2.astype(BF16)

    tm, tk = _tile(T, 512), _tile(F, 1024)
    rowblk = pl.BlockSpec((tm, D), lambda i, j, k: (i, 0))
    dz2, dz2b, loss_blk, dg_ln2, db_ln2 = _matmul(
        "ff2_ln2_loss", f_act, w_ff2_f, dims=NN, grid=(T // tm, 1, F // tk),
        a_spec=pl.BlockSpec((tm, tk), lambda i, j, k: (i, k)), b_spec=pl.BlockSpec((tk, D), lambda i, j, k: (k, 0)),
        acc_shape=(tm, D), extras=[(x1, rowblk), (tgt, rowblk), (g_ln2, vecD), (b_ln2, vecD)],
        outs=[(_sds((T, D), F32), rowblk), (_sds((T, D), BF16), rowblk),
              (_sds((8, LANES), F32), pl.BlockSpec((8, LANES), lambda i, j, k: (0, 0))),
              (_sds((1, D), F32), vecD), (_sds((1, D), F32), vecD)],
        epilogue=ep_ln2, ep_rows=EPILOGUE_ROWS)
    loss = lax.psum(loss_blk[0, 0], ("x", "y", "c"))

    def ep_dpre(acc, ex, out, first):
        out[0][...] = (acc * (2.0 * ex[0][...].astype(F32))).astype(BF16)

    tm, tn = _tile(T, 1024), _tile(F, 1024)
    fblk = pl.BlockSpec((tm, tn), lambda i, j, k: (i, j))
    dpre = _matmul("ff2_dgrad", dz2b, w_ff2_f, dims=NT, grid=(T // tm, F // tn, 1),
                   a_spec=pl.BlockSpec((tm, D), lambda i, j, k: (i, 0)), b_spec=pl.BlockSpec((tn, D), lambda i, j, k: (j, 0)),
                   acc_shape=(tm, tn), extras=[(r_act, fblk)], outs=[(_sds((T, F), BF16), fblk)], epilogue=ep_dpre)[0]

    dw_ff2 = _weight_grad("ff2_wgrad", f_act, dz2b).reshape(N_DEV, Fs, D)
    dw_ff1 = _weight_grad("ff1_wgrad", x1b, dpre, stacked_cols=Fs)
    scatter = lambda k_: [True] * k_
    rs_ff = _exchange_start("rs_ff_start", [dw_ff2, dw_ff1], scatter(2))

    def ep_ln1_bwd(acc, ex, out, first):
        dx1 = ALPHA * ex[0][...] + acc
        xhat, rstd = _ln_stats(ex[1][...])
        _acc_out(out[2], _colsum(dx1 * xhat), first)
        _acc_out(out[3], _colsum(dx1), first)
        dz1 = _ln_bwd(dx1, xhat, rstd, ex[2][...])
        out[0][...] = dz1
        out[1][...] = dz1.astype(BF16)

    tm, tk = _tile(T, 512), _tile(Fs, 1024)
    per = Fs // tk
    rowblk = pl.BlockSpec((tm, D), lambda i, j, k: (i, 0))
    dz1, dz1b, dg_ln1, db_ln1 = _matmul(
        "ff1_dgrad_ln1_bwd", dpre, g_w_ff1, dims=NT, grid=(T // tm, 1, F // tk),
        a_spec=pl.BlockSpec((tm, tk), lambda i, j, k: (i, k)),
        b_spec=pl.BlockSpec((None, D, tk), lambda i, j, k: (k // per, 0, k % per)),
        acc_shape=(tm, D), extras=[(dz2, rowblk), (z1, rowblk), (g_ln1 + rs_ff[4], vecD)],
        outs=[(_sds((T, D), F32), rowblk), (_sds((T, D), BF16), rowblk), (_sds((1, D), F32), vecD), (_sds((1, D), F32), vecD)],
        epilogue=ep_ln1_bwd, ep_rows=EPILOGUE_ROWS)

    dw_out = _weight_grad("out_wgrad", cat, dz1b).reshape(N_DEV, D // N_DEV, D)
    rs_out = _exchange_start("rs_out_start", [dw_out], scatter(1))
    tm, tn = _tile(T, 1024), _tile(D, 1024)
    dcat = _matmul("out_dgrad", dz1b, w_out_f, dims=NT, grid=(T // tm, D // tn, 1),
                   a_spec=pl.BlockSpec((tm, D), lambda i, j, k: (i, 0)), b_spec=pl.BlockSpec((tn, D), lambda i, j, k: (j, 0)),
                   acc_shape=(tm, tn), outs=[(_sds((T, D), F32), pl.BlockSpec((tm, tn), lambda i, j, k: (i, j)))],
                   epilogue=_store())[0]

    dc, dg_conv_ln, db_conv_ln = _conv_post_bwd(dcat, conv_c, g_conv_ln + rs_out[4], b_conv_ln)
    du, dconv_w_p, dconv_b = _conv_bwd(dc, u, conv_w_f)
    dh = _glu_bwd(du, h, C)

    delta, do_heads = _attn_delta(dcat, cat)
    dq, dk, dv = _flash_bwd(q, kf, vf, do_heads, lse, delta)
    dq_raw = _dq_post(dq, tabs)
    cq_blk = (2 * C) // LORA
    dkv, dh = _dk_post(dk, dv, tabs, dh, (2 * C + 2 * LORA) // LANES)
    by_rank = lambda w_: jnp.transpose(w_, (1, 0, 2)).reshape(LORA, N_HEADS * D_HEAD_PAD)
    dh, dg_cq = _latent_bwd("q_dgrad_rms_bwd", dq_raw, by_rank(w_uq_p), h, cq_blk, g_cq, dh)
    dh, dg_ckv = _latent_bwd("kv_dgrad_rms_bwd", dkv, by_rank(w_ukv), h, cq_blk + 1, g_ckv, dh)

    dw_in_p = _weight_grad("in_wgrad", x0b, dh, tn_pref=640)
    dw_in_f = jnp.concatenate([dw_in_p[:, 2 * C:2 * C + 2 * LORA + D_ROPE], dw_in_p[:, :2 * C]], axis=1)
    dw_in = jnp.transpose(dw_in_f.reshape(D, N_DEV, n_in // N_DEV), (1, 0, 2)).reshape(N_DEV, -1, LANES)
    rs_in =_exchange_start("rs_in_start", [dw_in], scatter(1))

    def ep_ln_in_bwd(acc, ex, out, first):
        dx0 = ALPHA * ex[0][...] + acc
        xhat, rstd = _ln_stats(ex[1][...])
        _acc_out(out[1], _colsum(dx0 * xhat), first)
        _acc_out(out[2], _colsum(dx0), first)
        out[0][...] = _ln_bwd(dx0, xhat, rstd, ex[2][...])

    tm, tk = _tile(T, 512), _tile(n_in_p, 640)
    rowblk = pl.BlockSpec((tm, D), lambda i, j, k: (i, 0))
    grad_x, dg_ln_in, db_ln_in = _matmul(
        "in_dgrad_ln_in_bwd", dh, w_in_p, dims=NT, grid=(T // tm, 1, n_in_p // tk),
        a_spec=pl.BlockSpec((tm, tk), lambda i, j, k: (i, k)), b_spec=pl.BlockSpec((D, tk), lambda i, j, k: (0, k)),
        acc_shape=(tm, D), extras=[(dz1, rowblk), (xs, rowblk), (row(ln_in_g) + rs_in[4], vecD)],
        outs=[(_sds((T, D), F32), rowblk), (_sds((1, D), F32), vecD), (_sds((1, D), F32), vecD)],
        epilogue=ep_ln_in_bwd, ep_rows=EPILOGUE_ROWS)

    dw_uq = _head_weight_grad("uq_wgrad", cqn, dq_raw)[:, :, :D_QK]
    dw_ukv = _head_weight_grad("ukv_wgrad", ckvn, dkv)
    dw_uk, dw_uv = dw_ukv[:, :, :D_NOPE], dw_ukv[:, :, D_NOPE:]
    dconv_w = jnp.transpose(dconv_w_p[:CONV_W].reshape(CONV_W, N_DEV, C // N_DEV), (1, 0, 2))
    rs_heads = _exchange_start("rs_heads_start", [dw_uq, dw_uk, dw_uv, dconv_w], scatter(4))

    small = dict(ln_in_g=dg_ln_in, ln_in_b=db_ln_in, g_cq=dg_cq, g_ckv=dg_ckv, conv_b=dconv_b, g_conv_ln=dg_conv_ln,
                 b_conv_ln=db_conv_ln, g_ln1=dg_ln1, b_ln1=db_ln1, g_ln2=dg_ln2, b_ln2=db_ln2)
    names = _small_names()
    rs_small = _exchange_start("rs_small_start", [_pack([small[n] for n in names])], gather(1))
    res = {}

    def update(group, parts, after):
        last = after
        for n, p in zip(group, parts):
            shard = lambda a: a[0].reshape(p.shape[1:])
            outs_n = _adamw("adamw_" + n, p, shard(args[n]), shard(args["m_" + n]), shard(args["v_" + n]))
            res[n] = [o.reshape(args[n].shape) for o in outs_n]
            last = outs_n[0]
        return last

    done = update(["w_ff2", "w_ff1"], _exchange_wait("rs_ff_wait", rs_ff, scatter(2), after=grad_x), grad_x)
    done = update(["w_out"], _exchange_wait("rs_out_wait", rs_out, scatter(1), after=done), done)
    done = update(["w_in"], _exchange_wait("rs_in_wait", rs_in, scatter(1), after=done), done)
    done = update(["w_uq", "w_uk", "w_uv", "conv_w"],
                  _exchange_wait("rs_heads_wait", rs_heads, scatter(4), after=done), done)
    (small_parts,) = _exchange_wait("rs_small_wait", rs_small, gather(1), after=done)
    packed = _adamw("adamw_small", small_parts, _pack([args[n] for n in names]), _pack([args["m_" + n] for n in names]),
                    _pack([args["v_" + n] for n in names]))
    like = [args[n] for n in names]
    unpacked = [_unpack(p, like) for p in packed]
    for i, n in enumerate(names):
        res[n] = [unpacked[kind][i] for kind in range(4)]

    order = ["ln_in_g", "ln_in_b", "w_in", "g_cq", "w_uq", "g_ckv", "w_uk", "w_uv", "conv_w", "conv_b", "g_conv_ln",
             "b_conv_ln", "w_out", "g_ln1", "b_ln1", "w_ff1", "w_ff2", "g_ln2", "b_ln2"]
    outs = [loss, grad_x.reshape(x.shape)]
    for kind in range(4):
        outs += [res[n][kind] for n in order]
    return tuple(outs)
```
